```python
import math
import jax, jax.numpy as jnp
from jax import lax
import numpy as np

D_MODEL = 1024
BATCH = 4
SEQ = 4096
DEPTH = 4
DEC_BATCH = 32
DEC_SEQ = 4
PAST_LEN = 8192
PAGE_SIZE = 128

N_HEADS = 8
HEAD_DIM = 64
E_A = N_HEADS * HEAD_DIM
IDX_HEADS = 8
IDX_DIM = 64
IDX_W_SCALE = 1.0 / math.sqrt(IDX_HEADS * IDX_DIM)
TOPK_MAX = 256
Q_BLOCK = 128
ROPE_THETA = 10000.0
CHUNK = 128
GMLP_GROUPS = 4
E_B = 512
GMLP_GROUP_W = E_B // GMLP_GROUPS
E_C = 512
CONV_W = 3
D_FF = 4 * D_MODEL
N_BRANCH = 3
EPS = 1e-6
OFF_K = E_A
OFF_V = 2 * E_A
OFF_QI = 3 * E_A
OFF_KI = OFF_QI + IDX_HEADS * IDX_DIM
OFF_WI = OFF_KI + IDX_DIM
OFF_B = OFF_WI + IDX_HEADS
OFF_C = OFF_B + 2 * E_B
OFF_G = OFF_C + 3 * E_C
N_IN = OFF_G + N_BRANCH * D_MODEL

kernel_name = 'hybrid_dsa_gmlp_shortconv_step'


def _rmsnorm(x, g):
    xf = x.astype(jnp.float32)
    y = xf * lax.rsqrt(jnp.mean(xf * xf, axis=-1, keepdims=True) + EPS)
    return (y * g.astype(jnp.float32)).astype(x.dtype)


def _layernorm(x, g, b):
    xf = x.astype(jnp.float32)
    mu = jnp.mean(xf, axis=-1, keepdims=True)
    xc = xf - mu
    y = xc * lax.rsqrt(jnp.mean(xc * xc, axis=-1, keepdims=True) + EPS)
    return (y * g.astype(jnp.float32) + b.astype(jnp.float32)).astype(x.dtype)


def _rope(x, pos):
    d = x.shape[-1]
    half = d // 2
    inv = ROPE_THETA ** (-jnp.arange(half, dtype=jnp.float32) * (2.0 / d))
    ang = pos[:, None] * inv[None, :]
    c = jnp.cos(ang)[:, None, :]
    s = jnp.sin(ang)[:, None, :]
    xf = x.astype(jnp.float32)
    x1, x2 = xf[..., :half], xf[..., half:]
    return jnp.concatenate([x1 * c - x2 * s, x1 * s + x2 * c], axis=-1).astype(x.dtype)


def _project(h, w_in, pos, ln_g, ln_b):
    b, s, _ = h.shape
    z = jnp.einsum('bsd,dn->bsn', h, w_in)
    q, k, v, qi, ki, wi, zb, zc, zg = jnp.split(
        z, [OFF_K, OFF_V, OFF_QI, OFF_KI, OFF_WI, OFF_B, OFF_C, OFF_G], axis=-1)
    q = _rope(q.reshape(b, s, N_HEADS, HEAD_DIM), pos)
    k = _rope(k.reshape(b, s, N_HEADS, HEAD_DIM), pos)
    v = v.reshape(b, s, N_HEADS, HEAD_DIM)
    qi = _rope(qi.reshape(b, s, IDX_HEADS, IDX_DIM), pos)
    ki = _rope(ki[:, :, None, :], pos)[:, :, 0, :]
    wi = wi * IDX_W_SCALE
    u, vg = jnp.split(jax.nn.gelu(zb), 2, axis=-1)
    vn = _layernorm(vg, ln_g, ln_b)
    cb, cc, ch = jnp.split(zc, 3, axis=-1)
    gates = jax.nn.sigmoid(zg.astype(jnp.float32)).astype(h.dtype).reshape(b, s, N_BRANCH, D_MODEL)
    return q, k, v, qi, ki, wi, u, vn, cb, cc, ch, gates


def _index_scores(qi, wi, ki):
    dots = jnp.einsum('bqhd,bsd->bqhs', qi.astype(jnp.float32), ki.astype(jnp.float32))
    return jnp.einsum('bqhs,bqh->bqs', jax.nn.relu(dots), wi.astype(jnp.float32))


def _sparse_attend(q, kg, vg, valid):
    logits = jnp.einsum('bqhd,bqkhd->bqhk', q.astype(jnp.float32), kg.astype(jnp.float32)) * (HEAD_DIM ** -0.5)
    logits = jnp.where(valid[:, :, None, :], logits, -jnp.inf)
    p = jax.nn.softmax(logits, axis=-1)
    return jnp.einsum('bqhk,bqkhd->bqhd', p, vg.astype(jnp.float32)).astype(q.dtype)


def _gather_rows(src, idx):
    return jax.vmap(lambda sb, ib: sb[ib])(src, idx)


def _attn_prompt(q, k, v, qi, ki, wi):
    b, s = q.shape[:2]
    topk = min(TOPK_MAX, s // 4)
    nb = s // Q_BLOCK
    key_pos = jnp.arange(s)

    def blocks(a):
        return jnp.moveaxis(a.reshape(b, nb, Q_BLOCK, *a.shape[2:]), 1, 0)

    def one_block(args):
        j, qb, qib, wib = args
        qpos = j * Q_BLOCK + jnp.arange(Q_BLOCK)
        sc = _index_scores(qib, wib, ki)
        sc = jnp.where(key_pos[None, None, :] <= qpos[None, :, None], sc, -jnp.inf)
        _, sel = lax.top_k(sc, topk)
        valid = sel <= qpos[None, :, None]
        return _sparse_attend(qb, _gather_rows(k, sel), _gather_rows(v, sel), valid)

    out = lax.map(one_block, (jnp.arange(nb), blocks(q), blocks(qi), blocks(wi)))
    return jnp.moveaxis(out, 0, 1).reshape(b, s, E_A)


def _attn_sample(q, k, v, qi, ki, wi, cache_k, cache_v, cache_idx_k, layer, page_table):
    db, t = q.shape[:2]
    page = cache_k.shape[2]
    past = page_table.shape[1] * page
    total = past + t
    topk = min(TOPK_MAX, total // 4)
    ki_past = cache_idx_k[layer, page_table].reshape(db, past, IDX_DIM)
    ki_all = jnp.concatenate([ki_past.astype(ki.dtype), ki], axis=1)
    qpos = past + jnp.arange(t)
    sc = _index_scores(qi, wi, ki_all)
    sc = jnp.where(jnp.arange(total)[None, None, :] <= qpos[None, :, None], sc, -jnp.inf)
    _, sel = lax.top_k(sc, topk)
    valid = sel <= qpos[None, :, None]
    is_past = (sel < past)[..., None, None]
    ps = jnp.clip(sel, 0, past - 1)
    phys = jax.vmap(lambda pt, i: pt[i])(page_table, ps // page)
    off = ps % page
    ns = jnp.clip(sel - past, 0, t - 1)
    kg = jnp.where(is_past, cache_k[layer, phys, off].astype(k.dtype), _gather_rows(k, ns))
    vg = jnp.where(is_past, cache_v[layer, phys, off].astype(v.dtype), _gather_rows(v, ns))
    return _sparse_attend(q, kg, vg, valid).reshape(db, t, E_A)


def _gmlp_mix(vn, ws, bs):
    b, s, e = vn.shape
    pad = (-s) % CHUNK
    vpad = jnp.pad(vn, ((0, 0), (0, pad), (0, 0)))
    nc = (s + pad) // CHUNK
    vc = vpad.reshape(b, nc, CHUNK, GMLP_GROUPS, GMLP_GROUP_W)
    causal = jnp.tril(jnp.ones((CHUNK, CHUNK), dtype=bool))
    wm = jnp.where(causal[None], ws, 0).astype(vn.dtype)
    mix = jnp.einsum('gts,bcsgd->bctgd', wm, vc) + jnp.swapaxes(bs, 0, 1).astype(vn.dtype)[None, None, :, :, None]
    return mix.reshape(b, s + pad, e)[:, :s]


def _short_conv(xpad, w, bias):
    t = xpad.shape[1] - (CONV_W - 1)
    y = xpad[:, 0:t] * w[0]
    for i in range(1, CONV_W):
        y = y + xpad[:, i:i + t] * w[i]
    return y + bias


def _merge(oa, ob, oc, gates, w_ba, w_bb, w_bc, w_o):
    m = (gates[:, :, 0] * jnp.einsum('bse,ed->bsd', oa, w_ba)
         + gates[:, :, 1] * jnp.einsum('bse,ed->bsd', ob, w_bb)
         + gates[:, :, 2] * jnp.einsum('bse,ed->bsd', oc, w_bc))
    return jnp.einsum('bsd,de->bse', m, w_o)


def _ffn(h, w_up, w_down):
    a = jax.nn.relu(jnp.einsum('bsd,df->bsf', h, w_up))
    return jnp.einsum('bsf,fd->bsd', a * a, w_down)


def setup_inputs(seed: int = 0) -> dict:
    key = jax.random.key(seed)
    ks = jax.random.split(key, 24)
    f32 = jnp.float32
    n_pages = PAST_LEN // PAGE_SIZE
    n_phys = (5 * DEC_BATCH * n_pages) // 4

    def nrm(k, shape, scale=1.0):
        return jax.random.normal(k, shape, f32) * scale

    perm = jax.random.permutation(ks[6], n_phys)
    page_table = perm[:DEC_BATCH * n_pages].reshape(DEC_BATCH, n_pages).astype(jnp.int32)
    return {
        'x_prompt': nrm(ks[0], (BATCH, SEQ, D_MODEL)),
        'x_sample': nrm(ks[1], (DEC_BATCH, DEC_SEQ, D_MODEL)),
        'cache_k': nrm(ks[2], (DEPTH, n_phys, PAGE_SIZE, N_HEADS, HEAD_DIM)),
        'cache_v': nrm(ks[3], (DEPTH, n_phys, PAGE_SIZE, N_HEADS, HEAD_DIM)),
        'cache_idx_k': nrm(ks[4], (DEPTH, n_phys, PAGE_SIZE, IDX_DIM)),
        'state_conv': nrm(ks[5], (DEPTH, DEC_BATCH, CONV_W - 1, E_C)),
        'page_table': page_table,
        'norm_mix_pre': 1.0 + nrm(ks[7], (DEPTH, D_MODEL), 0.05),
        'norm_mix_post': 1.0 + nrm(ks[8], (DEPTH, D_MODEL), 0.05),
        'norm_ffn_pre': 1.0 + nrm(ks[9], (DEPTH, D_MODEL), 0.05),
        'norm_ffn_post': 1.0 + nrm(ks[10], (DEPTH, D_MODEL), 0.05),
        'w_in': nrm(ks[11], (DEPTH, D_MODEL, N_IN), D_MODEL ** -0.5),
        'gmlp_ln_g': 1.0 + nrm(ks[12], (DEPTH, E_B), 0.05),
        'gmlp_ln_b': nrm(ks[13], (DEPTH, E_B), 0.01),
        'gmlp_ws': nrm(ks[14], (DEPTH, GMLP_GROUPS, CHUNK, CHUNK), CHUNK ** -0.5),
        'gmlp_bs': 1.0 + nrm(ks[15], (DEPTH, GMLP_GROUPS, CHUNK), 0.1),
        'conv_w': nrm(ks[16], (DEPTH, CONV_W, E_C), CONV_W ** -0.5),
        'conv_b': nrm(ks[17], (DEPTH, E_C), 0.01),
        'w_br_attn': nrm(ks[18], (DEPTH, E_A, D_MODEL), E_A ** -0.5),
        'w_br_gmlp': nrm(ks[19], (DEPTH, E_B, D_MODEL), E_B ** -0.5),
        'w_br_conv': nrm(ks[20], (DEPTH, E_C, D_MODEL), E_C ** -0.5),
        'w_out': nrm(ks[21], (DEPTH, D_MODEL, D_MODEL), D_MODEL ** -0.5),
        'w_ff_up': nrm(ks[22], (DEPTH, D_MODEL, D_FF), D_MODEL ** -0.5),
        'w_ff_down': nrm(ks[23], (DEPTH, D_FF, D_MODEL), D_FF ** -0.5),
    }


def reference(x_prompt, x_sample, cache_k, cache_v, cache_idx_k, state_conv, page_table,
              norm_mix_pre, norm_mix_post, norm_ffn_pre, norm_ffn_post, w_in,
              gmlp_ln_g, gmlp_ln_b, gmlp_ws, gmlp_bs, conv_w, conv_b,
              w_br_attn, w_br_gmlp, w_br_conv, w_out, w_ff_up, w_ff_down):
    seq = x_prompt.shape[1]
    dec_seq = x_sample.shape[1]
    past = page_table.shape[1] * cache_k.shape[2]
    pos_p = jnp.arange(seq, dtype=jnp.float32)
    pos_s = jnp.arange(dec_seq, dtype=jnp.float32) + past
    xp, xs = x_prompt, x_sample
    kp_l, vp_l, ikp_l, cp_l = [], [], [], []
    ksm_l, vsm_l, iks_l, cs_l, gvs_l = [], [], [], [], []
    for l in range(DEPTH):
        hp = _rmsnorm(xp, norm_mix_pre[l])
        hs = _rmsnorm(xs, norm_mix_pre[l])
        (q_p, k_p, v_p, qi_p, ki_p, wi_p, u_p, vn_p, cb_p, cc_p, ch_p, g_p) = _project(
            hp, w_in[l], pos_p, gmlp_ln_g[l], gmlp_ln_b[l])
        (q_s, k_s, v_s, qi_s, ki_s, wi_s, u_s, vn_s, cb_s, cc_s, ch_s, g_s) = _project(
            hs, w_in[l], pos_s, gmlp_ln_g[l], gmlp_ln_b[l])
        oa_p = _attn_prompt(q_p, k_p, v_p, qi_p, ki_p, wi_p)
        oa_s = _attn_sample(q_s, k_s, v_s, qi_s, ki_s, wi_s, cache_k, cache_v, cache_idx_k, l, page_table)
        ob_p = u_p * _gmlp_mix(vn_p, gmlp_ws[l], gmlp_bs[l])
        ob_s = u_s * _gmlp_mix(vn_s, gmlp_ws[l], gmlp_bs[l])
        cin_p = jnp.concatenate([jnp.zeros((hp.shape[0], CONV_W - 1, E_C), hp.dtype), cc_p * ch_p], axis=1)
        cin_s = jnp.concatenate([state_conv[l].astype(hs.dtype), cc_s * ch_s], axis=1)
        oc_p = cb_p * _short_conv(cin_p, conv_w[l], conv_b[l])
        oc_s = cb_s * _short_conv(cin_s, conv_w[l], conv_b[l])
        mp = _merge(oa_p, ob_p, oc_p, g_p, w_br_attn[l], w_br_gmlp[l], w_br_conv[l], w_out[l])
        ms = _merge(oa_s, ob_s, oc_s, g_s, w_br_attn[l], w_br_gmlp[l], w_br_conv[l], w_out[l])
        xp = xp + _rmsnorm(mp, norm_mix_post[l])
        xs = xs + _rmsnorm(ms, norm_mix_post[l])
        xp = xp + _rmsnorm(_ffn(_rmsnorm(xp, norm_ffn_pre[l]), w_ff_up[l], w_ff_down[l]), norm_ffn_post[l])
        xs = xs + _rmsnorm(_ffn(_rmsnorm(xs, norm_ffn_pre[l]), w_ff_up[l], w_ff_down[l]), norm_ffn_post[l])
        kp_l.append(k_p)
        vp_l.append(v_p)
        ikp_l.append(ki_p)
        cp_l.append(cin_p[:, -(CONV_W - 1):])
        ksm_l.append(k_s)
        vsm_l.append(v_s)
        iks_l.append(ki_s)
        cs_l.append(cin_s[:, -(CONV_W - 1):])
        gvs_l.append(vn_s)
    return (xp, xs, jnp.stack(kp_l), jnp.stack(vp_l), jnp.stack(ikp_l), jnp.stack(cp_l),
            jnp.stack(ksm_l), jnp.stack(vsm_l), jnp.stack(iks_l), jnp.stack(cs_l), jnp.stack(gvs_l))
```

```python
import functools
import math

import jax
import jax.numpy as jnp
from jax import lax
from jax.experimental import pallas as pl
from jax.experimental.pallas import tpu as pltpu

F32 = jnp.float32
BF16 = jnp.bfloat16

D_MODEL = 1024
N_HEADS = 8
HEAD_DIM = 64
E_A = N_HEADS * HEAD_DIM
IDX_HEADS = 8
IDX_DIM = 64
IDX_W_SCALE = 1.0 / math.sqrt(IDX_HEADS * IDX_DIM)
TOPK_MAX = 256
Q_BLOCK = 128
ROPE_THETA = 10000.0
CHUNK = 128
GMLP_GROUPS = 4
E_B = 512
E_C = 512
CONV_W = 3
D_FF = 4 * D_MODEL
EPS = 1e-6

OFF_KI = 3 * E_A + IDX_HEADS * IDX_DIM
OFF_B = OFF_KI + IDX_DIM + IDX_HEADS
LANE = 128
C_KW = OFF_KI
C_B = C_KW + LANE
C_C = C_B + 2 * E_B
C_G = C_C + 3 * E_C
N_PAD = C_G + 3 * D_MODEL

VMEM_LIMIT = 56 * 1024 * 1024
NEG_BIG = -1e30
KEY_NEG_INF = -2139095041
KEY_POS_INF = 2139095040
CK = 256
PAGES_PER_STEP = 8


def _rms(x, g):
    return x * lax.rsqrt(jnp.mean(x * x, axis=-1, keepdims=True) + EPS) * g


def _rope128(z, cos, sin):
    lane = lax.broadcasted_iota(jnp.int32, z.shape, 1)
    partner = jnp.where((lane % HEAD_DIM) < HEAD_DIM // 2,
                        pltpu.roll(z, LANE - HEAD_DIM // 2, 1),
                        pltpu.roll(z, HEAD_DIM // 2, 1))
    return z * cos + partner * sin


def _rope(z, cos, sin):
    return jnp.concatenate(
        [_rope128(z[:, i:i + LANE], cos, sin) for i in range(0, z.shape[1], LANE)], axis=1)


def _proj_kernel(*refs, sample, tm, tiles_per_seq):
    if sample:
        (x_ref, gpre_ref, w_ref, cos_ref, sin_ref, lng_ref, lnb_ref, wmix_ref, bmix_ref,
         cw_ref, cbias_ref, wbb_ref, wbc_ref, st_ref,
         q_ref, k_ref, v_ref, qi_ref, kw_ref, g0_ref, mbc_ref, cin_ref, vn_ref) = refs
    else:
        (x_ref, gpre_ref, w_ref, cos_ref, sin_ref, lng_ref, lnb_ref, wmix_ref, bmix_ref,
         cw_ref, cbias_ref, wbb_ref, wbc_ref,
         q_ref, k_ref, kT_ref, v_ref, vb_ref, qi_ref, kw_ref, kiT_ref, g0_ref, mbc_ref,
         ctail_ref, carry_ref) = refs

    h = _rms(x_ref[...], gpre_ref[...]).astype(BF16)
    cos = cos_ref[...]
    sin = sin_ref[...]

    z = jnp.dot(h, w_ref[:, 0:C_KW], preferred_element_type=F32)
    q_ref[...] = (_rope(z[:, 0:E_A], cos, sin) * (HEAD_DIM ** -0.5)).astype(BF16)
    k = _rope(z[:, E_A:2 * E_A], cos, sin)
    k_ref[...] = k
    v = z[:, 2 * E_A:3 * E_A]
    v_ref[...] = v
    qi_ref[...] = _rope(z[:, 3 * E_A:4 * E_A], cos, sin).astype(BF16)

    zk = jnp.dot(h, w_ref[:, C_KW:C_B], preferred_element_type=F32)
    lane = lax.broadcasted_iota(jnp.int32, zk.shape, 1)
    kw = jnp.where(lane < IDX_DIM, _rope128(zk, cos, sin), zk * IDX_W_SCALE)
    kw_ref[...] = kw
    if not sample:
        kT_ref[0] = k.T.astype(BF16)
        vb_ref[...] = v.astype(BF16)
        kiT_ref[0] = kw.T[0:IDX_DIM, :].astype(BF16)

    gl = jax.nn.gelu(jnp.dot(h, w_ref[:, C_B:C_C], preferred_element_type=F32))
    u = gl[:, 0:E_B]
    vg = gl[:, E_B:2 * E_B]
    mu = jnp.mean(vg, axis=-1, keepdims=True)
    vc = vg - mu
    vn = vc * lax.rsqrt(jnp.mean(vc * vc, axis=-1, keepdims=True) + EPS) * lng_ref[...] + lnb_ref[...]
    if sample:
        vn_ref[...] = vn
    vnb = vn.astype(BF16)
    gw = E_B // GMLP_GROUPS
    rows = []
    for c in range(tm // CHUNK):
        cols = []
        for g in range(GMLP_GROUPS):
            cols.append(jnp.dot(wmix_ref[g], vnb[c * CHUNK:(c + 1) * CHUNK, g * gw:(g + 1) * gw],
                                preferred_element_type=F32))
        rows.append(jnp.concatenate(cols, axis=1) + bmix_ref[...])
    mix = rows[0] if len(rows) == 1 else jnp.concatenate(rows, axis=0)
    ob = (u * mix).astype(BF16)

    zc = jnp.dot(h, w_ref[:, C_C:C_G], preferred_element_type=F32)
    cb = zc[:, 0:E_C]
    cin = zc[:, E_C:2 * E_C] * zc[:, 2 * E_C:3 * E_C]
    row = lax.broadcasted_iota(jnp.int32, cin.shape, 0)
    if sample:
        cin_ref[...] = cin
        i_in_seq = row % tiles_per_seq
        s1 = jnp.where(i_in_seq >= 1, pltpu.roll(cin, 1, 0), st_ref[:, 0:E_C])
        s2 = jnp.where(i_in_seq >= 2, pltpu.roll(cin, 2, 0), st_ref[:, E_C:2 * E_C])
    else:
        @pl.when(pl.program_id(0) % tiles_per_seq == 0)
        def _():
            carry_ref[...] = jnp.zeros_like(carry_ref)
        p2 = carry_ref[0:1, :]
        p1 = carry_ref[1:2, :]
        s1 = jnp.where(row >= 1, pltpu.roll(cin, 1, 0), p1)
        s2 = jnp.where(row >= 2, pltpu.roll(cin, 2, 0), jnp.where(row == 0, p2, p1))
        carry_ref[0:2, :] = cin[tm - 2:tm, :]
        ctail_ref[0] = cin[tm - 8:tm, :]
    y = s2 * cw_ref[0:1, :] + s1 * cw_ref[1:2, :] + cin * cw_ref[2:3, :] + cbias_ref[...]
    oc = (cb * y).astype(BF16)

    g0_ref[...] = jax.nn.sigmoid(
        jnp.dot(h, w_ref[:, C_G:C_G + D_MODEL], preferred_element_type=F32)).astype(BF16)
    g1 = jax.nn.sigmoid(jnp.dot(h, w_ref[:, C_G + D_MODEL:C_G + 2 * D_MODEL], preferred_element_type=F32))
    mbc = g1 * jnp.dot(ob, wbb_ref[...], preferred_element_type=F32)
    g2 = jax.nn.sigmoid(jnp.dot(h, w_ref[:, C_G + 2 * D_MODEL:N_PAD], preferred_element_type=F32))
    mbc_ref[...] = mbc + g2 * jnp.dot(oc, wbc_ref[...], preferred_element_type=F32)


def _const_spec(shape):
    nd = len(shape)
    return pl.BlockSpec(shape, lambda *_: (0,) * nd, pipeline_mode=pl.Buffered(1))


def _proj(x, gpre, w, cos, sin, lng, lnb, wmix, bmix, cw, cbias, wbb, wbc, *, sample, seq_len,
          st=None):
    m = x.shape[0]
    if sample:
        tm = m
        tiles_per_seq = seq_len
        n_seq = m // seq_len
    else:
        tm = 256
        tiles_per_seq = seq_len // tm
        n_seq = m // seq_len
    grid = (m // tm,)
    row = lambda n: pl.BlockSpec((tm, n), lambda i: (i, 0))
    if sample:
        pos_spec = pl.BlockSpec((tm, LANE), lambda i: (0, 0))
    else:
        pos_spec = pl.BlockSpec((tm, LANE), lambda i: (i % tiles_per_seq, 0))
    in_specs = [row(D_MODEL), _const_spec((1, D_MODEL)), _const_spec((D_MODEL, N_PAD)),
                pos_spec, pos_spec, _const_spec((1, E_B)), _const_spec((1, E_B)),
                _const_spec((GMLP_GROUPS, CHUNK, CHUNK)), _const_spec((CHUNK, E_B)),
                _const_spec((CONV_W, E_C)), _const_spec((1, E_C)),
                _const_spec((E_B, D_MODEL)), _const_spec((E_C, D_MODEL))]
    args = [x, gpre, w, cos, sin, lng, lnb, wmix, bmix, cw, cbias, wbb, wbc]
    sds = jax.ShapeDtypeStruct
    if sample:
        in_specs.append(row(2 * E_C))
        args.append(st)
        out_shape = [sds((m, E_A), BF16), sds((m, E_A), F32), sds((m, E_A), F32), sds((m, E_A), BF16),
                     sds((m, LANE), F32), sds((m, D_MODEL), BF16), sds((m, D_MODEL), F32),
                     sds((m, E_C), F32), sds((m, E_B), F32)]
        out_specs = [row(E_A), row(E_A), row(E_A), row(E_A), row(LANE), row(D_MODEL), row(D_MODEL),
                     row(E_C), row(E_B)]
        scratch = []
    else:
        tps = tiles_per_seq
        out_shape = [sds((m, E_A), BF16), sds((m, E_A), F32), sds((n_seq, E_A, seq_len), BF16),
                     sds((m, E_A), F32), sds((m, E_A), BF16), sds((m, E_A), BF16),
                     sds((m, LANE), F32), sds((n_seq, IDX_DIM, seq_len), BF16),
                     sds((m, D_MODEL), BF16), sds((m, D_MODEL), F32), sds((n_seq, 8, E_C), F32)]
        out_specs = [row(E_A), row(E_A),
                     pl.BlockSpec((1, E_A, tm), lambda i: (i // tps, 0, i % tps)),
                     row(E_A), row(E_A), row(E_A), row(LANE),
                     pl.BlockSpec((1, IDX_DIM, tm), lambda i: (i // tps, 0, i % tps)),
                     row(D_MODEL), row(D_MODEL),
                     pl.BlockSpec((1, 8, E_C), lambda i: (i // tps, 0, 0))]
        scratch = [pltpu.VMEM((8, E_C), F32)]
    return pl.pallas_call(
        functools.partial(_proj_kernel, sample=sample, tm=tm, tiles_per_seq=tiles_per_seq),
        grid=grid, in_specs=in_specs, out_specs=out_specs, out_shape=out_shape,
        scratch_shapes=scratch,
        compiler_params=pltpu.CompilerParams(dimension_semantics=("arbitrary",),
                                             vmem_limit_bytes=VMEM_LIMIT),
        name="proj_sample" if sample else "proj_prompt",
    )(*args)


def _key_to_f32(key):
    bits = key ^ ((key >> 31) & jnp.int32(0x7FFFFFFF))
    return lax.bitcast_convert_type(bits, F32)


def _topk_threshold(sc_ref, nck, k_row):
    rows = sc_ref.shape[0]
    k_f = k_row.astype(F32)

    def count(pred_fn):
        def body(c, acc):
            off = pl.multiple_of(c * CK, CK)
            s = sc_ref[:, pl.ds(off, CK)]
            for i in range(CK // LANE):
                acc = acc + jnp.where(pred_fn(s[:, i * LANE:(i + 1) * LANE]), 1.0, 0.0)
            return acc
        acc = lax.fori_loop(0, nck, body, jnp.zeros((rows, LANE), F32))
        return jnp.sum(acc, axis=1, keepdims=True)

    def bisect(_, lohi):
        lo, hi = lohi
        mid = (lo >> 1) + (hi >> 1) + (lo & hi & 1)
        midb = jnp.broadcast_to(_key_to_f32(mid), (rows, LANE))
        ge = count(lambda s: s >= midb) >= k_f
        return jnp.where(ge, mid, lo), jnp.where(ge, hi, mid)

    lo0 = jnp.full((rows, 1), KEY_NEG_INF, jnp.int32)
    hi0 = jnp.full((rows, 1), KEY_POS_INF, jnp.int32)
    lo, _ = lax.fori_loop(0, 32, bisect, (lo0, hi0))
    thr = _key_to_f32(lo)
    thrb = jnp.broadcast_to(thr, (rows, LANE))
    n_eq_take = k_f - count(lambda s: s > thrb)
    return thr, n_eq_take


def _select_mask(s, thr, n_eq_take, eq_seen, tri):
    eq = s == thr
    eqf = jnp.where(eq, 1.0, 0.0)
    prefix = jnp.dot(eqf.astype(BF16), tri, preferred_element_type=F32) + eq_seen
    sel = (s > thr) | (eq & (prefix <= n_eq_take))
    return sel, eq_seen + jnp.sum(eqf, axis=1, keepdims=True)


def _tri_inclusive(n):
    r = lax.broadcasted_iota(jnp.int32, (n, n), 0)
    c = lax.broadcasted_iota(jnp.int32, (n, n), 1)
    return jnp.where(r <= c, 1.0, 0.0).astype(BF16)


def _softmax_step(lg, sel, m, l):
    lg = jnp.where(sel, lg, -jnp.inf)
    m_new = jnp.maximum(m, jnp.max(lg, axis=1, keepdims=True))
    p = jnp.exp(lg - m_new)
    alpha = jnp.exp(m - m_new)
    return p, alpha, m_new, alpha * l + jnp.sum(p, axis=1, keepdims=True)


def _attn_prompt_kernel(q_ref, qi_ref, kw_ref, kiT_ref, kT_ref, v_ref, o_ref, sc_ref, acc_ref, *,
                        topk):
    j = pl.program_id(1)
    nck = (j * Q_BLOCK) // CK + 1
    qi_all = qi_ref[0]
    qh = jnp.concatenate([qi_all[:, h * IDX_DIM:(h + 1) * IDX_DIM] for h in range(IDX_HEADS)], axis=0)
    wi = kw_ref[0][:, IDX_DIM:IDX_DIM + IDX_HEADS]
    row_pos = j * Q_BLOCK + lax.broadcasted_iota(jnp.int32, (Q_BLOCK, 1), 0)

    def score_chunk(c, carry):
        off = pl.multiple_of(c * CK, CK)
        d = jnp.dot(qh, kiT_ref[0, :, pl.ds(off, CK)], preferred_element_type=F32)
        acc = jnp.maximum(d[0:Q_BLOCK], 0.0) * wi[:, 0:1]
        for h in range(1, IDX_HEADS):
            acc = acc + jnp.maximum(d[h * Q_BLOCK:(h + 1) * Q_BLOCK], 0.0) * wi[:, h:h + 1]
        key_pos = off + lax.broadcasted_iota(jnp.int32, (1, CK), 1)
        sc_ref[:, pl.ds(off, CK)] = jnp.where(key_pos <= row_pos, acc, -jnp.inf)
        return carry

    lax.fori_loop(0, nck, score_chunk, 0)
    thr, n_eq_take = _topk_threshold(sc_ref, nck, jnp.minimum(row_pos + 1, topk))

    tri = _tri_inclusive(CK)
    q_all = q_ref[0]
    acc_ref[...] = jnp.zeros_like(acc_ref)

    def attend_chunk(c, carry):
        eq_seen, ms, ls = carry
        off = pl.multiple_of(c * CK, CK)
        sel, eq_seen = _select_mask(sc_ref[:, pl.ds(off, CK)], thr, n_eq_take, eq_seen, tri)
        ms_new, ls_new = [], []
        for h in range(N_HEADS):
            hs = slice(h * HEAD_DIM, (h + 1) * HEAD_DIM)
            lg = jnp.dot(q_all[:, hs], kT_ref[0, hs, pl.ds(off, CK)], preferred_element_type=F32)
            p, alpha, m_new, l_new = _softmax_step(lg, sel, ms[h], ls[h])
            acc_ref[h] = alpha * acc_ref[h] + jnp.dot(p.astype(BF16), v_ref[0, pl.ds(off, CK), hs],
                                                      preferred_element_type=F32)
            ms_new.append(m_new)
            ls_new.append(l_new)
        return eq_seen, tuple(ms_new), tuple(ls_new)

    col = lambda v: jnp.full((Q_BLOCK, 1), v, F32)
    init = (col(0.0), tuple(col(NEG_BIG) for _ in range(N_HEADS)), tuple(col(0.0) for _ in range(N_HEADS)))
    _, _, ls = lax.fori_loop(0, nck, attend_chunk, init)
    o_ref[0] = jnp.concatenate([acc_ref[h] / ls[h] for h in range(N_HEADS)], axis=1).astype(BF16)


def _attn_prompt(q, qi, kw, kiT, kT, vb, *, topk):
    b, s, _ = q.shape
    blk = lambda n: pl.BlockSpec((1, Q_BLOCK, n), lambda i, j: (i, j, 0))
    return pl.pallas_call(
        functools.partial(_attn_prompt_kernel, topk=topk),
        grid=(b, s // Q_BLOCK),
        in_specs=[blk(E_A), blk(E_A), blk(LANE),
                  pl.BlockSpec((1, IDX_DIM, s), lambda i, j: (i, 0, 0)),
                  pl.BlockSpec((1, E_A, s), lambda i, j: (i, 0, 0)),
                  pl.BlockSpec((1, s, E_A), lambda i, j: (i, 0, 0))],
        out_specs=blk(E_A),
        out_shape=jax.ShapeDtypeStruct((b, s, E_A), BF16),
        scratch_shapes=[pltpu.VMEM((Q_BLOCK, s), F32), pltpu.VMEM((N_HEADS, Q_BLOCK, HEAD_DIM), F32)],
        compiler_params=pltpu.CompilerParams(dimension_semantics=("arbitrary", "arbitrary"),
                                             vmem_limit_bytes=VMEM_LIMIT),
        name="attn_prompt",
    )(q, qi, kw, kiT, kT, vb)


SROWS = 8
SCK = PAGES_PER_STEP * 128


def _sample_scores_kernel(pt_ref, qi_ref, w_ref, knew_ref, *refs, nch, t_new):
    pages = refs[:PAGES_PER_STEP]
    out_ref = refs[PAGES_PER_STEP]
    c = pl.program_id(1)
    qi = qi_ref[0]
    w = w_ref[0][:, 0:1]

    def scores(keys):
        d = lax.dot_general(qi, keys.astype(BF16), (((1,), (1,)), ((), ())), preferred_element_type=F32)
        t = jnp.maximum(d, 0.0) * w
        acc = t[0:SROWS]
        for h in range(1, IDX_HEADS):
            acc = acc + t[h * SROWS:(h + 1) * SROWS]
        return acc

    @pl.when(c < nch)
    def _():
        out_ref[0] = scores(jnp.concatenate([p[0, 0] for p in pages], axis=0))

    @pl.when(c == nch)
    def _():
        sc = scores(knew_ref[0])
        qrow = lax.broadcasted_iota(jnp.int32, sc.shape, 0)
        kcol = lax.broadcasted_iota(jnp.int32, sc.shape, 1)
        ok = (kcol <= jnp.minimum(qrow, t_new - 1))
        out_ref[0] = jnp.concatenate(
            [jnp.where(ok, sc, -jnp.inf), jnp.full((SROWS, SCK - sc.shape[1]), -jnp.inf, F32)], axis=1)


def _sample_scores(page_table, qi_h, w_h, knew, cache_idx_k, layer, *, t_new):
    db, n_pages = page_table.shape
    nch = n_pages // PAGES_PER_STEP
    page = cache_idx_k.shape[2]

    def page_spec(i):
        return pl.BlockSpec(
            (1, 1, page, IDX_DIM),
            lambda b, c, pt: (layer, pt[b, jnp.minimum(c, nch - 1) * PAGES_PER_STEP + i], 0, 0))

    grid_spec = pltpu.PrefetchScalarGridSpec(
        num_scalar_prefetch=1, grid=(db, nch + 1),
        in_specs=[pl.BlockSpec((1, IDX_HEADS * SROWS, IDX_DIM), lambda b, c, pt: (b, 0, 0)),
                  pl.BlockSpec((1, IDX_HEADS * SROWS, LANE), lambda b, c, pt: (b, 0, 0)),
                  pl.BlockSpec((1, page, IDX_DIM), lambda b, c, pt: (b, 0, 0))]
                 + [page_spec(i) for i in range(PAGES_PER_STEP)],
        out_specs=pl.BlockSpec((1, SROWS, SCK), lambda b, c, pt: (b, 0, c)))
    return pl.pallas_call(
        functools.partial(_sample_scores_kernel, nch=nch, t_new=t_new),
        grid_spec=grid_spec,
        out_shape=jax.ShapeDtypeStruct((db, SROWS, (nch + 1) * SCK), F32),
        compiler_params=pltpu.CompilerParams(dimension_semantics=("arbitrary", "arbitrary"),
                                             vmem_limit_bytes=VMEM_LIMIT),
        name="sample_scores",
    )(page_table, qi_h, w_h, knew, *([cache_idx_k] * PAGES_PER_STEP))


def _sample_threshold_kernel(sc_ref, thr_ref, take_ref, *, topk):
    rows, width = sc_ref.shape
    thr, n_eq_take = _topk_threshold(sc_ref, width // CK, jnp.full((rows, 1), topk, jnp.int32))
    thr_ref[...] = jnp.broadcast_to(thr, thr_ref.shape)
    take_ref[...] = jnp.broadcast_to(n_eq_take, take_ref.shape)


def _sample_threshold(scores2d, *, topk):
    rows, width = scores2d.shape
    rb = 128 if rows % 128 == 0 else rows
    return pl.pallas_call(
        functools.partial(_sample_threshold_kernel, topk=topk),
        grid=(rows // rb,),
        in_specs=[pl.BlockSpec((rb, width), lambda i: (i, 0))],
        out_specs=[pl.BlockSpec((rb, LANE), lambda i: (i, 0))] * 2,
        out_shape=[jax.ShapeDtypeStruct((rows, LANE), F32)] * 2,
        compiler_params=pltpu.CompilerParams(dimension_semantics=("arbitrary",),
                                             vmem_limit_bytes=VMEM_LIMIT),
        name="sample_threshold",
    )(scores2d)


def _sample_attn_kernel(pt_ref, q_ref, sc_ref, thr_ref, take_ref, knew_ref, vnew_ref, *refs, nch):
    kpages = refs[:PAGES_PER_STEP]
    vpages = refs[PAGES_PER_STEP:2 * PAGES_PER_STEP]
    o_ref, m_ref, l_ref, acc_ref, seen_ref = refs[2 * PAGES_PER_STEP:]
    c = pl.program_id(1)

    @pl.when(c == 0)
    def _():
        m_ref[...] = jnp.full_like(m_ref, NEG_BIG)
        l_ref[...] = jnp.zeros_like(l_ref)
        acc_ref[...] = jnp.zeros_like(acc_ref)
        seen_ref[...] = jnp.zeros_like(seen_ref)

    thr = thr_ref[0][:, 0:1]
    take = take_ref[0][:, 0:1]
    tri = _tri_inclusive(CK)

    def attend(width, k_of_head, v_of_head):
        seen = seen_ref[:, 0:1]
        sels = []
        for i in range(width // CK) if width >= CK else range(1):
            w_i = min(CK, width)
            sel, seen = _select_mask(sc_ref[0][:, i * CK:i * CK + w_i], thr, take, seen, tri[:w_i, :w_i])
            sels.append(sel)
        sel = sels[0] if len(sels) == 1 else jnp.concatenate(sels, axis=1)
        seen_ref[...] = jnp.broadcast_to(seen, seen_ref.shape)
        for h in range(N_HEADS):
            qh = q_ref[0][h * SROWS:(h + 1) * SROWS, :]
            lg = lax.dot_general(qh, k_of_head(h).astype(BF16), (((1,), (1,)), ((), ())),
                                 preferred_element_type=F32)
            p, alpha, m_new, l_new = _softmax_step(lg, sel, m_ref[h][:, 0:1], l_ref[h][:, 0:1])
            acc_ref[h] = alpha * acc_ref[h] + jnp.dot(p.astype(BF16), v_of_head(h).astype(BF16),
                                                      preferred_element_type=F32)
            m_ref[h] = jnp.broadcast_to(m_new, m_ref.shape[1:])
            l_ref[h] = jnp.broadcast_to(l_new, l_ref.shape[1:])

    @pl.when(c < nch)
    def _():
        attend(SCK,
               lambda h: jnp.concatenate([p[0, 0, :, h, :] for p in kpages], axis=0),
               lambda h: jnp.concatenate([p[0, 0, :, h, :] for p in vpages], axis=0))

    @pl.when(c == nch)
    def _():
        attend(knew_ref.shape[1], lambda h: knew_ref[0, :, h, :], lambda h: vnew_ref[0, :, h, :])
        for h in range(N_HEADS):
            o_ref[0, h] = acc_ref[h] / l_ref[h][:, 0:1]


def _sample_attn(page_table, q_h, scores, thr, take, knew, vnew, cache_k, cache_v, layer):
    db, n_pages = page_table.shape
    nch = n_pages // PAGES_PER_STEP
    page = cache_k.shape[2]

    def page_spec(i):
        return pl.BlockSpec(
            (1, 1, page, N_HEADS, HEAD_DIM),
            lambda b, c, pt: (layer, pt[b, jnp.minimum(c, nch - 1) * PAGES_PER_STEP + i], 0, 0, 0))

    per_b = lambda *shape: pl.BlockSpec((1,) + shape, lambda b, c, pt: (b,) + (0,) * len(shape))
    grid_spec = pltpu.PrefetchScalarGridSpec(
        num_scalar_prefetch=1, grid=(db, nch + 1),
        in_specs=[per_b(N_HEADS * SROWS, HEAD_DIM),
                  pl.BlockSpec((1, SROWS, SCK), lambda b, c, pt: (b, 0, c)),
                  per_b(SROWS, LANE), per_b(SROWS, LANE),
                  per_b(page, N_HEADS, HEAD_DIM), per_b(page, N_HEADS, HEAD_DIM)]
                 + [page_spec(i) for i in range(PAGES_PER_STEP)] * 2,
        out_specs=per_b(N_HEADS, SROWS, HEAD_DIM),
        scratch_shapes=[pltpu.VMEM((N_HEADS, SROWS, LANE), F32), pltpu.VMEM((N_HEADS, SROWS, LANE), F32),
                        pltpu.VMEM((N_HEADS, SROWS, HEAD_DIM), F32), pltpu.VMEM((SROWS, LANE), F32)])
    return pl.pallas_call(
        functools.partial(_sample_attn_kernel, nch=nch),
        grid_spec=grid_spec,
        out_shape=jax.ShapeDtypeStruct((db, N_HEADS, SROWS, HEAD_DIM), F32),
        compiler_params=pltpu.CompilerParams(dimension_semantics=("arbitrary", "arbitrary"),
                                             vmem_limit_bytes=VMEM_LIMIT),
        name="sample_attn",
    )(page_table, q_h, scores, thr, take, knew, vnew,
      *([cache_k] * PAGES_PER_STEP), *([cache_v] * PAGES_PER_STEP))


def _post_kernel(x_ref, oa_ref, g0_ref, mbc_ref, wba_ref, wo_ref, wup_ref, wdn_ref,
                 gpost_ref, gfpre_ref, gfpost_ref, y_ref):
    m = g0_ref[...].astype(F32) * jnp.dot(oa_ref[...], wba_ref[...], preferred_element_type=F32) \
        + mbc_ref[...]
    y = jnp.dot(m.astype(BF16), wo_ref[...], preferred_element_type=F32)
    x1 = x_ref[...] + _rms(y, gpost_ref[...])
    a = jnp.maximum(jnp.dot(_rms(x1, gfpre_ref[...]).astype(BF16), wup_ref[...],
                            preferred_element_type=F32), 0.0)
    f = jnp.dot((a * a).astype(BF16), wdn_ref[...], preferred_element_type=F32)
    y_ref[...] = x1 + _rms(f, gfpost_ref[...])


def _post(x, oa, g0, mbc, wba, wo, wup, wdn, gpost, gfpre, gfpost):
    m = x.shape[0]
    tm = min(256, m)
    row = lambda n: pl.BlockSpec((tm, n), lambda i: (i, 0))
    return pl.pallas_call(
        _post_kernel, grid=(m // tm,),
        in_specs=[row(D_MODEL), row(E_A), row(D_MODEL), row(D_MODEL),
                  _const_spec((E_A, D_MODEL)), _const_spec((D_MODEL, D_MODEL)),
                  _const_spec((D_MODEL, D_FF)), _const_spec((D_FF, D_MODEL)),
                  _const_spec((1, D_MODEL)), _const_spec((1, D_MODEL)), _const_spec((1, D_MODEL))],
        out_specs=row(D_MODEL),
        out_shape=jax.ShapeDtypeStruct((m, D_MODEL), F32),
        compiler_params=pltpu.CompilerParams(dimension_semantics=("arbitrary",),
                                             vmem_limit_bytes=VMEM_LIMIT),
        name="post",
    )(x, oa, g0, mbc, wba, wo, wup, wdn, gpost, gfpre, gfpost)


def _rope_tables(pos):
    half = HEAD_DIM // 2
    inv = ROPE_THETA ** (-jnp.arange(half, dtype=F32) * (2.0 / HEAD_DIM))
    ang = pos[:, None] * inv[None, :]
    c, s = jnp.cos(ang), jnp.sin(ang)
    return jnp.tile(c, (1, LANE // half)), jnp.tile(jnp.concatenate([-s, s], axis=1), (1, LANE // HEAD_DIM))


def _heads_first(a, t):
    db = a.shape[0] // t
    a = a.reshape(db, t, N_HEADS, -1).transpose(0, 2, 1, 3)
    a = jnp.pad(a, ((0, 0), (0, 0), (0, SROWS - t), (0, 0)))
    return a.reshape(db, N_HEADS * SROWS, -1)


def kernel(x_prompt, x_sample, cache_k, cache_v, cache_idx_k, state_conv, page_table, norm_mix_pre, norm_mix_post, norm_ffn_pre, norm_ffn_post, w_in, gmlp_ln_g, gmlp_ln_b, gmlp_ws, gmlp_bs, conv_w, conv_b, w_br_attn, w_br_gmlp, w_br_conv, w_out, w_ff_up, w_ff_down):
    depth = w_in.shape[0]
    b, s, _ = x_prompt.shape
    db, t, _ = x_sample.shape
    page = cache_k.shape[2]
    past = page_table.shape[1] * page
    assert s % 256 == 0 and (db * t) % 8 == 0 and t <= SROWS and page == 128
    assert page_table.shape[1] % PAGES_PER_STEP == 0

    w_in_p = jnp.concatenate(
        [w_in[:, :, :OFF_B], jnp.zeros((depth, D_MODEL, C_B - OFF_B), w_in.dtype), w_in[:, :, OFF_B:]],
        axis=2).astype(BF16)
    causal = jnp.tril(jnp.ones((CHUNK, CHUNK), bool))
    wm = jnp.where(causal[None, None], gmlp_ws, 0)
    wmix_p = wm.astype(BF16)
    bmix_p = jnp.repeat(jnp.swapaxes(gmlp_bs, 1, 2), E_B // GMLP_GROUPS, axis=2)
    ms = db * t
    reps = ms // t
    eye = jnp.eye(reps, dtype=wm.dtype)
    wmix_s = jnp.einsum('ab,lgts->lgatbs', eye, wm[:, :, :t, :t]).reshape(depth, GMLP_GROUPS, ms, ms)
    wmix_s = wmix_s.astype(BF16)
    bmix_s = jnp.tile(bmix_p[:, :t], (1, reps, 1))
    wbb, wbc, wba = (w.astype(BF16) for w in (w_br_gmlp, w_br_conv, w_br_attn))
    wo, wup, wdn = (w.astype(BF16) for w in (w_out, w_ff_up, w_ff_down))
    r2 = lambda a, l: a[l][None, :]

    cos_p, sin_p = _rope_tables(jnp.arange(s, dtype=F32))
    pos_s = jnp.tile(jnp.arange(t, dtype=F32) + past, reps)
    cos_s, sin_s = _rope_tables(pos_s)
    topk_p = min(TOPK_MAX, s // 4)
    topk_s = min(TOPK_MAX, (past + t) // 4)

    xp = x_prompt.reshape(b * s, D_MODEL)
    xs = x_sample.reshape(ms, D_MODEL)
    outs = [[] for _ in range(9)]
    for l in range(depth):
        common = (r2(norm_mix_pre, l), w_in_p[l])
        tail = (r2(gmlp_ln_g, l), r2(gmlp_ln_b, l))
        conv = (conv_w[l], r2(conv_b, l), wbb[l], wbc[l])
        (q, k, kT, v, vb, qi, kw, kiT, g0, mbc, ctail) = _proj(
            xp, *common, cos_p, sin_p, *tail, wmix_p[l], bmix_p[l], *conv, sample=False, seq_len=s)
        oa = _attn_prompt(q.reshape(b, s, E_A), qi.reshape(b, s, E_A), kw.reshape(b, s, LANE),
                          kiT, kT, vb.reshape(b, s, E_A), topk=topk_p)
        post_w = (wba[l], wo[l], wup[l], wdn[l], r2(norm_mix_post, l), r2(norm_ffn_pre, l),
                  r2(norm_ffn_post, l))
        xp = _post(xp, oa.reshape(b * s, E_A), g0, mbc, *post_w)
        outs[0].append(k.reshape(b, s, N_HEADS, HEAD_DIM))
        outs[1].append(v.reshape(b, s, N_HEADS, HEAD_DIM))
        outs[2].append(kw[:, :IDX_DIM].reshape(b, s, IDX_DIM))
        outs[3].append(ctail[:, 8 - (CONV_W - 1):, :])
        st = state_conv[l]
        z = jnp.zeros((db, 1, E_C), st.dtype)
        st1 = jnp.concatenate([st[:, 1:2]] + [z] * (t - 1), axis=1)
        st2 = jnp.concatenate([st[:, 0:1], st[:, 1:2]] + [z] * (t - 2), axis=1)
        st12 = jnp.concatenate([st1, st2], axis=2).reshape(ms, 2 * E_C)
        (q_s, k_s, v_s, qi_s, kw_s, g0_s, mbc_s, cin_s, vn_s) = _proj(
            xs, *common, cos_s, sin_s, *tail, wmix_s[l], bmix_s[l], *conv, sample=True, seq_len=t,
            st=st12)
        qi_h = _heads_first(qi_s, t)
        w_h = _heads_first(kw_s[:, IDX_DIM:IDX_DIM + IDX_HEADS][:, :, None], t)
        w_h = jnp.broadcast_to(w_h, w_h.shape[:2] + (LANE,))
        ki_new = jnp.pad(kw_s[:, :IDX_DIM].reshape(db, t, IDX_DIM), ((0, 0), (0, page - t), (0, 0)))
        scores = _sample_scores(page_table, qi_h, w_h, ki_new, cache_idx_k, l, t_new=t)
        thr, take = _sample_threshold(scores.reshape(db * SROWS, -1), topk=topk_s)
        k_new = jnp.pad(k_s.reshape(db, t, N_HEADS, HEAD_DIM), ((0, 0), (0, page - t), (0, 0), (0, 0)))
        v_new = jnp.pad(v_s.reshape(db, t, N_HEADS, HEAD_DIM), ((0, 0), (0, page - t), (0, 0), (0, 0)))
        oa_s = _sample_attn(page_table, _heads_first(q_s, t), scores, thr.reshape(db, SROWS, LANE),
                            take.reshape(db, SROWS, LANE), k_new, v_new, cache_k, cache_v, l)
        oa_s = oa_s[:, :, :t].transpose(0, 2, 1, 3).reshape(ms, E_A).astype(BF16)
        xs = _post(xs, oa_s, g0_s, mbc_s, *post_w)
        outs[4].append(k_s.reshape(db, t, N_HEADS, HEAD_DIM))
        outs[5].append(v_s.reshape(db, t, N_HEADS, HEAD_DIM))
        outs[6].append(kw_s[:, :IDX_DIM].reshape(db, t, IDX_DIM))
        outs[7].append(cin_s.reshape(db, t, E_C)[:, t - (CONV_W - 1):])
        outs[8].append(vn_s.reshape(db, t, E_B))
    return (xp.reshape(b, s, D_MODEL), xs.reshape(db, t, D_MODEL)) + tuple(jnp.stack(o) for o in outs)
```

```python
import functools
import math

import jax
import jax.numpy as jnp
from jax import lax
from jax.experimental import pallas as pl
from jax.experimental.pallas import tpu as pltpu

F32 = jnp.float32
BF16 = jnp.bfloat16

D_MODEL = 1024
N_HEADS = 8
HEAD_DIM = 64
E_A = N_HEADS * HEAD_DIM
IDX_HEADS = 8
IDX_DIM = 64
IDX_W_SCALE = 1.0 / math.sqrt(IDX_HEADS * IDX_DIM)
TOPK_MAX = 256
Q_BLOCK = 128
ROPE_THETA = 10000.0
CHUNK = 128
GMLP_GROUPS = 4
E_B = 512
E_C = 512
CONV_W = 3
D_FF = 4 * D_MODEL
EPS = 1e-6

OFF_KI = 3 * E_A + IDX_HEADS * IDX_DIM
OFF_B = OFF_KI + IDX_DIM + IDX_HEADS
LANE = 128
SUBLANE = 8
C_KW = OFF_KI
C_B = C_KW + LANE
C_C = C_B + 2 * E_B
C_G = C_C + 3 * E_C
N_PAD = C_G + 3 * D_MODEL

VMEM_LIMIT = 56 * 1024 * 1024
ROW_TILE = 256
NEG_BIG = -1e30
KEY_NEG_INF = -2139095041
KEY_POS_INF = 2139095040
CK = 256
PAGES_PER_STEP = 16
SROWS = SUBLANE


def _rms(x, g):
    return x * lax.rsqrt(jnp.mean(x * x, axis=-1, keepdims=True) + EPS) * g


def _rope128(z, cos, sin):
    lane = lax.broadcasted_iota(jnp.int32, z.shape, 1)
    partner = jnp.where((lane % HEAD_DIM) < HEAD_DIM // 2,
                        pltpu.roll(z, LANE - HEAD_DIM // 2, 1),
                        pltpu.roll(z, HEAD_DIM // 2, 1))
    return z * cos + partner * sin


def _rope(z, cos, sin):
    return jnp.concatenate(
        [_rope128(z[:, i:i + LANE], cos, sin) for i in range(0, z.shape[1], LANE)], axis=1)


def _proj_kernel(*refs, sample, tm, seq_rows):
    if sample:
        (x_ref, gpre_ref, w_ref, cos_ref, sin_ref, lng_ref, lnb_ref, wmix_ref, bmix_ref,
         cw_ref, cbias_ref, wbb_ref, wbc_ref, st_ref,
         q_ref, k_ref, v_ref, qi_ref, kw_ref, g0_ref, mbc_ref, cin_ref, vn_ref) = refs
    else:
        (x_ref, gpre_ref, w_ref, cos_ref, sin_ref, lng_ref, lnb_ref, wmix_ref, bmix_ref,
         cw_ref, cbias_ref, wbb_ref, wbc_ref,
         qT_ref, qiT_ref, wT_ref, kb_ref, kib_ref, vTb_ref, kT_ref, vT_ref, kiT_ref,
         g0_ref, mbc_ref, ctail_ref, carry_ref) = refs

    h = _rms(x_ref[...], gpre_ref[...]).astype(BF16)
    cos = cos_ref[...]
    sin = sin_ref[...]

    z = jnp.dot(h, w_ref[:, 0:C_KW], preferred_element_type=F32)
    q = _rope(z[:, 0:E_A], cos, sin) * (HEAD_DIM ** -0.5)
    k = _rope(z[:, E_A:2 * E_A], cos, sin)
    v = z[:, 2 * E_A:3 * E_A]
    qi = _rope(z[:, 3 * E_A:4 * E_A], cos, sin)

    zk = jnp.dot(h, w_ref[:, C_KW:C_B], preferred_element_type=F32)
    lane = lax.broadcasted_iota(jnp.int32, zk.shape, 1)
    kw = jnp.where(lane < IDX_DIM, _rope128(zk, cos, sin), zk * IDX_W_SCALE)
    if sample:
        q_ref[...] = q.astype(BF16)
        k_ref[...] = k
        v_ref[...] = v
        qi_ref[...] = qi.astype(BF16)
        kw_ref[...] = kw
    else:
        qT_ref[0] = q.T.astype(BF16)
        qiT_ref[0] = qi.T.astype(BF16)
        kT = k.T
        kT_ref[0] = kT
        kb_ref[...] = k.astype(BF16)
        vT = v.T
        vT_ref[0] = vT
        vTb_ref[0] = vT.astype(BF16)
        kwT = kw.T
        kiT_ref[0] = kwT[0:IDX_DIM, :]
        wT_ref[0] = kwT[IDX_DIM:IDX_DIM + IDX_HEADS, :]
        kib_ref[...] = kw[:, 0:IDX_DIM].astype(BF16)

    gl = jax.nn.gelu(jnp.dot(h, w_ref[:, C_B:C_C], preferred_element_type=F32))
    u = gl[:, 0:E_B]
    vg = gl[:, E_B:2 * E_B]
    mu = jnp.mean(vg, axis=-1, keepdims=True)
    vc = vg - mu
    vn = vc * lax.rsqrt(jnp.mean(vc * vc, axis=-1, keepdims=True) + EPS) * lng_ref[...] + lnb_ref[...]
    if sample:
        vn_ref[...] = vn
    vnb = vn.astype(BF16)
    gw = E_B // GMLP_GROUPS
    rows = []
    for c in range(tm // CHUNK):
        cols = []
        for g in range(GMLP_GROUPS):
            cols.append(jnp.dot(wmix_ref[g], vnb[c * CHUNK:(c + 1) * CHUNK, g * gw:(g + 1) * gw],
                                preferred_element_type=F32))
        rows.append(jnp.concatenate(cols, axis=1) + bmix_ref[...])
    mix = rows[0] if len(rows) == 1 else jnp.concatenate(rows, axis=0)
    ob = (u * mix).astype(BF16)

    zc = jnp.dot(h, w_ref[:, C_C:C_G], preferred_element_type=F32)
    cb = zc[:, 0:E_C]
    cin = zc[:, E_C:2 * E_C] * zc[:, 2 * E_C:3 * E_C]
    row = lax.broadcasted_iota(jnp.int32, cin.shape, 0)
    if sample:
        cin_ref[...] = cin
        i_in_seq = row % seq_rows
        s1 = jnp.where(i_in_seq >= 1, pltpu.roll(cin, 1, 0), st_ref[:, 0:E_C])
        s2 = jnp.where(i_in_seq >= 2, pltpu.roll(cin, 2, 0), st_ref[:, E_C:2 * E_C])
    else:
        @pl.when(pl.program_id(0) % (seq_rows // tm) == 0)
        def _():
            carry_ref[...] = jnp.zeros_like(carry_ref)
        p2 = carry_ref[0:1, :]
        p1 = carry_ref[1:2, :]
        s1 = jnp.where(row >= 1, pltpu.roll(cin, 1, 0), p1)
        s2 = jnp.where(row >= 2, pltpu.roll(cin, 2, 0), jnp.where(row == 0, p2, p1))
        carry_ref[0:2, :] = cin[tm - 2:tm, :]
        ctail_ref[0] = cin[tm - SUBLANE:tm, :]
    y = s2 * cw_ref[0:1, :] + s1 * cw_ref[1:2, :] + cin * cw_ref[2:3, :] + cbias_ref[...]
    oc = (cb * y).astype(BF16)

    g0_ref[...] = jax.nn.sigmoid(
        jnp.dot(h, w_ref[:, C_G:C_G + D_MODEL], preferred_element_type=F32)).astype(BF16)
    g1 = jax.nn.sigmoid(jnp.dot(h, w_ref[:, C_G + D_MODEL:C_G + 2 * D_MODEL], preferred_element_type=F32))
    mbc = g1 * jnp.dot(ob, wbb_ref[...], preferred_element_type=F32)
    g2 = jax.nn.sigmoid(jnp.dot(h, w_ref[:, C_G + 2 * D_MODEL:N_PAD], preferred_element_type=F32))
    mbc_ref[...] = mbc + g2 * jnp.dot(oc, wbc_ref[...], preferred_element_type=F32)


def _const_spec(shape):
    nd = len(shape)
    return pl.BlockSpec(shape, lambda *_: (0,) * nd, pipeline_mode=pl.Buffered(1))


def _proj(x, gpre, w, cos, sin, lng, lnb, wmix, bmix, cw, cbias, wbb, wbc, *, sample, seq_len,
          st=None):
    m = x.shape[0]
    tm = m if sample else ROW_TILE
    tps = max(seq_len // tm, 1)
    n_seq = m // seq_len
    row = lambda n: pl.BlockSpec((tm, n), lambda i: (i, 0))
    pos_spec = pl.BlockSpec((tm, LANE), (lambda i: (0, 0)) if sample else (lambda i: (i % tps, 0)))
    in_specs = [row(D_MODEL), _const_spec((1, D_MODEL)), _const_spec((D_MODEL, N_PAD)),
                pos_spec, pos_spec, _const_spec((1, E_B)), _const_spec((1, E_B)),
                _const_spec((GMLP_GROUPS, CHUNK, CHUNK)), _const_spec((CHUNK, E_B)),
                _const_spec((CONV_W, E_C)), _const_spec((1, E_C)),
                _const_spec((E_B, D_MODEL)), _const_spec((E_C, D_MODEL))]
    args = [x, gpre, w, cos, sin, lng, lnb, wmix, bmix, cw, cbias, wbb, wbc]
    sds = jax.ShapeDtypeStruct
    if sample:
        in_specs.append(row(2 * E_C))
        args.append(st)
        out_shape = [sds((m, E_A), BF16), sds((m, E_A), F32), sds((m, E_A), F32), sds((m, E_A), BF16),
                     sds((m, LANE), F32), sds((m, D_MODEL), BF16), sds((m, D_MODEL), F32),
                     sds((m, E_C), F32), sds((m, E_B), F32)]
        out_specs = [row(E_A), row(E_A), row(E_A), row(E_A), row(LANE), row(D_MODEL), row(D_MODEL),
                     row(E_C), row(E_B)]
        scratch = []
    else:
        colT = lambda n: pl.BlockSpec((1, n, tm), lambda i: (i // tps, 0, i % tps))
        out_shape = [sds((n_seq, E_A, seq_len), BF16), sds((n_seq, E_A, seq_len), BF16),
                     sds((n_seq, IDX_HEADS, seq_len), F32), sds((m, E_A), BF16), sds((m, IDX_DIM), BF16),
                     sds((n_seq, E_A, seq_len), BF16), sds((n_seq, E_A, seq_len), F32),
                     sds((n_seq, E_A, seq_len), F32), sds((n_seq, IDX_DIM, seq_len), F32),
                     sds((m, D_MODEL), BF16), sds((m, D_MODEL), F32), sds((n_seq, SUBLANE, E_C), F32)]
        out_specs = [colT(E_A), colT(E_A), colT(IDX_HEADS), row(E_A), row(IDX_DIM),
                     colT(E_A), colT(E_A), colT(E_A), colT(IDX_DIM),
                     row(D_MODEL), row(D_MODEL),
                     pl.BlockSpec((1, SUBLANE, E_C), lambda i: (i // tps, 0, 0))]
        scratch = [pltpu.VMEM((SUBLANE, E_C), F32)]
    return pl.pallas_call(
        functools.partial(_proj_kernel, sample=sample, tm=tm, seq_rows=seq_len),
        grid=(m // tm,), in_specs=in_specs, out_specs=out_specs, out_shape=out_shape,
        scratch_shapes=scratch,
        compiler_params=pltpu.CompilerParams(dimension_semantics=("arbitrary",),
                                             vmem_limit_bytes=VMEM_LIMIT),
        name="proj_sample" if sample else "proj_prompt",
    )(*args)


def _key_to_f32(key):
    bits = key ^ ((key >> 31) & jnp.int32(0x7FFFFFFF))
    return lax.bitcast_convert_type(bits, F32)


def _topk_threshold(sc_ref, nck, k_row):
    lanes = sc_ref.shape[1]
    k_f = k_row.astype(F32)
    acc_rows = CK // 4

    def count(pred_fn):
        def body(c, acc):
            off = pl.multiple_of(c * CK, CK)
            hit = jnp.where(pred_fn(sc_ref[pl.ds(off, CK), :]), 1.0, 0.0)
            return acc + jnp.sum(hit.reshape(CK // acc_rows, acc_rows, lanes), axis=0)
        acc = lax.fori_loop(0, nck, body, jnp.zeros((acc_rows, lanes), F32))
        return jnp.sum(acc, axis=0, keepdims=True)

    def bisect(_, lohi):
        lo, hi = lohi
        mid = (lo >> 1) + (hi >> 1) + (lo & hi & 1)
        midf = _key_to_f32(mid)
        ge = count(lambda s: s >= midf) >= k_f
        return jnp.where(ge, mid, lo), jnp.where(ge, hi, mid)

    lo0 = jnp.full((1, lanes), KEY_NEG_INF, jnp.int32)
    hi0 = jnp.full((1, lanes), KEY_POS_INF, jnp.int32)
    lo, _ = lax.fori_loop(0, 32, bisect, (lo0, hi0))
    thr = _key_to_f32(lo)
    n_eq_take = k_f - count(lambda s: s > thr)
    return thr, n_eq_take


def _select_cols(s, thr, n_eq_take, eq_seen, tril):
    eq = s == thr
    prefix = jnp.dot(tril, jnp.where(eq, 1.0, 0.0).astype(BF16), preferred_element_type=F32) + eq_seen
    sel = (s > thr) | (eq & (prefix <= n_eq_take))
    return sel, prefix[s.shape[0] - 1:s.shape[0], :]


def _select_rows(s, thr, n_eq_take, eq_seen, triu):
    eq = s == thr
    prefix = jnp.dot(jnp.where(eq, 1.0, 0.0).astype(BF16), triu, preferred_element_type=F32) + eq_seen
    sel = (s > thr) | (eq & (prefix <= n_eq_take))
    return sel, prefix[:, s.shape[1] - 1:s.shape[1]]


def _tri(n, lower):
    r = lax.broadcasted_iota(jnp.int32, (n, n), 0)
    c = lax.broadcasted_iota(jnp.int32, (n, n), 1)
    return jnp.where((c <= r) if lower else (r <= c), 1.0, 0.0).astype(BF16)


def _attn_prompt_kernel(qT_ref, qiT_ref, wT_ref, ki_ref, k_ref, vT_ref, o_ref, sc_ref, acc_ref,
                        lg_ref, p_ref, wq_ref, tri_ref, *, topk):
    j = pl.program_id(1)
    nck = (j * Q_BLOCK) // CK + 1
    qiT = qiT_ref[0]
    w_idx = jnp.concatenate([qiT[h * IDX_DIM:(h + 1) * IDX_DIM, :] for h in range(IDX_HEADS)], axis=1)
    wT = wT_ref[0]
    q_pos = j * Q_BLOCK + lax.broadcasted_iota(jnp.int32, (1, Q_BLOCK), 1)

    def score_chunk(c, carry):
        off = pl.multiple_of(c * CK, CK)
        d = jnp.dot(ki_ref[0, pl.ds(off, CK), :], w_idx, preferred_element_type=F32)
        acc = jnp.maximum(d[:, 0:Q_BLOCK], 0.0) * wT[0:1, :]
        for h in range(1, IDX_HEADS):
            acc = acc + jnp.maximum(d[:, h * Q_BLOCK:(h + 1) * Q_BLOCK], 0.0) * wT[h:h + 1, :]
        key_pos = off + lax.broadcasted_iota(jnp.int32, (CK, 1), 0)
        sc_ref[pl.ds(off, CK), :] = jnp.where(key_pos <= q_pos, acc, -jnp.inf)
        return carry

    lax.fori_loop(0, nck, score_chunk, 0)
    thr, n_eq_take = _topk_threshold(sc_ref, nck, jnp.minimum(q_pos + 1, topk))

    qT = qT_ref[0].astype(F32)
    upper = lax.broadcasted_iota(jnp.int32, (LANE, Q_BLOCK), 0) < HEAD_DIM
    for h in range(N_HEADS):
        slab = qT[(h // 2) * LANE:(h // 2 + 1) * LANE, :]
        wq_ref[h] = jnp.where(upper if h % 2 == 0 else ~upper, slab, 0.0).astype(BF16)
    tri_ref[...] = _tri(CK, lower=True)
    acc_ref[...] = jnp.zeros_like(acc_ref)

    def attend_chunk(c, carry):
        eq_seen, m_all, l_all = carry
        off = pl.multiple_of(c * CK, CK)
        sel, eq_seen = _select_cols(sc_ref[pl.ds(off, CK), :], thr, n_eq_take, eq_seen, tri_ref[...])
        sc_ref[pl.ds(off, CK), :] = jnp.where(sel, 0.0, -jnp.inf)
        for h in range(N_HEADS):
            kh = k_ref[0, pl.ds(off, CK), (h // 2) * LANE:(h // 2 + 1) * LANE]
            lg_ref[h] = jnp.dot(kh, wq_ref[h], preferred_element_type=F32) + sc_ref[pl.ds(off, CK), :]
        m_new = jnp.maximum(m_all, jnp.concatenate(
            [jnp.max(lg_ref[h], axis=0, keepdims=True) for h in range(N_HEADS)], axis=0))
        alpha = jnp.exp(m_all - m_new)
        l_rows = []
        for h in range(N_HEADS):
            p = jnp.exp(lg_ref[h] - m_new[h:h + 1, :])
            l_rows.append(jnp.sum(p, axis=0, keepdims=True))
            p_ref[h] = p.astype(BF16)
        for h in range(N_HEADS):
            hs = slice(h * HEAD_DIM, (h + 1) * HEAD_DIM)
            acc_ref[hs, :] = alpha[h:h + 1, :] * acc_ref[hs, :] + jnp.dot(
                vT_ref[0, hs, pl.ds(off, CK)], p_ref[h], preferred_element_type=F32)
        return eq_seen, m_new, alpha * l_all + jnp.concatenate(l_rows, axis=0)

    init = (jnp.zeros((1, Q_BLOCK), F32), jnp.full((N_HEADS, Q_BLOCK), NEG_BIG, F32),
            jnp.zeros((N_HEADS, Q_BLOCK), F32))
    _, _, l_all = lax.fori_loop(0, nck, attend_chunk, init)
    l_full = jnp.concatenate(
        [jnp.broadcast_to(l_all[h:h + 1, :], (HEAD_DIM, Q_BLOCK)) for h in range(N_HEADS)], axis=0)
    o_ref[0] = (acc_ref[...] / l_full).T.astype(BF16)


def _attn_prompt(qT, qiT, wT, kib, kb, vTb, *, topk):
    b, _, s = qT.shape
    colT = lambda n: pl.BlockSpec((1, n, Q_BLOCK), lambda i, j: (i, 0, j))
    return pl.pallas_call(
        functools.partial(_attn_prompt_kernel, topk=topk),
        grid=(b, s // Q_BLOCK),
        in_specs=[colT(E_A), colT(E_A), colT(IDX_HEADS),
                  pl.BlockSpec((1, s, IDX_DIM), lambda i, j: (i, 0, 0)),
                  pl.BlockSpec((1, s, E_A), lambda i, j: (i, 0, 0)),
                  pl.BlockSpec((1, E_A, s), lambda i, j: (i, 0, 0))],
        out_specs=pl.BlockSpec((1, Q_BLOCK, E_A), lambda i, j: (i, j, 0)),
        out_shape=jax.ShapeDtypeStruct((b, s, E_A), BF16),
        scratch_shapes=[pltpu.VMEM((s, Q_BLOCK), F32), pltpu.VMEM((E_A, Q_BLOCK), F32),
                        pltpu.VMEM((N_HEADS, CK, Q_BLOCK), F32), pltpu.VMEM((N_HEADS, CK, Q_BLOCK), BF16),
                        pltpu.VMEM((N_HEADS, LANE, Q_BLOCK), BF16), pltpu.VMEM((CK, CK), BF16)],
        compiler_params=pltpu.CompilerParams(dimension_semantics=("arbitrary", "arbitrary"),
                                             vmem_limit_bytes=VMEM_LIMIT),
        name="attn_prompt",
    )(qT, qiT, wT, kib, kb, vTb)


def _sample_scores_kernel(pt_ref, qi_ref, w_ref, kinew_ref, *refs, nch, t_new):
    pages = refs[:PAGES_PER_STEP]
    out_ref = refs[PAGES_PER_STEP]
    c = pl.program_id(1)
    qi = qi_ref[0]
    w = w_ref[0][:, 0:1]

    def scores(keysT):
        t = jnp.maximum(jnp.dot(qi, keysT.astype(BF16), preferred_element_type=F32), 0.0) * w
        acc = t[0:SROWS]
        for h in range(1, IDX_HEADS):
            acc = acc + t[h * SROWS:(h + 1) * SROWS]
        return acc

    @pl.when(c < nch)
    def _():
        out_ref[0] = scores(jnp.concatenate([p[0, 0] for p in pages], axis=1))

    @pl.when(c == nch)
    def _():
        sc = scores(kinew_ref[0])
        qrow = lax.broadcasted_iota(jnp.int32, sc.shape, 0)
        kcol = lax.broadcasted_iota(jnp.int32, sc.shape, 1)
        ok = kcol <= jnp.minimum(qrow, t_new - 1)
        out_ref[0] = jnp.concatenate(
            [jnp.where(ok, sc, -jnp.inf),
             jnp.full((SROWS, out_ref.shape[2] - sc.shape[1]), -jnp.inf, F32)], axis=1)


def _sample_scores(page_table, qi_h, w_h, kiT_new, cache_iT, layer, *, t_new):
    db, n_pages = page_table.shape
    nch = n_pages // PAGES_PER_STEP
    page = cache_iT.shape[3]
    sck = PAGES_PER_STEP * page

    def page_spec(i):
        return pl.BlockSpec(
            (1, 1, IDX_DIM, page),
            lambda b, c, pt: (layer, pt[b, jnp.minimum(c, nch - 1) * PAGES_PER_STEP + i], 0, 0))

    per_b = lambda *shape: pl.BlockSpec((1,) + shape, lambda b, c, pt: (b,) + (0,) * len(shape))
    grid_spec = pltpu.PrefetchScalarGridSpec(
        num_scalar_prefetch=1, grid=(db, nch + 1),
        in_specs=[per_b(IDX_HEADS * SROWS, IDX_DIM), per_b(IDX_HEADS * SROWS, LANE), per_b(IDX_DIM, page)]
                 + [page_spec(i) for i in range(PAGES_PER_STEP)],
        out_specs=pl.BlockSpec((1, SROWS, sck), lambda b, c, pt: (b, 0, c)))
    return pl.pallas_call(
        functools.partial(_sample_scores_kernel, nch=nch, t_new=t_new),
        grid_spec=grid_spec,
        out_shape=jax.ShapeDtypeStruct((db, SROWS, (nch + 1) * sck), F32),
        compiler_params=pltpu.CompilerParams(dimension_semantics=("arbitrary", "arbitrary"),
                                             vmem_limit_bytes=VMEM_LIMIT),
        name="sample_scores",
    )(page_table, qi_h, w_h, kiT_new, *([cache_iT] * PAGES_PER_STEP))


def _sample_threshold_kernel(sc_ref, thr_ref, take_ref, *, topk):
    keys, lanes = sc_ref.shape
    thr, n_eq_take = _topk_threshold(sc_ref, keys // CK, jnp.full((1, lanes), topk, jnp.int32))
    thr_ref[...] = jnp.broadcast_to(thr, thr_ref.shape)
    take_ref[...] = jnp.broadcast_to(n_eq_take, take_ref.shape)


def _sample_threshold(scoresT, *, topk):
    keys, nq = scoresT.shape
    return pl.pallas_call(
        functools.partial(_sample_threshold_kernel, topk=topk),
        grid=(nq // LANE,),
        in_specs=[pl.BlockSpec((keys, LANE), lambda i: (0, i))],
        out_specs=[pl.BlockSpec((SUBLANE, LANE), lambda i: (0, i))] * 2,
        out_shape=[jax.ShapeDtypeStruct((SUBLANE, nq), F32)] * 2,
        compiler_params=pltpu.CompilerParams(dimension_semantics=("arbitrary",),
                                             vmem_limit_bytes=VMEM_LIMIT),
        name="sample_threshold",
    )(scoresT)


def _sample_attn_kernel(pt_ref, q_ref, sc_ref, thr_ref, take_ref, kTnew_ref, vTnew_ref, *refs, nch):
    kpages = refs[:PAGES_PER_STEP]
    vpages = refs[PAGES_PER_STEP:2 * PAGES_PER_STEP]
    o_ref, m_ref, l_ref, acc_ref, seen_ref = refs[2 * PAGES_PER_STEP:]
    c = pl.program_id(1)

    @pl.when(c == 0)
    def _():
        m_ref[...] = jnp.full_like(m_ref, NEG_BIG)
        l_ref[...] = jnp.zeros_like(l_ref)
        acc_ref[...] = jnp.zeros_like(acc_ref)
        seen_ref[...] = jnp.zeros_like(seen_ref)

    thr = thr_ref[0][:, 0:1]
    take = take_ref[0][:, 0:1]

    def attend(kT, vT):
        width = kT.shape[1]
        blk = min(CK, width)
        triu = _tri(blk, lower=False)
        seen = seen_ref[:, 0:1]
        sels = []
        for i in range(width // blk):
            sel, seen = _select_rows(sc_ref[0][:, i * blk:(i + 1) * blk], thr, take, seen, triu)
            sels.append(sel)
        seen_ref[...] = jnp.broadcast_to(seen, seen_ref.shape)
        sel = jnp.tile(jnp.concatenate(sels, axis=1) if len(sels) > 1 else sels[0], (N_HEADS, 1))
        lg = jnp.where(sel, jnp.dot(q_ref[0], kT.astype(BF16), preferred_element_type=F32), -jnp.inf)
        m_old = m_ref[:, 0:1]
        m_new = jnp.maximum(m_old, jnp.max(lg, axis=1, keepdims=True))
        p = jnp.exp(lg - m_new)
        alpha = jnp.exp(m_old - m_new)
        l_new = alpha * l_ref[:, 0:1] + jnp.sum(p, axis=1, keepdims=True)
        pv = lax.dot_general(p.astype(BF16), vT.astype(BF16), (((1,), (1,)), ((), ())),
                             preferred_element_type=F32)
        acc_ref[...] = alpha * acc_ref[...] + pv
        m_ref[...] = jnp.broadcast_to(m_new, m_ref.shape)
        l_ref[...] = jnp.broadcast_to(l_new, l_ref.shape)

    def stack(pages):
        return jnp.concatenate([p[0, 0].reshape(E_A, p.shape[4]) for p in pages], axis=1)

    @pl.when(c < nch)
    def _():
        attend(stack(kpages), stack(vpages))

    @pl.when(c == nch)
    def _():
        attend(kTnew_ref[0], vTnew_ref[0])
        o = acc_ref[...] / l_ref[:, 0:1]
        o_ref[0] = jnp.concatenate(
            [o[h * SROWS:(h + 1) * SROWS, h * HEAD_DIM:(h + 1) * HEAD_DIM] for h in range(N_HEADS)], axis=1)


def _sample_attn(page_table, q_bd, scores, thr, take, kT_new, vT_new, cache_kT, cache_vT, layer):
    db, n_pages = page_table.shape
    nch = n_pages // PAGES_PER_STEP
    page = cache_kT.shape[4]
    sck = PAGES_PER_STEP * page
    hq = N_HEADS * SROWS

    def page_spec(i):
        return pl.BlockSpec(
            (1, 1, N_HEADS, HEAD_DIM, page),
            lambda b, c, pt: (layer, pt[b, jnp.minimum(c, nch - 1) * PAGES_PER_STEP + i], 0, 0, 0))

    per_b = lambda *shape: pl.BlockSpec((1,) + shape, lambda b, c, pt: (b,) + (0,) * len(shape))
    grid_spec = pltpu.PrefetchScalarGridSpec(
        num_scalar_prefetch=1, grid=(db, nch + 1),
        in_specs=[per_b(hq, E_A),
                  pl.BlockSpec((1, SROWS, sck), lambda b, c, pt: (b, 0, c)),
                  per_b(SROWS, LANE), per_b(SROWS, LANE),
                  per_b(E_A, page), per_b(E_A, page)]
                 + [page_spec(i) for i in range(PAGES_PER_STEP)] * 2,
        out_specs=per_b(SROWS, E_A),
        scratch_shapes=[pltpu.VMEM((hq, LANE), F32), pltpu.VMEM((hq, LANE), F32),
                        pltpu.VMEM((hq, E_A), F32), pltpu.VMEM((SROWS, LANE), F32)])
    return pl.pallas_call(
        functools.partial(_sample_attn_kernel, nch=nch),
        grid_spec=grid_spec,
        out_shape=jax.ShapeDtypeStruct((db, SROWS, E_A), F32),
        compiler_params=pltpu.CompilerParams(dimension_semantics=("arbitrary", "arbitrary"),
                                             vmem_limit_bytes=VMEM_LIMIT),
        name="sample_attn",
    )(page_table, q_bd, scores, thr, take, kT_new, vT_new,
      *([cache_kT] * PAGES_PER_STEP), *([cache_vT] * PAGES_PER_STEP))


def _post_kernel(x_ref, oa_ref, g0_ref, mbc_ref, wba_ref, wo_ref, wup_ref, wdn_ref,
                 gpost_ref, gfpre_ref, gfpost_ref, y_ref):
    m = g0_ref[...].astype(F32) * jnp.dot(oa_ref[...], wba_ref[...], preferred_element_type=F32) \
        + mbc_ref[...]
    y = jnp.dot(m.astype(BF16), wo_ref[...], preferred_element_type=F32)
    x1 = x_ref[...] + _rms(y, gpost_ref[...])
    a = jnp.maximum(jnp.dot(_rms(x1, gfpre_ref[...]).astype(BF16), wup_ref[...],
                            preferred_element_type=F32), 0.0)
    f = jnp.dot((a * a).astype(BF16), wdn_ref[...], preferred_element_type=F32)
    y_ref[...] = x1 + _rms(f, gfpost_ref[...])


def _post(x, oa, g0, mbc, wba, wo, wup, wdn, gpost, gfpre, gfpost):
    m = x.shape[0]
    tm = min(ROW_TILE, m)
    row = lambda n: pl.BlockSpec((tm, n), lambda i: (i, 0))
    return pl.pallas_call(
        _post_kernel, grid=(m // tm,),
        in_specs=[row(D_MODEL), row(E_A), row(D_MODEL), row(D_MODEL),
                  _const_spec((E_A, D_MODEL)), _const_spec((D_MODEL, D_MODEL)),
                  _const_spec((D_MODEL, D_FF)), _const_spec((D_FF, D_MODEL)),
                  _const_spec((1, D_MODEL)), _const_spec((1, D_MODEL)), _const_spec((1, D_MODEL))],
        out_specs=row(D_MODEL),
        out_shape=jax.ShapeDtypeStruct((m, D_MODEL), F32),
        compiler_params=pltpu.CompilerParams(dimension_semantics=("arbitrary",),
                                             vmem_limit_bytes=VMEM_LIMIT),
        name="post",
    )(x, oa, g0, mbc, wba, wo, wup, wdn, gpost, gfpre, gfpost)


def _rope_tables(pos):
    half = HEAD_DIM // 2
    inv = ROPE_THETA ** (-jnp.arange(half, dtype=F32) * (2.0 / HEAD_DIM))
    ang = pos[:, None] * inv[None, :]
    c, s = jnp.cos(ang), jnp.sin(ang)
    return jnp.tile(c, (1, LANE // half)), jnp.tile(jnp.concatenate([-s, s], axis=1), (1, LANE // HEAD_DIM))


def _heads_first(a, t):
    db = a.shape[0] // t
    a = a.reshape(db, t, N_HEADS, -1).transpose(0, 2, 1, 3)
    a = jnp.pad(a, ((0, 0), (0, 0), (0, SROWS - t), (0, 0)))
    return a.reshape(db, N_HEADS * SROWS, -1)


def _new_keys_T(a, db, t, page):
    a = jnp.pad(a.reshape(db, t, -1), ((0, 0), (0, page - t), (0, 0)))
    return a.transpose(0, 2, 1)


def kernel(x_prompt, x_sample, cache_k, cache_v, cache_idx_k, state_conv, page_table, norm_mix_pre, norm_mix_post, norm_ffn_pre, norm_ffn_post, w_in, gmlp_ln_g, gmlp_ln_b, gmlp_ws, gmlp_bs, conv_w, conv_b, w_br_attn, w_br_gmlp, w_br_conv, w_out, w_ff_up, w_ff_down):
    depth = w_in.shape[0]
    b, s, _ = x_prompt.shape
    db, t, _ = x_sample.shape
    page = cache_k.shape[2]
    n_pages = page_table.shape[1]
    past = n_pages * page
    ms = db * t
    assert s % ROW_TILE == 0 and s % CK == 0 and ms == CHUNK and CONV_W - 1 <= t <= SROWS
    assert page == LANE and n_pages % PAGES_PER_STEP == 0

    cache_kT = cache_k.transpose(0, 1, 3, 4, 2)
    cache_vT = cache_v.transpose(0, 1, 3, 4, 2)
    cache_iT = cache_idx_k.transpose(0, 1, 3, 2)

    w_in_p = jnp.concatenate(
        [w_in[:, :, :OFF_B], jnp.zeros((depth, D_MODEL, C_B - OFF_B), w_in.dtype), w_in[:, :, OFF_B:]],
        axis=2).astype(BF16)
    causal = jnp.tril(jnp.ones((CHUNK, CHUNK), bool))
    wm = jnp.where(causal[None, None], gmlp_ws, 0)
    wmix_p = wm.astype(BF16)
    bmix_p = jnp.repeat(jnp.swapaxes(gmlp_bs, 1, 2), E_B // GMLP_GROUPS, axis=2)
    eye = jnp.eye(db, dtype=wm.dtype)
    wmix_s = jnp.einsum('ab,lgts->lgatbs', eye, wm[:, :, :t, :t]).reshape(depth, GMLP_GROUPS, ms, ms)
    wmix_s = wmix_s.astype(BF16)
    bmix_s = jnp.tile(bmix_p[:, :t], (1, db, 1))
    wbb, wbc, wba = (w.astype(BF16) for w in (w_br_gmlp, w_br_conv, w_br_attn))
    wo, wup, wdn = (w.astype(BF16) for w in (w_out, w_ff_up, w_ff_down))
    r2 = lambda a, l: a[l][None, :]

    cos_p, sin_p = _rope_tables(jnp.arange(s, dtype=F32))
    cos_s, sin_s = _rope_tables(jnp.tile(jnp.arange(t, dtype=F32) + past, db))
    topk_p = min(TOPK_MAX, s // 4)
    topk_s = min(TOPK_MAX, (past + t) // 4)
    head_eye = jnp.eye(N_HEADS, dtype=BF16)

    xp = x_prompt.reshape(b * s, D_MODEL)
    xs = x_sample.reshape(ms, D_MODEL)
    outs = [[] for _ in range(9)]
    for l in range(depth):
        common = (r2(norm_mix_pre, l), w_in_p[l])
        tail = (r2(gmlp_ln_g, l), r2(gmlp_ln_b, l))
        conv = (conv_w[l], r2(conv_b, l), wbb[l], wbc[l])
        post_w = (wba[l], wo[l], wup[l], wdn[l], r2(norm_mix_post, l), r2(norm_ffn_pre, l),
                  r2(norm_ffn_post, l))
        (qT, qiT, wT, kb, kib, vTb, kT, vT, kiT, g0, mbc, ctail) = _proj(
            xp, *common, cos_p, sin_p, *tail, wmix_p[l], bmix_p[l], *conv, sample=False, seq_len=s)
        oa = _attn_prompt(qT, qiT, wT, kib.reshape(b, s, IDX_DIM), kb.reshape(b, s, E_A), vTb, topk=topk_p)
        xp = _post(xp, oa.reshape(b * s, E_A), g0, mbc, *post_w)
        outs[0].append(kT.reshape(b, N_HEADS, HEAD_DIM, s))
        outs[1].append(vT.reshape(b, N_HEADS, HEAD_DIM, s))
        outs[2].append(kiT)
        outs[3].append(ctail[:, SUBLANE - (CONV_W - 1):, :])
        st = state_conv[l]
        z = jnp.zeros((db, 1, E_C), st.dtype)
        st1 = jnp.concatenate([st[:, 1:2]] + [z] * (t - 1), axis=1)
        st2 = jnp.concatenate([st[:, 0:1], st[:, 1:2]] + [z] * (t - 2), axis=1)
        st12 = jnp.concatenate([st1, st2], axis=2).reshape(ms, 2 * E_C)
        (q_s, k_s, v_s, qi_s, kw_s, g0_s, mbc_s, cin_s, vn_s) = _proj(
            xs, *common, cos_s, sin_s, *tail, wmix_s[l], bmix_s[l], *conv, sample=True, seq_len=t,
            st=st12)
        qi_h = _heads_first(qi_s, t)
        w_h = _heads_first(kw_s[:, IDX_DIM:IDX_DIM + IDX_HEADS][:, :, None], t)
        w_h = jnp.broadcast_to(w_h, w_h.shape[:2] + (LANE,))
        scores = _sample_scores(page_table, qi_h, w_h, _new_keys_T(kw_s[:, :IDX_DIM], db, t, page),
                                cache_iT, l, t_new=t)
        thr, take = _sample_threshold(scores.reshape(db * SROWS, -1).T, topk=topk_s)
        per_q = lambda a: jnp.broadcast_to(a[0].reshape(db, SROWS, 1), (db, SROWS, LANE))
        q4 = jnp.pad(q_s.reshape(db, t, N_HEADS, HEAD_DIM), ((0, 0), (0, SROWS - t), (0, 0), (0, 0)))
        q_bd = jnp.einsum('bqhd,hg->bhqgd', q4, head_eye).reshape(db, N_HEADS * SROWS, E_A)
        oa_s = _sample_attn(page_table, q_bd, scores, per_q(thr), per_q(take),
                            _new_keys_T(k_s, db, t, page), _new_keys_T(v_s, db, t, page),
                            cache_kT, cache_vT, l)
        oa_s = oa_s[:, :t].reshape(ms, E_A).astype(BF16)
        xs = _post(xs, oa_s, g0_s, mbc_s, *post_w)
        outs[4].append(k_s.reshape(db, t, N_HEADS, HEAD_DIM))
        outs[5].append(v_s.reshape(db, t, N_HEADS, HEAD_DIM))
        outs[6].append(kw_s[:, :IDX_DIM].reshape(db, t, IDX_DIM))
        outs[7].append(cin_s.reshape(db, t, E_C)[:, t - (CONV_W - 1):])
        outs[8].append(vn_s.reshape(db, t, E_B))
    stacked = [jnp.stack(o) for o in outs]
    stacked[0] = stacked[0].transpose(0, 1, 4, 2, 3)
    stacked[1] = stacked[1].transpose(0, 1, 4, 2, 3)
    stacked[2] = stacked[2].transpose(0, 1, 3, 2)
    return (xp.reshape(b, s, D_MODEL), xs.reshape(db, t, D_MODEL)) + tuple(stacked)
```

```python
import functools
import math

import jax
import jax.numpy as jnp
from jax import lax
from jax.experimental import pallas as pl
from jax.experimental.pallas import tpu as pltpu

F32 = jnp.float32
BF16 = jnp.bfloat16

D_MODEL = 1024
N_HEADS = 8
HEAD_DIM = 64
E_A = N_HEADS * HEAD_DIM
IDX_HEADS = 8
IDX_DIM = 64
IDX_W_SCALE = 1.0 / math.sqrt(IDX_HEADS * IDX_DIM)
TOPK_MAX = 256
Q_BLOCK = 128
ROPE_THETA = 10000.0
CHUNK = 128
GMLP_GROUPS = 4
E_B = 512
E_C = 512
CONV_W = 3
D_FF = 4 * D_MODEL
EPS = 1e-6

OFF_KI = 3 * E_A + IDX_HEADS * IDX_DIM
OFF_B = OFF_KI + IDX_DIM + IDX_HEADS
LANE = 128
SUBLANE = 8
C_KW = OFF_KI
C_B = C_KW + LANE
C_C = C_B + 2 * E_B
C_G = C_C + 3 * E_C
N_PAD = C_G + 3 * D_MODEL

VMEM_LIMIT = 56 * 1024 * 1024
ROW_TILE = 256
POST_ROW_TILE = 512
NEG_BIG = -1e30
KEY_NEG_INF = -2139095041
KEY_POS_INF = 2139095040
CK = 256
COUNT_ROWS = 2 * CK
PAGES_PER_STEP = 16
SROWS = SUBLANE


def _rms(x, g):
    return x * lax.rsqrt(jnp.mean(x * x, axis=-1, keepdims=True) + EPS) * g


def _rope128(z, cos, sin):
    lane = lax.broadcasted_iota(jnp.int32, z.shape, 1)
    partner = jnp.where((lane % HEAD_DIM) < HEAD_DIM // 2,
                        pltpu.roll(z, LANE - HEAD_DIM // 2, 1),
                        pltpu.roll(z, HEAD_DIM // 2, 1))
    return z * cos + partner * sin


def _rope(z, cos, sin):
    return jnp.concatenate(
        [_rope128(z[:, i:i + LANE], cos, sin) for i in range(0, z.shape[1], LANE)], axis=1)


def _proj_kernel(*refs, sample, tm, seq_rows):
    if sample:
        (x_ref, gpre_ref, w_ref, cos_ref, sin_ref, lng_ref, lnb_ref, wmix_ref, bmix_ref,
         cw_ref, cbias_ref, wbb_ref, wbc_ref, st_ref,
         q_ref, k_ref, v_ref, qi_ref, kw_ref, g0_ref, mbc_ref, cin_ref, vn_ref) = refs
    else:
        (x_ref, gpre_ref, w_ref, cos_ref, sin_ref, lng_ref, lnb_ref, wmix_ref, bmix_ref,
         cw_ref, cbias_ref, wbb_ref, wbc_ref,
         qT_ref, qiT_ref, wT_ref, kb_ref, kib_ref, vTb_ref, kT_ref, vT_ref, kiT_ref,
         g0_ref, mbc_ref, ctail_ref, carry_ref) = refs

    h = _rms(x_ref[...], gpre_ref[...]).astype(BF16)
    cos = cos_ref[...]
    sin = sin_ref[...]

    z = jnp.dot(h, w_ref[:, 0:C_KW], preferred_element_type=F32)
    q = _rope(z[:, 0:E_A], cos, sin) * (HEAD_DIM ** -0.5)
    k = _rope(z[:, E_A:2 * E_A], cos, sin)
    v = z[:, 2 * E_A:3 * E_A]
    qi = _rope(z[:, 3 * E_A:4 * E_A], cos, sin)

    zk = jnp.dot(h, w_ref[:, C_KW:C_B], preferred_element_type=F32)
    lane = lax.broadcasted_iota(jnp.int32, zk.shape, 1)
    kw = jnp.where(lane < IDX_DIM, _rope128(zk, cos, sin), zk * IDX_W_SCALE)
    if sample:
        q_ref[...] = q.astype(BF16)
        k_ref[...] = k
        v_ref[...] = v
        qi_ref[...] = qi.astype(BF16)
        kw_ref[...] = kw
    else:
        qT_ref[0] = q.T.astype(BF16)
        qiT_ref[0] = qi.T.astype(BF16)
        kT = k.T
        kT_ref[0] = kT
        kb_ref[...] = k.astype(BF16)
        vT = v.T
        vT_ref[0] = vT
        vTb_ref[0] = vT.astype(BF16)
        kwT = kw.T
        kiT_ref[0] = kwT[0:IDX_DIM, :]
        wT_ref[0] = kwT[IDX_DIM:IDX_DIM + IDX_HEADS, :]
        kib_ref[...] = kw[:, 0:IDX_DIM].astype(BF16)

    gl = jax.nn.gelu(jnp.dot(h, w_ref[:, C_B:C_C], preferred_element_type=F32))
    u = gl[:, 0:E_B]
    vg = gl[:, E_B:2 * E_B]
    mu = jnp.mean(vg, axis=-1, keepdims=True)
    vc = vg - mu
    vn = vc * lax.rsqrt(jnp.mean(vc * vc, axis=-1, keepdims=True) + EPS) * lng_ref[...] + lnb_ref[...]
    if sample:
        vn_ref[...] = vn
    vnb = vn.astype(BF16)
    gw = E_B // GMLP_GROUPS
    rows = []
    for c in range(tm // CHUNK):
        cols = []
        for g in range(GMLP_GROUPS):
            cols.append(jnp.dot(wmix_ref[g], vnb[c * CHUNK:(c + 1) * CHUNK, g * gw:(g + 1) * gw],
                                preferred_element_type=F32))
        rows.append(jnp.concatenate(cols, axis=1) + bmix_ref[...])
    mix = rows[0] if len(rows) == 1 else jnp.concatenate(rows, axis=0)
    ob = (u * mix).astype(BF16)

    zc = jnp.dot(h, w_ref[:, C_C:C_G], preferred_element_type=F32)
    cb = zc[:, 0:E_C]
    cin = zc[:, E_C:2 * E_C] * zc[:, 2 * E_C:3 * E_C]
    row = lax.broadcasted_iota(jnp.int32, cin.shape, 0)
    if sample:
        cin_ref[...] = cin
        i_in_seq = row % seq_rows
        s1 = jnp.where(i_in_seq >= 1, pltpu.roll(cin, 1, 0), st_ref[:, 0:E_C])
        s2 = jnp.where(i_in_seq >= 2, pltpu.roll(cin, 2, 0), st_ref[:, E_C:2 * E_C])
    else:
        @pl.when(pl.program_id(0) % (seq_rows // tm) == 0)
        def _():
            carry_ref[...] = jnp.zeros_like(carry_ref)
        p2 = carry_ref[0:1, :]
        p1 = carry_ref[1:2, :]
        s1 = jnp.where(row >= 1, pltpu.roll(cin, 1, 0), p1)
        s2 = jnp.where(row >= 2, pltpu.roll(cin, 2, 0), jnp.where(row == 0, p2, p1))
        carry_ref[0:2, :] = cin[tm - 2:tm, :]
        ctail_ref[0] = cin[tm - SUBLANE:tm, :]
    y = s2 * cw_ref[0:1, :] + s1 * cw_ref[1:2, :] + cin * cw_ref[2:3, :] + cbias_ref[...]
    oc = (cb * y).astype(BF16)

    g0_ref[...] = jax.nn.sigmoid(
        jnp.dot(h, w_ref[:, C_G:C_G + D_MODEL], preferred_element_type=F32)).astype(BF16)
    g1 = jax.nn.sigmoid(jnp.dot(h, w_ref[:, C_G + D_MODEL:C_G + 2 * D_MODEL], preferred_element_type=F32))
    mbc = g1 * jnp.dot(ob, wbb_ref[...], preferred_element_type=F32)
    g2 = jax.nn.sigmoid(jnp.dot(h, w_ref[:, C_G + 2 * D_MODEL:N_PAD], preferred_element_type=F32))
    mbc_ref[...] = mbc + g2 * jnp.dot(oc, wbc_ref[...], preferred_element_type=F32)


def _const_spec(shape):
    nd = len(shape)
    return pl.BlockSpec(shape, lambda *_: (0,) * nd, pipeline_mode=pl.Buffered(1))


def _proj(x, gpre, w, cos, sin, lng, lnb, wmix, bmix, cw, cbias, wbb, wbc, *, sample, seq_len,
          st=None):
    m = x.shape[0]
    tm = m if sample else ROW_TILE
    tps = max(seq_len // tm, 1)
    n_seq = m // seq_len
    row = lambda n: pl.BlockSpec((tm, n), lambda i: (i, 0))
    pos_spec = pl.BlockSpec((tm, LANE), (lambda i: (0, 0)) if sample else (lambda i: (i % tps, 0)))
    in_specs = [row(D_MODEL), _const_spec((1, D_MODEL)), _const_spec((D_MODEL, N_PAD)),
                pos_spec, pos_spec, _const_spec((1, E_B)), _const_spec((1, E_B)),
                _const_spec((GMLP_GROUPS, CHUNK, CHUNK)), _const_spec((CHUNK, E_B)),
                _const_spec((CONV_W, E_C)), _const_spec((1, E_C)),
                _const_spec((E_B, D_MODEL)), _const_spec((E_C, D_MODEL))]
    args = [x, gpre, w, cos, sin, lng, lnb, wmix, bmix, cw, cbias, wbb, wbc]
    sds = jax.ShapeDtypeStruct
    if sample:
        in_specs.append(row(2 * E_C))
        args.append(st)
        out_shape = [sds((m, E_A), BF16), sds((m, E_A), F32), sds((m, E_A), F32), sds((m, E_A), BF16),
                     sds((m, LANE), F32), sds((m, D_MODEL), BF16), sds((m, D_MODEL), F32),
                     sds((m, E_C), F32), sds((m, E_B), F32)]
        out_specs = [row(E_A), row(E_A), row(E_A), row(E_A), row(LANE), row(D_MODEL), row(D_MODEL),
                     row(E_C), row(E_B)]
        scratch = []
    else:
        colT = lambda n: pl.BlockSpec((1, n, tm), lambda i: (i // tps, 0, i % tps))
        out_shape = [sds((n_seq, E_A, seq_len), BF16), sds((n_seq, E_A, seq_len), BF16),
                     sds((n_seq, IDX_HEADS, seq_len), F32), sds((m, E_A), BF16), sds((m, IDX_DIM), BF16),
                     sds((n_seq, E_A, seq_len), BF16), sds((n_seq, E_A, seq_len), F32),
                     sds((n_seq, E_A, seq_len), F32), sds((n_seq, IDX_DIM, seq_len), F32),
                     sds((m, D_MODEL), BF16), sds((m, D_MODEL), F32), sds((n_seq, SUBLANE, E_C), F32)]
        out_specs = [colT(E_A), colT(E_A), colT(IDX_HEADS), row(E_A), row(IDX_DIM),
                     colT(E_A), colT(E_A), colT(E_A), colT(IDX_DIM),
                     row(D_MODEL), row(D_MODEL),
                     pl.BlockSpec((1, SUBLANE, E_C), lambda i: (i // tps, 0, 0))]
        scratch = [pltpu.VMEM((SUBLANE, E_C), F32)]
    return pl.pallas_call(
        functools.partial(_proj_kernel, sample=sample, tm=tm, seq_rows=seq_len),
        grid=(m // tm,), in_specs=in_specs, out_specs=out_specs, out_shape=out_shape,
        scratch_shapes=scratch,
        compiler_params=pltpu.CompilerParams(dimension_semantics=("arbitrary",),
                                             vmem_limit_bytes=VMEM_LIMIT),
        name="proj_sample" if sample else "proj_prompt",
    )(*args)


def _key_to_f32(key):
    bits = key ^ ((key >> 31) & jnp.int32(0x7FFFFFFF))
    return lax.bitcast_convert_type(bits, F32)


def _truncate_bf16(x):
    bits = lax.bitcast_convert_type(x, jnp.int32) & jnp.int32(-65536)
    return lax.bitcast_convert_type(bits, F32).astype(BF16)


def _topk_threshold(sc_ref, tb_ref, n_count, k_row):
    lanes = sc_ref.shape[1]
    k_f = k_row.astype(F32)
    acc_rows = COUNT_ROWS // 8

    def count(ref, pred_fn, dtype):
        one, zero = jnp.ones((), dtype), jnp.zeros((), dtype)

        def body(c, acc):
            off = pl.multiple_of(c * COUNT_ROWS, COUNT_ROWS)
            hit = jnp.where(pred_fn(ref[pl.ds(off, COUNT_ROWS), :]), one, zero)
            p = [hit[i * acc_rows:(i + 1) * acc_rows] for i in range(8)]
            return acc + (((p[0] + p[1]) + (p[2] + p[3])) + ((p[4] + p[5]) + (p[6] + p[7])))
        acc = lax.fori_loop(0, n_count, body, jnp.zeros((acc_rows, lanes), dtype))
        return jnp.sum(acc.astype(F32), axis=0, keepdims=True)

    def bisect_high(_, lohi):
        lo, hi = lohi
        mid = (lo + hi) >> 1
        rep = (mid << 16) | jnp.where(mid < 0, jnp.int32(0xFFFF), jnp.int32(0))
        midb = _key_to_f32(rep).astype(BF16)
        ge = count(tb_ref, lambda t: t >= midb, BF16) >= k_f
        return jnp.where(ge, mid, lo), jnp.where(ge, hi, mid)

    def bisect_low(_, lohi):
        lo, hi = lohi
        mid = (lo >> 1) + (hi >> 1) + (lo & hi & 1)
        midf = _key_to_f32(mid)
        ge = count(sc_ref, lambda s: s >= midf, F32) >= k_f
        return jnp.where(ge, mid, lo), jnp.where(ge, hi, mid)

    lo0 = jnp.full((1, lanes), KEY_NEG_INF >> 16, jnp.int32)
    hi0 = jnp.full((1, lanes), (KEY_POS_INF >> 16) + 1, jnp.int32)
    hi_half, _ = lax.fori_loop(0, 16, bisect_high, (lo0, hi0))
    lo, _ = lax.fori_loop(0, 16, bisect_low, (hi_half << 16, (hi_half << 16) + 65536))
    thr = _key_to_f32(lo)
    n_eq_take = k_f - count(sc_ref, lambda s: s > thr, F32)
    return thr, n_eq_take


def _select_cols(s, thr, n_eq_take, eq_seen, tril):
    eq = s == thr
    prefix = jnp.dot(tril, jnp.where(eq, 1.0, 0.0).astype(BF16), preferred_element_type=F32) + eq_seen
    sel = (s > thr) | (eq & (prefix <= n_eq_take))
    return sel, prefix[s.shape[0] - 1:s.shape[0], :]


def _select_rows(s, thr, n_eq_take, eq_seen, triu):
    eq = s == thr
    prefix = jnp.dot(jnp.where(eq, 1.0, 0.0).astype(BF16), triu, preferred_element_type=F32) + eq_seen
    sel = (s > thr) | (eq & (prefix <= n_eq_take))
    return sel, prefix[:, s.shape[1] - 1:s.shape[1]]


def _tri(n, lower):
    r = lax.broadcasted_iota(jnp.int32, (n, n), 0)
    c = lax.broadcasted_iota(jnp.int32, (n, n), 1)
    return jnp.where((c <= r) if lower else (r <= c), 1.0, 0.0).astype(BF16)


def _attn_prompt_kernel(qT_ref, qiT_ref, wT_ref, ki_ref, k_ref, vT_ref, o_ref, sc_ref, tb_ref, acc_ref,
                        lga_ref, lgb_ref, p_ref, wq_ref, tri_ref, *, topk):
    j = pl.program_id(1)
    nck = (j * Q_BLOCK) // CK + 1
    qiT = qiT_ref[0]
    w_idx = jnp.concatenate([qiT[h * IDX_DIM:(h + 1) * IDX_DIM, :] for h in range(IDX_HEADS)], axis=1)
    wT = wT_ref[0]
    q_pos = j * Q_BLOCK + lax.broadcasted_iota(jnp.int32, (1, Q_BLOCK), 1)

    def score_chunk(c, carry):
        off = pl.multiple_of(c * CK, CK)
        d = jnp.dot(ki_ref[0, pl.ds(off, CK), :], w_idx, preferred_element_type=F32)
        acc = jnp.maximum(d[:, 0:Q_BLOCK], 0.0) * wT[0:1, :]
        for h in range(1, IDX_HEADS):
            acc = acc + jnp.maximum(d[:, h * Q_BLOCK:(h + 1) * Q_BLOCK], 0.0) * wT[h:h + 1, :]
        key_pos = off + lax.broadcasted_iota(jnp.int32, (CK, 1), 0)
        acc = jnp.where(key_pos <= q_pos, acc, -jnp.inf)
        sc_ref[pl.ds(off, CK), :] = acc
        tb_ref[pl.ds(off, CK), :] = _truncate_bf16(acc)
        return carry

    lax.fori_loop(0, nck, score_chunk, 0)

    @pl.when(nck % 2 == 1)
    def _():
        off = pl.multiple_of(nck * CK, CK)
        sc_ref[pl.ds(off, CK), :] = jnp.full((CK, Q_BLOCK), -jnp.inf, F32)
        tb_ref[pl.ds(off, CK), :] = jnp.full((CK, Q_BLOCK), -jnp.inf, BF16)

    n_pairs = (nck + 1) // 2
    thr, n_eq_take = _topk_threshold(sc_ref, tb_ref, n_pairs, jnp.minimum(q_pos + 1, topk))

    qT = qT_ref[0].astype(F32)
    upper = lax.broadcasted_iota(jnp.int32, (LANE, Q_BLOCK), 0) < HEAD_DIM
    for h in range(N_HEADS):
        slab = qT[(h // 2) * LANE:(h // 2 + 1) * LANE, :]
        wq_ref[h] = jnp.where(upper if h % 2 == 0 else ~upper, slab, 0.0).astype(BF16)
    tri_ref[...] = _tri(CK, lower=True)
    acc_ref[...] = jnp.zeros_like(acc_ref)

    last_chunk = k_ref.shape[1] // CK - 1

    def chunk_offset(c):
        return pl.multiple_of(jnp.minimum(c, last_chunk) * CK, CK)

    def masked_logits(c, eq_seen, lg_ref):
        off = chunk_offset(c)
        sel, eq_seen = _select_cols(sc_ref[pl.ds(off, CK), :], thr, n_eq_take, eq_seen, tri_ref[...])
        sc_ref[pl.ds(off, CK), :] = jnp.where(sel, 0.0, -jnp.inf)
        for h in range(N_HEADS):
            kh = k_ref[0, pl.ds(off, CK), (h // 2) * LANE:(h // 2 + 1) * LANE]
            lg_ref[h] = jnp.dot(kh, wq_ref[h], preferred_element_type=F32) + sc_ref[pl.ds(off, CK), :]
        return eq_seen

    def softmax_update(c, lg_ref, lg_next_ref, carry):
        eq_seen, m_all, l_all = carry
        m_new = jnp.maximum(m_all, jnp.concatenate(
            [jnp.max(lg_ref[h], axis=0, keepdims=True) for h in range(N_HEADS)], axis=0))
        alpha = jnp.exp(m_all - m_new)
        eq_seen = masked_logits(c + 1, eq_seen, lg_next_ref)
        l_rows = []
        for h in range(N_HEADS):
            p = jnp.exp(lg_ref[h] - m_new[h:h + 1, :])
            l_rows.append(jnp.sum(p, axis=0, keepdims=True))
            p_ref[h] = p.astype(BF16)
        off = chunk_offset(c)
        for h in range(N_HEADS):
            hs = slice(h * HEAD_DIM, (h + 1) * HEAD_DIM)
            acc_ref[hs, :] = alpha[h:h + 1, :] * acc_ref[hs, :] + jnp.dot(
                vT_ref[0, hs, pl.ds(off, CK)], p_ref[h], preferred_element_type=F32)
        return eq_seen, m_new, alpha * l_all + jnp.concatenate(l_rows, axis=0)

    def attend_pair(i, carry):
        carry = softmax_update(2 * i, lga_ref, lgb_ref, carry)
        return softmax_update(2 * i + 1, lgb_ref, lga_ref, carry)

    eq_seen0 = masked_logits(0, jnp.zeros((1, Q_BLOCK), F32), lga_ref)
    init = (eq_seen0, jnp.full((N_HEADS, Q_BLOCK), NEG_BIG, F32), jnp.zeros((N_HEADS, Q_BLOCK), F32))
    _, _, l_all = lax.fori_loop(0, n_pairs, attend_pair, init)
    l_full = jnp.concatenate(
        [jnp.broadcast_to(l_all[h:h + 1, :], (HEAD_DIM, Q_BLOCK)) for h in range(N_HEADS)], axis=0)
    o_ref[0] = (acc_ref[...] / l_full).T.astype(BF16)


def _attn_prompt(qT, qiT, wT, kib, kb, vTb, *, topk):
    b, _, s = qT.shape
    colT = lambda n: pl.BlockSpec((1, n, Q_BLOCK), lambda i, j: (i, 0, j))
    return pl.pallas_call(
        functools.partial(_attn_prompt_kernel, topk=topk),
        grid=(b, s // Q_BLOCK),
        in_specs=[colT(E_A), colT(E_A), colT(IDX_HEADS),
                  pl.BlockSpec((1, s, IDX_DIM), lambda i, j: (i, 0, 0)),
                  pl.BlockSpec((1, s, E_A), lambda i, j: (i, 0, 0)),
                  pl.BlockSpec((1, E_A, s), lambda i, j: (i, 0, 0))],
        out_specs=pl.BlockSpec((1, Q_BLOCK, E_A), lambda i, j: (i, j, 0)),
        out_shape=jax.ShapeDtypeStruct((b, s, E_A), BF16),
        scratch_shapes=[pltpu.VMEM((s, Q_BLOCK), F32), pltpu.VMEM((s, Q_BLOCK), BF16),
                        pltpu.VMEM((E_A, Q_BLOCK), F32),
                        pltpu.VMEM((N_HEADS, CK, Q_BLOCK), F32), pltpu.VMEM((N_HEADS, CK, Q_BLOCK), F32),
                        pltpu.VMEM((N_HEADS, CK, Q_BLOCK), BF16),
                        pltpu.VMEM((N_HEADS, LANE, Q_BLOCK), BF16), pltpu.VMEM((CK, CK), BF16)],
        compiler_params=pltpu.CompilerParams(dimension_semantics=("arbitrary", "arbitrary"),
                                             vmem_limit_bytes=VMEM_LIMIT),
        name="attn_prompt",
    )(qT, qiT, wT, kib, kb, vTb)


def _sample_scores_kernel(pt_ref, qi_ref, w_ref, kinew_ref, *refs, nch, t_new):
    pages = refs[:PAGES_PER_STEP]
    out_ref = refs[PAGES_PER_STEP]
    c = pl.program_id(1)
    qi = qi_ref[0]
    w = w_ref[0][:, 0:1]

    def scores(keysT):
        t = jnp.maximum(jnp.dot(qi, keysT.astype(BF16), preferred_element_type=F32), 0.0) * w
        acc = t[0:SROWS]
        for h in range(1, IDX_HEADS):
            acc = acc + t[h * SROWS:(h + 1) * SROWS]
        return acc

    @pl.when(c < nch)
    def _():
        out_ref[0] = scores(jnp.concatenate([p[0, 0] for p in pages], axis=1))

    @pl.when(c == nch)
    def _():
        sc = scores(kinew_ref[0])
        qrow = lax.broadcasted_iota(jnp.int32, sc.shape, 0)
        kcol = lax.broadcasted_iota(jnp.int32, sc.shape, 1)
        ok = kcol <= jnp.minimum(qrow, t_new - 1)
        out_ref[0] = jnp.concatenate(
            [jnp.where(ok, sc, -jnp.inf),
             jnp.full((SROWS, out_ref.shape[2] - sc.shape[1]), -jnp.inf, F32)], axis=1)


def _sample_scores(page_table, qi_h, w_h, kiT_new, cache_iT, layer, *, t_new):
    db, n_pages = page_table.shape
    nch = n_pages // PAGES_PER_STEP
    page = cache_iT.shape[3]
    sck = PAGES_PER_STEP * page

    def page_spec(i):
        return pl.BlockSpec(
            (1, 1, IDX_DIM, page),
            lambda b, c, pt: (layer, pt[b, jnp.minimum(c, nch - 1) * PAGES_PER_STEP + i], 0, 0))

    per_b = lambda *shape: pl.BlockSpec((1,) + shape, lambda b, c, pt: (b,) + (0,) * len(shape))
    grid_spec = pltpu.PrefetchScalarGridSpec(
        num_scalar_prefetch=1, grid=(db, nch + 1),
        in_specs=[per_b(IDX_HEADS * SROWS, IDX_DIM), per_b(IDX_HEADS * SROWS, LANE), per_b(IDX_DIM, page)]
                 + [page_spec(i) for i in range(PAGES_PER_STEP)],
        out_specs=pl.BlockSpec((1, SROWS, sck), lambda b, c, pt: (b, 0, c)))
    return pl.pallas_call(
        functools.partial(_sample_scores_kernel, nch=nch, t_new=t_new),
        grid_spec=grid_spec,
        out_shape=jax.ShapeDtypeStruct((db, SROWS, (nch + 1) * sck), F32),
        compiler_params=pltpu.CompilerParams(dimension_semantics=("arbitrary", "arbitrary"),
                                             vmem_limit_bytes=VMEM_LIMIT),
        name="sample_scores",
    )(page_table, qi_h, w_h, kiT_new, *([cache_iT] * PAGES_PER_STEP))


def _sample_threshold_kernel(sc_ref, thr_ref, take_ref, tb_ref, *, topk):
    keys, lanes = sc_ref.shape

    def truncate_chunk(c, carry):
        off = pl.multiple_of(c * CK, CK)
        tb_ref[pl.ds(off, CK), :] = _truncate_bf16(sc_ref[pl.ds(off, CK), :])
        return carry

    lax.fori_loop(0, keys // CK, truncate_chunk, 0)
    thr, n_eq_take = _topk_threshold(sc_ref, tb_ref, keys // COUNT_ROWS,
                                     jnp.full((1, lanes), topk, jnp.int32))
    thr_ref[...] = jnp.broadcast_to(thr, thr_ref.shape)
    take_ref[...] = jnp.broadcast_to(n_eq_take, take_ref.shape)


def _sample_threshold(scoresT, *, topk):
    keys, nq = scoresT.shape
    return pl.pallas_call(
        functools.partial(_sample_threshold_kernel, topk=topk),
        grid=(nq // LANE,),
        in_specs=[pl.BlockSpec((keys, LANE), lambda i: (0, i))],
        out_specs=[pl.BlockSpec((SUBLANE, LANE), lambda i: (0, i))] * 2,
        out_shape=[jax.ShapeDtypeStruct((SUBLANE, nq), F32)] * 2,
        scratch_shapes=[pltpu.VMEM((keys, LANE), BF16)],
        compiler_params=pltpu.CompilerParams(dimension_semantics=("arbitrary",),
                                             vmem_limit_bytes=VMEM_LIMIT),
        name="sample_threshold",
    )(scoresT)


def _sample_attn_kernel(pt_ref, q_ref, sc_ref, thr_ref, take_ref, kTnew_ref, vTnew_ref, *refs, nch):
    kpages = refs[:PAGES_PER_STEP]
    vpages = refs[PAGES_PER_STEP:2 * PAGES_PER_STEP]
    o_ref, m_ref, l_ref, acc_ref, seen_ref = refs[2 * PAGES_PER_STEP:]
    c = pl.program_id(1)

    @pl.when(c == 0)
    def _():
        m_ref[...] = jnp.full_like(m_ref, NEG_BIG)
        l_ref[...] = jnp.zeros_like(l_ref)
        acc_ref[...] = jnp.zeros_like(acc_ref)
        seen_ref[...] = jnp.zeros_like(seen_ref)

    thr = thr_ref[0][:, 0:1]
    take = take_ref[0][:, 0:1]

    def attend(kT, vT):
        width = kT.shape[1]
        blk = min(CK, width)
        triu = _tri(blk, lower=False)
        seen = seen_ref[:, 0:1]
        sels = []
        for i in range(width // blk):
            sel, seen = _select_rows(sc_ref[0][:, i * blk:(i + 1) * blk], thr, take, seen, triu)
            sels.append(sel)
        seen_ref[...] = jnp.broadcast_to(seen, seen_ref.shape)
        sel = jnp.tile(jnp.concatenate(sels, axis=1) if len(sels) > 1 else sels[0], (N_HEADS, 1))
        lg = jnp.where(sel, jnp.dot(q_ref[0], kT.astype(BF16), preferred_element_type=F32), -jnp.inf)
        m_old = m_ref[:, 0:1]
        m_new = jnp.maximum(m_old, jnp.max(lg, axis=1, keepdims=True))
        p = jnp.exp(lg - m_new)
        alpha = jnp.exp(m_old - m_new)
        l_new = alpha * l_ref[:, 0:1] + jnp.sum(p, axis=1, keepdims=True)
        pv = lax.dot_general(p.astype(BF16), vT.astype(BF16), (((1,), (1,)), ((), ())),
                             preferred_element_type=F32)
        acc_ref[...] = alpha * acc_ref[...] + pv
        m_ref[...] = jnp.broadcast_to(m_new, m_ref.shape)
        l_ref[...] = jnp.broadcast_to(l_new, l_ref.shape)

    def stack(pages):
        return jnp.concatenate([p[0, 0].reshape(E_A, p.shape[4]) for p in pages], axis=1)

    @pl.when(c < nch)
    def _():
        attend(stack(kpages), stack(vpages))

    @pl.when(c == nch)
    def _():
        attend(kTnew_ref[0], vTnew_ref[0])
        o = acc_ref[...] / l_ref[:, 0:1]
        o_ref[0] = jnp.concatenate(
            [o[h * SROWS:(h + 1) * SROWS, h * HEAD_DIM:(h + 1) * HEAD_DIM] for h in range(N_HEADS)], axis=1)


def _sample_attn(page_table, q_bd, scores, thr, take, kT_new, vT_new, cache_kT, cache_vT, layer):
    db, n_pages = page_table.shape
    nch = n_pages // PAGES_PER_STEP
    page = cache_kT.shape[4]
    sck = PAGES_PER_STEP * page
    hq = N_HEADS * SROWS

    def page_spec(i):
        return pl.BlockSpec(
            (1, 1, N_HEADS, HEAD_DIM, page),
            lambda b, c, pt: (layer, pt[b, jnp.minimum(c, nch - 1) * PAGES_PER_STEP + i], 0, 0, 0))

    per_b = lambda *shape: pl.BlockSpec((1,) + shape, lambda b, c, pt: (b,) + (0,) * len(shape))
    grid_spec = pltpu.PrefetchScalarGridSpec(
        num_scalar_prefetch=1, grid=(db, nch + 1),
        in_specs=[per_b(hq, E_A),
                  pl.BlockSpec((1, SROWS, sck), lambda b, c, pt: (b, 0, c)),
                  per_b(SROWS, LANE), per_b(SROWS, LANE),
                  per_b(E_A, page), per_b(E_A, page)]
                 + [page_spec(i) for i in range(PAGES_PER_STEP)] * 2,
        out_specs=per_b(SROWS, E_A),
        scratch_shapes=[pltpu.VMEM((hq, LANE), F32), pltpu.VMEM((hq, LANE), F32),
                        pltpu.VMEM((hq, E_A), F32), pltpu.VMEM((SROWS, LANE), F32)])
    return pl.pallas_call(
        functools.partial(_sample_attn_kernel, nch=nch),
        grid_spec=grid_spec,
        out_shape=jax.ShapeDtypeStruct((db, SROWS, E_A), F32),
        compiler_params=pltpu.CompilerParams(dimension_semantics=("arbitrary", "arbitrary"),
                                             vmem_limit_bytes=VMEM_LIMIT),
        name="sample_attn",
    )(page_table, q_bd, scores, thr, take, kT_new, vT_new,
      *([cache_kT] * PAGES_PER_STEP), *([cache_vT] * PAGES_PER_STEP))


def _post_kernel(x_ref, oa_ref, g0_ref, mbc_ref, wba_ref, wo_ref, wup_ref, wdn_ref,
                 gpost_ref, gfpre_ref, gfpost_ref, y_ref):
    m = g0_ref[...].astype(F32) * jnp.dot(oa_ref[...], wba_ref[...], preferred_element_type=F32) \
        + mbc_ref[...]
    y = jnp.dot(m.astype(BF16), wo_ref[...], preferred_element_type=F32)
    x1 = x_ref[...] + _rms(y, gpost_ref[...])
    a = jnp.maximum(jnp.dot(_rms(x1, gfpre_ref[...]).astype(BF16), wup_ref[...],
                            preferred_element_type=F32), 0.0)
    f = jnp.dot((a * a).astype(BF16), wdn_ref[...], preferred_element_type=F32)
    y_ref[...] = x1 + _rms(f, gfpost_ref[...])


def _post(x, oa, g0, mbc, wba, wo, wup, wdn, gpost, gfpre, gfpost):
    m = x.shape[0]
    tm = min(POST_ROW_TILE, m)
    row = lambda n: pl.BlockSpec((tm, n), lambda i: (i, 0))
    return pl.pallas_call(
        _post_kernel, grid=(m // tm,),
        in_specs=[row(D_MODEL), row(E_A), row(D_MODEL), row(D_MODEL),
                  _const_spec((E_A, D_MODEL)), _const_spec((D_MODEL, D_MODEL)),
                  _const_spec((D_MODEL, D_FF)), _const_spec((D_FF, D_MODEL)),
                  _const_spec((1, D_MODEL)), _const_spec((1, D_MODEL)), _const_spec((1, D_MODEL))],
        out_specs=row(D_MODEL),
        out_shape=jax.ShapeDtypeStruct((m, D_MODEL), F32),
        compiler_params=pltpu.CompilerParams(dimension_semantics=("arbitrary",),
                                             vmem_limit_bytes=VMEM_LIMIT),
        name="post",
    )(x, oa, g0, mbc, wba, wo, wup, wdn, gpost, gfpre, gfpost)


def _rope_tables(pos):
    half = HEAD_DIM // 2
    inv = ROPE_THETA ** (-jnp.arange(half, dtype=F32) * (2.0 / HEAD_DIM))
    ang = pos[:, None] * inv[None, :]
    c, s = jnp.cos(ang), jnp.sin(ang)
    return jnp.tile(c, (1, LANE // half)), jnp.tile(jnp.concatenate([-s, s], axis=1), (1, LANE // HEAD_DIM))


def _heads_first(a, t):
    db = a.shape[0] // t
    a = a.reshape(db, t, N_HEADS, -1).transpose(0, 2, 1, 3)
    a = jnp.pad(a, ((0, 0), (0, 0), (0, SROWS - t), (0, 0)))
    return a.reshape(db, N_HEADS * SROWS, -1)


def _new_keys_T(a, db, t, page):
    a = jnp.pad(a.reshape(db, t, -1), ((0, 0), (0, page - t), (0, 0)))
    return a.transpose(0, 2, 1)


def kernel(x_prompt, x_sample, cache_k, cache_v, cache_idx_k, state_conv, page_table, norm_mix_pre, norm_mix_post, norm_ffn_pre, norm_ffn_post, w_in, gmlp_ln_g, gmlp_ln_b, gmlp_ws, gmlp_bs, conv_w, conv_b, w_br_attn, w_br_gmlp, w_br_conv, w_out, w_ff_up, w_ff_down):
    depth = w_in.shape[0]
    b, s, _ = x_prompt.shape
    db, t, _ = x_sample.shape
    page = cache_k.shape[2]
    n_pages = page_table.shape[1]
    past = n_pages * page
    ms = db * t
    assert s % POST_ROW_TILE == 0 and s % (2 * CK) == 0 and ms == CHUNK and CONV_W - 1 <= t <= SROWS
    assert page == LANE and n_pages % PAGES_PER_STEP == 0

    cache_kT = cache_k.transpose(0, 1, 3, 4, 2)
    cache_vT = cache_v.transpose(0, 1, 3, 4, 2)
    cache_iT = cache_idx_k.transpose(0, 1, 3, 2)

    w_in_p = jnp.concatenate(
        [w_in[:, :, :OFF_B], jnp.zeros((depth, D_MODEL, C_B - OFF_B), w_in.dtype), w_in[:, :, OFF_B:]],
        axis=2).astype(BF16)
    causal = jnp.tril(jnp.ones((CHUNK, CHUNK), bool))
    wm = jnp.where(causal[None, None], gmlp_ws, 0)
    wmix_p = wm.astype(BF16)
    bmix_p = jnp.repeat(jnp.swapaxes(gmlp_bs, 1, 2), E_B // GMLP_GROUPS, axis=2)
    eye = jnp.eye(db, dtype=wm.dtype)
    wmix_s = jnp.einsum('ab,lgts->lgatbs', eye, wm[:, :, :t, :t]).reshape(depth, GMLP_GROUPS, ms, ms)
    wmix_s = wmix_s.astype(BF16)
    bmix_s = jnp.tile(bmix_p[:, :t], (1, db, 1))
    wbb, wbc, wba = (w.astype(BF16) for w in (w_br_gmlp, w_br_conv, w_br_attn))
    wo, wup, wdn = (w.astype(BF16) for w in (w_out, w_ff_up, w_ff_down))
    r2 = lambda a, l: a[l][None, :]

    cos_p, sin_p = _rope_tables(jnp.arange(s, dtype=F32))
    cos_s, sin_s = _rope_tables(jnp.tile(jnp.arange(t, dtype=F32) + past, db))
    topk_p = min(TOPK_MAX, s // 4)
    topk_s = min(TOPK_MAX, (past + t) // 4)
    head_eye = jnp.eye(N_HEADS, dtype=BF16)

    xp = x_prompt.reshape(b * s, D_MODEL)
    xs = x_sample.reshape(ms, D_MODEL)
    outs = [[] for _ in range(9)]
    for l in range(depth):
        common = (r2(norm_mix_pre, l), w_in_p[l])
        tail = (r2(gmlp_ln_g, l), r2(gmlp_ln_b, l))
        conv = (conv_w[l], r2(conv_b, l), wbb[l], wbc[l])
        post_w = (wba[l], wo[l], wup[l], wdn[l], r2(norm_mix_post, l), r2(norm_ffn_pre, l),
                  r2(norm_ffn_post, l))
        (qT, qiT, wT, kb, kib, vTb, kT, vT, kiT, g0, mbc, ctail) = _proj(
            xp, *common, cos_p, sin_p, *tail, wmix_p[l], bmix_p[l], *conv, sample=False, seq_len=s)
        oa = _attn_prompt(qT, qiT, wT, kib.reshape(b, s, IDX_DIM), kb.reshape(b, s, E_A), vTb, topk=topk_p)
        xp = _post(xp, oa.reshape(b * s, E_A), g0, mbc, *post_w)
        outs[0].append(kT.reshape(b, N_HEADS, HEAD_DIM, s))
        outs[1].append(vT.reshape(b, N_HEADS, HEAD_DIM, s))
        outs[2].append(kiT)
        outs[3].append(ctail[:, SUBLANE - (CONV_W - 1):, :])
        st = state_conv[l]
        z = jnp.zeros((db, 1, E_C), st.dtype)
        st1 = jnp.concatenate([st[:, 1:2]] + [z] * (t - 1), axis=1)
        st2 = jnp.concatenate([st[:, 0:1], st[:, 1:2]] + [z] * (t - 2), axis=1)
        st12 = jnp.concatenate([st1, st2], axis=2).reshape(ms, 2 * E_C)
        (q_s, k_s, v_s, qi_s, kw_s, g0_s, mbc_s, cin_s, vn_s) = _proj(
            xs, *common, cos_s, sin_s, *tail, wmix_s[l], bmix_s[l], *conv, sample=True, seq_len=t,
            st=st12)
        qi_h = _heads_first(qi_s, t)
        w_h = _heads_first(kw_s[:, IDX_DIM:IDX_DIM + IDX_HEADS][:, :, None], t)
        w_h = jnp.broadcast_to(w_h, w_h.shape[:2] + (LANE,))
        scores = _sample_scores(page_table, qi_h, w_h, _new_keys_T(kw_s[:, :IDX_DIM], db, t, page),
                                cache_iT, l, t_new=t)
        thr, take = _sample_threshold(scores.reshape(db * SROWS, -1).T, topk=topk_s)
        per_q = lambda a: jnp.broadcast_to(a[0].reshape(db, SROWS, 1), (db, SROWS, LANE))
        q4 = jnp.pad(q_s.reshape(db, t, N_HEADS, HEAD_DIM), ((0, 0), (0, SROWS - t), (0, 0), (0, 0)))
        q_bd = jnp.einsum('bqhd,hg->bhqgd', q4, head_eye).reshape(db, N_HEADS * SROWS, E_A)
        oa_s = _sample_attn(page_table, q_bd, scores, per_q(thr), per_q(take),
                            _new_keys_T(k_s, db, t, page), _new_keys_T(v_s, db, t, page),
                            cache_kT, cache_vT, l)
        oa_s = oa_s[:, :t].reshape(ms, E_A).astype(BF16)
        xs = _post(xs, oa_s, g0_s, mbc_s, *post_w)
        outs[4].append(k_s.reshape(db, t, N_HEADS, HEAD_DIM))
        outs[5].append(v_s.reshape(db, t, N_HEADS, HEAD_DIM))
        outs[6].append(kw_s[:, :IDX_DIM].reshape(db, t, IDX_DIM))
        outs[7].append(cin_s.reshape(db, t, E_C)[:, t - (CONV_W - 1):])
        outs[8].append(vn_s.reshape(db, t, E_B))
    stacked = [jnp.stack(o) for o in outs]
    stacked[0] = stacked[0].transpose(0, 1, 4, 2, 3)
    stacked[1] = stacked[1].transpose(0, 1, 4, 2, 3)
    stacked[2] = stacked[2].transpose(0, 1, 3, 2)
    return (xp.reshape(b, s, D_MODEL), xs.reshape(db, t, D_MODEL)) + tuple(stacked)
```

```python
import functools
import math

import jax
import jax.numpy as jnp
from jax import lax
from jax.experimental import pallas as pl
from jax.experimental.pallas import tpu as pltpu

F32 = jnp.float32
BF16 = jnp.bfloat16

D_MODEL = 1024
N_HEADS = 8
HEAD_DIM = 64
E_A = N_HEADS * HEAD_DIM
IDX_HEADS = 8
IDX_DIM = 64
IDX_W_SCALE = 1.0 / math.sqrt(IDX_HEADS * IDX_DIM)
TOPK_MAX = 256
Q_BLOCK = 256
ROPE_THETA = 10000.0
CHUNK = 128
GMLP_GROUPS = 4
E_B = 512
E_C = 512
CONV_W = 3
D_FF = 4 * D_MODEL
EPS = 1e-6

OFF_KI = 3 * E_A + IDX_HEADS * IDX_DIM
OFF_B = OFF_KI + IDX_DIM + IDX_HEADS
LANE = 128
SUBLANE = 8
C_KW = OFF_KI
C_B = C_KW + LANE
C_C = C_B + 2 * E_B
C_G = C_C + 3 * E_C
N_PAD = C_G + 3 * D_MODEL

VMEM_LIMIT = 56 * 1024 * 1024
ROW_TILE = 256
POST_ROW_TILE = 512
NEG_BIG = -1e30
KEY_NEG_INF = -2139095041
KEY_POS_INF = 2139095040
CK = 256
COUNT_ELEMS = 2 * CK * LANE
ACC_ROWS = HEAD_DIM + 16
LOG2_E = math.log2(math.e)
PAGES_PER_STEP = 16
SROWS = SUBLANE


def _rms(x, g):
    return x * lax.rsqrt(jnp.mean(x * x, axis=-1, keepdims=True) + EPS) * g


def _rope128(z, cos, sin):
    lane = lax.broadcasted_iota(jnp.int32, z.shape, 1)
    partner = jnp.where((lane % HEAD_DIM) < HEAD_DIM // 2,
                        pltpu.roll(z, LANE - HEAD_DIM // 2, 1),
                        pltpu.roll(z, HEAD_DIM // 2, 1))
    return z * cos + partner * sin


def _rope(z, cos, sin):
    return jnp.concatenate(
        [_rope128(z[:, i:i + LANE], cos, sin) for i in range(0, z.shape[1], LANE)], axis=1)


def _proj_kernel(*refs, sample, tm, seq_rows):
    if sample:
        (x_ref, gpre_ref, w_ref, cos_ref, sin_ref, lng_ref, lnb_ref, wmix_ref, bmix_ref,
         cw_ref, cbias_ref, wbb_ref, wbc_ref, st_ref,
         q_ref, k_ref, v_ref, qi_ref, kw_ref, g0_ref, mbc_ref, cin_ref, vn_ref) = refs
    else:
        (x_ref, gpre_ref, w_ref, cos_ref, sin_ref, lng_ref, lnb_ref, wmix_ref, bmix_ref,
         cw_ref, cbias_ref, wbb_ref, wbc_ref,
         qT_ref, qiT_ref, wT_ref, kb_ref, kib_ref, vTb_ref, kT_ref, vT_ref, kiT_ref,
         g0_ref, mbc_ref, ctail_ref, carry_ref) = refs

    h = _rms(x_ref[...], gpre_ref[...]).astype(BF16)
    cos = cos_ref[...]
    sin = sin_ref[...]

    z = jnp.dot(h, w_ref[:, 0:C_KW], preferred_element_type=F32)
    q = _rope(z[:, 0:E_A], cos, sin) * (HEAD_DIM ** -0.5 * (1.0 if sample else LOG2_E))
    k = _rope(z[:, E_A:2 * E_A], cos, sin)
    v = z[:, 2 * E_A:3 * E_A]
    qi = _rope(z[:, 3 * E_A:4 * E_A], cos, sin)

    zk = jnp.dot(h, w_ref[:, C_KW:C_B], preferred_element_type=F32)
    lane = lax.broadcasted_iota(jnp.int32, zk.shape, 1)
    kw = jnp.where(lane < IDX_DIM, _rope128(zk, cos, sin), zk * IDX_W_SCALE)
    if sample:
        q_ref[...] = q.astype(BF16)
        k_ref[...] = k
        v_ref[...] = v
        qi_ref[...] = qi.astype(BF16)
        kw_ref[...] = kw
    else:
        qT_ref[0] = q.T.astype(BF16)
        qiT_ref[0] = qi.T.astype(BF16)
        kT = k.T
        kT_ref[0] = kT
        kb_ref[...] = k.astype(BF16)
        vT = v.T
        vT_ref[0] = vT
        vTb_ref[0] = vT.astype(BF16)
        kwT = kw.T
        kiT_ref[0] = kwT[0:IDX_DIM, :]
        wT_ref[0] = kwT[IDX_DIM:IDX_DIM + IDX_HEADS, :]
        kib_ref[...] = kw[:, 0:IDX_DIM].astype(BF16)

    gl = jax.nn.gelu(jnp.dot(h, w_ref[:, C_B:C_C], preferred_element_type=F32))
    u = gl[:, 0:E_B]
    vg = gl[:, E_B:2 * E_B]
    mu = jnp.mean(vg, axis=-1, keepdims=True)
    vc = vg - mu
    vn = vc * lax.rsqrt(jnp.mean(vc * vc, axis=-1, keepdims=True) + EPS) * lng_ref[...] + lnb_ref[...]
    if sample:
        vn_ref[...] = vn
    vnb = vn.astype(BF16)
    gw = E_B // GMLP_GROUPS
    rows = []
    for c in range(tm // CHUNK):
        cols = []
        for g in range(GMLP_GROUPS):
            cols.append(jnp.dot(wmix_ref[g], vnb[c * CHUNK:(c + 1) * CHUNK, g * gw:(g + 1) * gw],
                                preferred_element_type=F32))
        rows.append(jnp.concatenate(cols, axis=1) + bmix_ref[...])
    mix = rows[0] if len(rows) == 1 else jnp.concatenate(rows, axis=0)
    ob = (u * mix).astype(BF16)

    zc = jnp.dot(h, w_ref[:, C_C:C_G], preferred_element_type=F32)
    cb = zc[:, 0:E_C]
    cin = zc[:, E_C:2 * E_C] * zc[:, 2 * E_C:3 * E_C]
    row = lax.broadcasted_iota(jnp.int32, cin.shape, 0)
    if sample:
        cin_ref[...] = cin
        i_in_seq = row % seq_rows
        s1 = jnp.where(i_in_seq >= 1, pltpu.roll(cin, 1, 0), st_ref[:, 0:E_C])
        s2 = jnp.where(i_in_seq >= 2, pltpu.roll(cin, 2, 0), st_ref[:, E_C:2 * E_C])
    else:
        @pl.when(pl.program_id(0) % (seq_rows // tm) == 0)
        def _():
            carry_ref[...] = jnp.zeros_like(carry_ref)
        p2 = carry_ref[0:1, :]
        p1 = carry_ref[1:2, :]
        s1 = jnp.where(row >= 1, pltpu.roll(cin, 1, 0), p1)
        s2 = jnp.where(row >= 2, pltpu.roll(cin, 2, 0), jnp.where(row == 0, p2, p1))
        carry_ref[0:2, :] = cin[tm - 2:tm, :]
        ctail_ref[0] = cin[tm - SUBLANE:tm, :]
    y = s2 * cw_ref[0:1, :] + s1 * cw_ref[1:2, :] + cin * cw_ref[2:3, :] + cbias_ref[...]
    oc = (cb * y).astype(BF16)

    g0_ref[...] = jax.nn.sigmoid(
        jnp.dot(h, w_ref[:, C_G:C_G + D_MODEL], preferred_element_type=F32)).astype(BF16)
    g1 = jax.nn.sigmoid(jnp.dot(h, w_ref[:, C_G + D_MODEL:C_G + 2 * D_MODEL], preferred_element_type=F32))
    mbc = g1 * jnp.dot(ob, wbb_ref[...], preferred_element_type=F32)
    g2 = jax.nn.sigmoid(jnp.dot(h, w_ref[:, C_G + 2 * D_MODEL:N_PAD], preferred_element_type=F32))
    mbc_ref[...] = mbc + g2 * jnp.dot(oc, wbc_ref[...], preferred_element_type=F32)


def _const_spec(shape):
    nd = len(shape)
    return pl.BlockSpec(shape, lambda *_: (0,) * nd, pipeline_mode=pl.Buffered(1))


def _proj(x, gpre, w, cos, sin, lng, lnb, wmix, bmix, cw, cbias, wbb, wbc, *, sample, seq_len,
          st=None):
    m = x.shape[0]
    tm = m if sample else ROW_TILE
    tps = max(seq_len // tm, 1)
    n_seq = m // seq_len
    row = lambda n: pl.BlockSpec((tm, n), lambda i: (i, 0))
    pos_spec = pl.BlockSpec((tm, LANE), (lambda i: (0, 0)) if sample else (lambda i: (i % tps, 0)))
    in_specs = [row(D_MODEL), _const_spec((1, D_MODEL)), _const_spec((D_MODEL, N_PAD)),
                pos_spec, pos_spec, _const_spec((1, E_B)), _const_spec((1, E_B)),
                _const_spec((GMLP_GROUPS, CHUNK, CHUNK)), _const_spec((CHUNK, E_B)),
                _const_spec((CONV_W, E_C)), _const_spec((1, E_C)),
                _const_spec((E_B, D_MODEL)), _const_spec((E_C, D_MODEL))]
    args = [x, gpre, w, cos, sin, lng, lnb, wmix, bmix, cw, cbias, wbb, wbc]
    sds = jax.ShapeDtypeStruct
    if sample:
        in_specs.append(row(2 * E_C))
        args.append(st)
        out_shape = [sds((m, E_A), BF16), sds((m, E_A), F32), sds((m, E_A), F32), sds((m, E_A), BF16),
                     sds((m, LANE), F32), sds((m, D_MODEL), BF16), sds((m, D_MODEL), F32),
                     sds((m, E_C), F32), sds((m, E_B), F32)]
        out_specs = [row(E_A), row(E_A), row(E_A), row(E_A), row(LANE), row(D_MODEL), row(D_MODEL),
                     row(E_C), row(E_B)]
        scratch = []
    else:
        colT = lambda n: pl.BlockSpec((1, n, tm), lambda i: (i // tps, 0, i % tps))
        out_shape = [sds((n_seq, E_A, seq_len), BF16), sds((n_seq, E_A, seq_len), BF16),
                     sds((n_seq, IDX_HEADS, seq_len), F32), sds((m, E_A), BF16), sds((m, IDX_DIM), BF16),
                     sds((n_seq, E_A, seq_len), BF16), sds((n_seq, E_A, seq_len), F32),
                     sds((n_seq, E_A, seq_len), F32), sds((n_seq, IDX_DIM, seq_len), F32),
                     sds((m, D_MODEL), BF16), sds((m, D_MODEL), F32), sds((n_seq, SUBLANE, E_C), F32)]
        out_specs = [colT(E_A), colT(E_A), colT(IDX_HEADS), row(E_A), row(IDX_DIM),
                     colT(E_A), colT(E_A), colT(E_A), colT(IDX_DIM),
                     row(D_MODEL), row(D_MODEL),
                     pl.BlockSpec((1, SUBLANE, E_C), lambda i: (i // tps, 0, 0))]
        scratch = [pltpu.VMEM((SUBLANE, E_C), F32)]
    return pl.pallas_call(
        functools.partial(_proj_kernel, sample=sample, tm=tm, seq_rows=seq_len),
        grid=(m // tm,), in_specs=in_specs, out_specs=out_specs, out_shape=out_shape,
        scratch_shapes=scratch,
        compiler_params=pltpu.CompilerParams(dimension_semantics=("arbitrary",),
                                             vmem_limit_bytes=VMEM_LIMIT),
        name="proj_sample" if sample else "proj_prompt",
    )(*args)


def _key_to_f32(key):
    bits = key ^ ((key >> 31) & jnp.int32(0x7FFFFFFF))
    return lax.bitcast_convert_type(bits, F32)


def _truncate_bf16(x):
    bits = lax.bitcast_convert_type(x, jnp.int32) & jnp.int32(-65536)
    return lax.bitcast_convert_type(bits, F32).astype(BF16)


def _topk_threshold(sc_ref, tb_ref, n_chunks, k_row):
    lanes = sc_ref.shape[1]
    k_f = k_row.astype(F32)
    count_rows = COUNT_ELEMS // lanes
    n_count = n_chunks * CK // count_rows
    acc_rows = count_rows // 8

    def count(ref, pred_fn, dtype):
        one, zero = jnp.ones((), dtype), jnp.zeros((), dtype)

        def body(c, acc):
            off = pl.multiple_of(c * count_rows, count_rows)
            hit = jnp.where(pred_fn(ref[pl.ds(off, count_rows), :]), one, zero)
            p = [hit[i * acc_rows:(i + 1) * acc_rows] for i in range(8)]
            return acc + (((p[0] + p[1]) + (p[2] + p[3])) + ((p[4] + p[5]) + (p[6] + p[7])))
        acc = lax.fori_loop(0, n_count, body, jnp.zeros((acc_rows, lanes), dtype))
        return jnp.sum(acc.astype(F32), axis=0, keepdims=True)

    def bisect_high(_, lohi):
        lo, hi = lohi
        mid = (lo + hi) >> 1
        rep = (mid << 16) | jnp.where(mid < 0, jnp.int32(0xFFFF), jnp.int32(0))
        midb = _key_to_f32(rep).astype(BF16)
        ge = count(tb_ref, lambda t: t >= midb, BF16) >= k_f
        return jnp.where(ge, mid, lo), jnp.where(ge, hi, mid)

    def bisect_low(_, lohi):
        lo, hi = lohi
        mid = (lo >> 1) + (hi >> 1) + (lo & hi & 1)
        midf = _key_to_f32(mid)
        ge = count(sc_ref, lambda s: s >= midf, F32) >= k_f
        return jnp.where(ge, mid, lo), jnp.where(ge, hi, mid)

    lo0 = jnp.full((1, lanes), KEY_NEG_INF >> 16, jnp.int32)
    hi0 = jnp.full((1, lanes), (KEY_POS_INF >> 16) + 1, jnp.int32)
    hi_half, _ = lax.fori_loop(0, 16, bisect_high, (lo0, hi0))
    lo, _ = lax.fori_loop(0, 16, bisect_low, (hi_half << 16, (hi_half << 16) + 65536))
    thr = _key_to_f32(lo)
    n_eq_take = k_f - count(sc_ref, lambda s: s > thr, F32)
    return thr, n_eq_take


def _select_cols(s, thr, n_eq_take, eq_seen, tril):
    eq = s == thr
    prefix = jnp.dot(tril, jnp.where(eq, 1.0, 0.0).astype(BF16), preferred_element_type=F32) + eq_seen
    sel = (s > thr) | (eq & (prefix <= n_eq_take))
    return sel, prefix[s.shape[0] - 1:s.shape[0], :]


def _select_rows(s, thr, n_eq_take, eq_seen, triu):
    eq = s == thr
    prefix = jnp.dot(jnp.where(eq, 1.0, 0.0).astype(BF16), triu, preferred_element_type=F32) + eq_seen
    sel = (s > thr) | (eq & (prefix <= n_eq_take))
    return sel, prefix[:, s.shape[1] - 1:s.shape[1]]


def _tri(n, lower):
    r = lax.broadcasted_iota(jnp.int32, (n, n), 0)
    c = lax.broadcasted_iota(jnp.int32, (n, n), 1)
    return jnp.where((c <= r) if lower else (r <= c), 1.0, 0.0).astype(BF16)


def _attn_prompt_kernel(qT_ref, qiT_ref, wT_ref, ki_ref, k_ref, vT_ref, o_ref, sc_ref, tb_ref, acc_ref,
                        lga_ref, lgb_ref, wq_ref, tri_ref, *, topk):
    j = pl.program_id(1)
    nck = (j * Q_BLOCK) // CK + 1
    qiT = qiT_ref[0]
    w_idx = jnp.concatenate([qiT[h * IDX_DIM:(h + 1) * IDX_DIM, :] for h in range(IDX_HEADS)], axis=1)
    wT = wT_ref[0]
    q_pos = j * Q_BLOCK + lax.broadcasted_iota(jnp.int32, (1, Q_BLOCK), 1)

    def score_chunk(c, carry):
        off = pl.multiple_of(c * CK, CK)
        d = jnp.dot(ki_ref[0, pl.ds(off, CK), :], w_idx, preferred_element_type=F32)
        acc = jnp.maximum(d[:, 0:Q_BLOCK], 0.0) * wT[0:1, :]
        for h in range(1, IDX_HEADS):
            acc = acc + jnp.maximum(d[:, h * Q_BLOCK:(h + 1) * Q_BLOCK], 0.0) * wT[h:h + 1, :]
        key_pos = off + lax.broadcasted_iota(jnp.int32, (CK, 1), 0)
        acc = jnp.where(key_pos <= q_pos, acc, -jnp.inf)
        sc_ref[pl.ds(off, CK), :] = acc
        tb_ref[pl.ds(off, CK), :] = _truncate_bf16(acc)
        return carry

    lax.fori_loop(0, nck, score_chunk, 0)

    @pl.when(nck % 2 == 1)
    def _():
        off = pl.multiple_of(nck * CK, CK)
        sc_ref[pl.ds(off, CK), :] = jnp.full((CK, Q_BLOCK), -jnp.inf, F32)
        tb_ref[pl.ds(off, CK), :] = jnp.full((CK, Q_BLOCK), -jnp.inf, BF16)

    n_pairs = (nck + 1) // 2
    thr, n_eq_take = _topk_threshold(sc_ref, tb_ref, 2 * n_pairs, jnp.minimum(q_pos + 1, topk))

    qT = qT_ref[0].astype(F32)
    upper = lax.broadcasted_iota(jnp.int32, (LANE, Q_BLOCK), 0) < HEAD_DIM
    for h in range(N_HEADS):
        slab = qT[(h // 2) * LANE:(h // 2 + 1) * LANE, :]
        wq_ref[h] = jnp.where(upper if h % 2 == 0 else ~upper, slab, 0.0).astype(BF16)
    tri_ref[...] = _tri(CK, lower=True)
    acc_ref[...] = jnp.zeros_like(acc_ref)

    last_chunk = k_ref.shape[1] // CK - 1

    def chunk_offset(c):
        return pl.multiple_of(jnp.minimum(c, last_chunk) * CK, CK)

    def masked_logits(c, eq_seen, lg_ref):
        off = chunk_offset(c)
        sel, eq_seen = _select_cols(sc_ref[pl.ds(off, CK), :], thr, n_eq_take, eq_seen, tri_ref[...])
        sc_ref[pl.ds(off, CK), :] = jnp.where(sel, 0.0, -jnp.inf)
        for h in range(N_HEADS):
            kh = k_ref[0, pl.ds(off, CK), (h // 2) * LANE:(h // 2 + 1) * LANE]
            lg_ref[h] = jnp.dot(kh, wq_ref[h], preferred_element_type=F32) + sc_ref[pl.ds(off, CK), :]
        return eq_seen

    def softmax_update(c, lg_ref, lg_next_ref, carry):
        eq_seen, m_all = carry
        m_new = jnp.maximum(m_all, jnp.concatenate(
            [jnp.max(lg_ref[h], axis=0, keepdims=True) for h in range(N_HEADS)], axis=0))
        alpha = jnp.exp2(m_all - m_new)
        eq_seen = masked_logits(c + 1, eq_seen, lg_next_ref)
        off = chunk_offset(c)
        ones_rows = jnp.ones((ACC_ROWS - HEAD_DIM, CK), BF16)
        for h in range(N_HEADS):
            p = jnp.exp2(lg_ref[h] - m_new[h:h + 1, :])
            lhs = jnp.concatenate([vT_ref[0, h * HEAD_DIM:(h + 1) * HEAD_DIM, pl.ds(off, CK)], ones_rows],
                                  axis=0)
            acc_ref[h] = alpha[h:h + 1, :] * acc_ref[h] + jnp.dot(lhs, p.astype(BF16),
                                                                  preferred_element_type=F32)
        return eq_seen, m_new

    def attend_pair(i, carry):
        carry = softmax_update(2 * i, lga_ref, lgb_ref, carry)
        return softmax_update(2 * i + 1, lgb_ref, lga_ref, carry)

    eq_seen0 = masked_logits(0, jnp.zeros((1, Q_BLOCK), F32), lga_ref)
    lax.fori_loop(0, n_pairs, attend_pair, (eq_seen0, jnp.full((N_HEADS, Q_BLOCK), NEG_BIG, F32)))
    o_ref[0] = jnp.concatenate(
        [acc_ref[h][0:HEAD_DIM, :] / acc_ref[h][HEAD_DIM:HEAD_DIM + 1, :] for h in range(N_HEADS)],
        axis=0).T.astype(BF16)


def _attn_prompt(qT, qiT, wT, kib, kb, vTb, *, topk):
    b, _, s = qT.shape
    colT = lambda n: pl.BlockSpec((1, n, Q_BLOCK), lambda i, j: (i, 0, j))
    return pl.pallas_call(
        functools.partial(_attn_prompt_kernel, topk=topk),
        grid=(b, s // Q_BLOCK),
        in_specs=[colT(E_A), colT(E_A), colT(IDX_HEADS),
                  pl.BlockSpec((1, s, IDX_DIM), lambda i, j: (i, 0, 0)),
                  pl.BlockSpec((1, s, E_A), lambda i, j: (i, 0, 0)),
                  pl.BlockSpec((1, E_A, s), lambda i, j: (i, 0, 0))],
        out_specs=pl.BlockSpec((1, Q_BLOCK, E_A), lambda i, j: (i, j, 0)),
        out_shape=jax.ShapeDtypeStruct((b, s, E_A), BF16),
        scratch_shapes=[pltpu.VMEM((s, Q_BLOCK), F32), pltpu.VMEM((s, Q_BLOCK), BF16),
                        pltpu.VMEM((N_HEADS, ACC_ROWS, Q_BLOCK), F32),
                        pltpu.VMEM((N_HEADS, CK, Q_BLOCK), F32), pltpu.VMEM((N_HEADS, CK, Q_BLOCK), F32),
                        pltpu.VMEM((N_HEADS, LANE, Q_BLOCK), BF16), pltpu.VMEM((CK, CK), BF16)],
        compiler_params=pltpu.CompilerParams(dimension_semantics=("arbitrary", "arbitrary"),
                                             vmem_limit_bytes=VMEM_LIMIT),
        name="attn_prompt",
    )(qT, qiT, wT, kib, kb, vTb)


def _sample_scores_kernel(pt_ref, qi_ref, w_ref, kinew_ref, *refs, nch, t_new):
    pages = refs[:PAGES_PER_STEP]
    out_ref = refs[PAGES_PER_STEP]
    c = pl.program_id(1)
    qi = qi_ref[0]
    w = w_ref[0][:, 0:1]

    def scores(keysT):
        t = jnp.maximum(jnp.dot(qi, keysT.astype(BF16), preferred_element_type=F32), 0.0) * w
        acc = t[0:SROWS]
        for h in range(1, IDX_HEADS):
            acc = acc + t[h * SROWS:(h + 1) * SROWS]
        return acc

    @pl.when(c < nch)
    def _():
        out_ref[0] = scores(jnp.concatenate([p[0, 0] for p in pages], axis=1))

    @pl.when(c == nch)
    def _():
        sc = scores(kinew_ref[0])
        qrow = lax.broadcasted_iota(jnp.int32, sc.shape, 0)
        kcol = lax.broadcasted_iota(jnp.int32, sc.shape, 1)
        ok = kcol <= jnp.minimum(qrow, t_new - 1)
        out_ref[0] = jnp.concatenate(
            [jnp.where(ok, sc, -jnp.inf),
             jnp.full((SROWS, out_ref.shape[2] - sc.shape[1]), -jnp.inf, F32)], axis=1)


def _sample_scores(page_table, qi_h, w_h, kiT_new, cache_iT, layer, *, t_new):
    db, n_pages = page_table.shape
    nch = n_pages // PAGES_PER_STEP
    page = cache_iT.shape[3]
    sck = PAGES_PER_STEP * page

    def page_spec(i):
        return pl.BlockSpec(
            (1, 1, IDX_DIM, page),
            lambda b, c, pt: (layer, pt[b, jnp.minimum(c, nch - 1) * PAGES_PER_STEP + i], 0, 0))

    per_b = lambda *shape: pl.BlockSpec((1,) + shape, lambda b, c, pt: (b,) + (0,) * len(shape))
    grid_spec = pltpu.PrefetchScalarGridSpec(
        num_scalar_prefetch=1, grid=(db, nch + 1),
        in_specs=[per_b(IDX_HEADS * SROWS, IDX_DIM), per_b(IDX_HEADS * SROWS, LANE), per_b(IDX_DIM, page)]
                 + [page_spec(i) for i in range(PAGES_PER_STEP)],
        out_specs=pl.BlockSpec((1, SROWS, sck), lambda b, c, pt: (b, 0, c)))
    return pl.pallas_call(
        functools.partial(_sample_scores_kernel, nch=nch, t_new=t_new),
        grid_spec=grid_spec,
        out_shape=jax.ShapeDtypeStruct((db, SROWS, (nch + 1) * sck), F32),
        compiler_params=pltpu.CompilerParams(dimension_semantics=("arbitrary", "arbitrary"),
                                             vmem_limit_bytes=VMEM_LIMIT),
        name="sample_scores",
    )(page_table, qi_h, w_h, kiT_new, *([cache_iT] * PAGES_PER_STEP))


def _sample_threshold_kernel(sc_ref, thr_ref, take_ref, tb_ref, *, topk):
    keys, lanes = sc_ref.shape

    def truncate_chunk(c, carry):
        off = pl.multiple_of(c * CK, CK)
        tb_ref[pl.ds(off, CK), :] = _truncate_bf16(sc_ref[pl.ds(off, CK), :])
        return carry

    lax.fori_loop(0, keys // CK, truncate_chunk, 0)
    thr, n_eq_take = _topk_threshold(sc_ref, tb_ref, keys // CK, jnp.full((1, lanes), topk, jnp.int32))
    thr_ref[...] = jnp.broadcast_to(thr, thr_ref.shape)
    take_ref[...] = jnp.broadcast_to(n_eq_take, take_ref.shape)


def _sample_threshold(scoresT, *, topk):
    keys, nq = scoresT.shape
    return pl.pallas_call(
        functools.partial(_sample_threshold_kernel, topk=topk),
        grid=(nq // LANE,),
        in_specs=[pl.BlockSpec((keys, LANE), lambda i: (0, i))],
        out_specs=[pl.BlockSpec((SUBLANE, LANE), lambda i: (0, i))] * 2,
        out_shape=[jax.ShapeDtypeStruct((SUBLANE, nq), F32)] * 2,
        scratch_shapes=[pltpu.VMEM((keys, LANE), BF16)],
        compiler_params=pltpu.CompilerParams(dimension_semantics=("arbitrary",),
                                             vmem_limit_bytes=VMEM_LIMIT),
        name="sample_threshold",
    )(scoresT)


def _sample_attn_kernel(pt_ref, q_ref, sc_ref, thr_ref, take_ref, kTnew_ref, vTnew_ref, *refs, nch):
    kpages = refs[:PAGES_PER_STEP]
    vpages = refs[PAGES_PER_STEP:2 * PAGES_PER_STEP]
    o_ref, m_ref, l_ref, acc_ref, seen_ref = refs[2 * PAGES_PER_STEP:]
    c = pl.program_id(1)

    @pl.when(c == 0)
    def _():
        m_ref[...] = jnp.full_like(m_ref, NEG_BIG)
        l_ref[...] = jnp.zeros_like(l_ref)
        acc_ref[...] = jnp.zeros_like(acc_ref)
        seen_ref[...] = jnp.zeros_like(seen_ref)

    thr = thr_ref[0][:, 0:1]
    take = take_ref[0][:, 0:1]

    def attend(kT, vT):
        width = kT.shape[1]
        blk = min(CK, width)
        triu = _tri(blk, lower=False)
        seen = seen_ref[:, 0:1]
        sels = []
        for i in range(width // blk):
            sel, seen = _select_rows(sc_ref[0][:, i * blk:(i + 1) * blk], thr, take, seen, triu)
            sels.append(sel)
        seen_ref[...] = jnp.broadcast_to(seen, seen_ref.shape)
        sel = jnp.tile(jnp.concatenate(sels, axis=1) if len(sels) > 1 else sels[0], (N_HEADS, 1))
        lg = jnp.where(sel, jnp.dot(q_ref[0], kT.astype(BF16), preferred_element_type=F32), -jnp.inf)
        m_old = m_ref[:, 0:1]
        m_new = jnp.maximum(m_old, jnp.max(lg, axis=1, keepdims=True))
        p = jnp.exp(lg - m_new)
        alpha = jnp.exp(m_old - m_new)
        l_new = alpha * l_ref[:, 0:1] + jnp.sum(p, axis=1, keepdims=True)
        pv = lax.dot_general(p.astype(BF16), vT.astype(BF16), (((1,), (1,)), ((), ())),
                             preferred_element_type=F32)
        acc_ref[...] = alpha * acc_ref[...] + pv
        m_ref[...] = jnp.broadcast_to(m_new, m_ref.shape)
        l_ref[...] = jnp.broadcast_to(l_new, l_ref.shape)

    def stack(pages):
        return jnp.concatenate([p[0, 0].reshape(E_A, p.shape[4]) for p in pages], axis=1)

    @pl.when(c < nch)
    def _():
        attend(stack(kpages), stack(vpages))

    @pl.when(c == nch)
    def _():
        attend(kTnew_ref[0], vTnew_ref[0])
        o = acc_ref[...] / l_ref[:, 0:1]
        o_ref[0] = jnp.concatenate(
            [o[h * SROWS:(h + 1) * SROWS, h * HEAD_DIM:(h + 1) * HEAD_DIM] for h in range(N_HEADS)], axis=1)


def _sample_attn(page_table, q_bd, scores, thr, take, kT_new, vT_new, cache_kT, cache_vT, layer):
    db, n_pages = page_table.shape
    nch = n_pages // PAGES_PER_STEP
    page = cache_kT.shape[4]
    sck = PAGES_PER_STEP * page
    hq = N_HEADS * SROWS

    def page_spec(i):
        return pl.BlockSpec(
            (1, 1, N_HEADS, HEAD_DIM, page),
            lambda b, c, pt: (layer, pt[b, jnp.minimum(c, nch - 1) * PAGES_PER_STEP + i], 0, 0, 0))

    per_b = lambda *shape: pl.BlockSpec((1,) + shape, lambda b, c, pt: (b,) + (0,) * len(shape))
    grid_spec = pltpu.PrefetchScalarGridSpec(
        num_scalar_prefetch=1, grid=(db, nch + 1),
        in_specs=[per_b(hq, E_A),
                  pl.BlockSpec((1, SROWS, sck), lambda b, c, pt: (b, 0, c)),
                  per_b(SROWS, LANE), per_b(SROWS, LANE),
                  per_b(E_A, page), per_b(E_A, page)]
                 + [page_spec(i) for i in range(PAGES_PER_STEP)] * 2,
        out_specs=per_b(SROWS, E_A),
        scratch_shapes=[pltpu.VMEM((hq, LANE), F32), pltpu.VMEM((hq, LANE), F32),
                        pltpu.VMEM((hq, E_A), F32), pltpu.VMEM((SROWS, LANE), F32)])
    return pl.pallas_call(
        functools.partial(_sample_attn_kernel, nch=nch),
        grid_spec=grid_spec,
        out_shape=jax.ShapeDtypeStruct((db, SROWS, E_A), F32),
        compiler_params=pltpu.CompilerParams(dimension_semantics=("arbitrary", "arbitrary"),
                                             vmem_limit_bytes=VMEM_LIMIT),
        name="sample_attn",
    )(page_table, q_bd, scores, thr, take, kT_new, vT_new,
      *([cache_kT] * PAGES_PER_STEP), *([cache_vT] * PAGES_PER_STEP))


def _post_kernel(x_ref, oa_ref, g0_ref, mbc_ref, wba_ref, wo_ref, wup_ref, wdn_ref,
                 gpost_ref, gfpre_ref, gfpost_ref, y_ref):
    m = g0_ref[...].astype(F32) * jnp.dot(oa_ref[...], wba_ref[...], preferred_element_type=F32) \
        + mbc_ref[...]
    y = jnp.dot(m.astype(BF16), wo_ref[...], preferred_element_type=F32)
    x1 = x_ref[...] + _rms(y, gpost_ref[...])
    a = jnp.maximum(jnp.dot(_rms(x1, gfpre_ref[...]).astype(BF16), wup_ref[...],
                            preferred_element_type=F32), 0.0)
    f = jnp.dot((a * a).astype(BF16), wdn_ref[...], preferred_element_type=F32)
    y_ref[...] = x1 + _rms(f, gfpost_ref[...])


def _post(x, oa, g0, mbc, wba, wo, wup, wdn, gpost, gfpre, gfpost):
    m = x.shape[0]
    tm = min(POST_ROW_TILE, m)
    row = lambda n: pl.BlockSpec((tm, n), lambda i: (i, 0))
    return pl.pallas_call(
        _post_kernel, grid=(m // tm,),
        in_specs=[row(D_MODEL), row(E_A), row(D_MODEL), row(D_MODEL),
                  _const_spec((E_A, D_MODEL)), _const_spec((D_MODEL, D_MODEL)),
                  _const_spec((D_MODEL, D_FF)), _const_spec((D_FF, D_MODEL)),
                  _const_spec((1, D_MODEL)), _const_spec((1, D_MODEL)), _const_spec((1, D_MODEL))],
        out_specs=row(D_MODEL),
        out_shape=jax.ShapeDtypeStruct((m, D_MODEL), F32),
        compiler_params=pltpu.CompilerParams(dimension_semantics=("arbitrary",),
                                             vmem_limit_bytes=VMEM_LIMIT),
        name="post",
    )(x, oa, g0, mbc, wba, wo, wup, wdn, gpost, gfpre, gfpost)


def _rope_tables(pos):
    half = HEAD_DIM // 2
    inv = ROPE_THETA ** (-jnp.arange(half, dtype=F32) * (2.0 / HEAD_DIM))
    ang = pos[:, None] * inv[None, :]
    c, s = jnp.cos(ang), jnp.sin(ang)
    return jnp.tile(c, (1, LANE // half)), jnp.tile(jnp.concatenate([-s, s], axis=1), (1, LANE // HEAD_DIM))


def _heads_first(a, t):
    db = a.shape[0] // t
    a = a.reshape(db, t, N_HEADS, -1).transpose(0, 2, 1, 3)
    a = jnp.pad(a, ((0, 0), (0, 0), (0, SROWS - t), (0, 0)))
    return a.reshape(db, N_HEADS * SROWS, -1)


def _new_keys_T(a, db, t, page):
    a = jnp.pad(a.reshape(db, t, -1), ((0, 0), (0, page - t), (0, 0)))
    return a.transpose(0, 2, 1)


def kernel(x_prompt, x_sample, cache_k, cache_v, cache_idx_k, state_conv, page_table, norm_mix_pre, norm_mix_post, norm_ffn_pre, norm_ffn_post, w_in, gmlp_ln_g, gmlp_ln_b, gmlp_ws, gmlp_bs, conv_w, conv_b, w_br_attn, w_br_gmlp, w_br_conv, w_out, w_ff_up, w_ff_down):
    depth = w_in.shape[0]
    b, s, _ = x_prompt.shape
    db, t, _ = x_sample.shape
    page = cache_k.shape[2]
    n_pages = page_table.shape[1]
    past = n_pages * page
    ms = db * t
    assert s % POST_ROW_TILE == 0 and s % (2 * CK) == 0 and ms == CHUNK and CONV_W - 1 <= t <= SROWS
    assert page == LANE and n_pages % PAGES_PER_STEP == 0

    cache_kT = cache_k.transpose(0, 1, 3, 4, 2)
    cache_vT = cache_v.transpose(0, 1, 3, 4, 2)
    cache_iT = cache_idx_k.transpose(0, 1, 3, 2)

    w_in_p = jnp.concatenate(
        [w_in[:, :, :OFF_B], jnp.zeros((depth, D_MODEL, C_B - OFF_B), w_in.dtype), w_in[:, :, OFF_B:]],
        axis=2).astype(BF16)
    causal = jnp.tril(jnp.ones((CHUNK, CHUNK), bool))
    wm = jnp.where(causal[None, None], gmlp_ws, 0)
    wmix_p = wm.astype(BF16)
    bmix_p = jnp.repeat(jnp.swapaxes(gmlp_bs, 1, 2), E_B // GMLP_GROUPS, axis=2)
    eye = jnp.eye(db, dtype=wm.dtype)
    wmix_s = jnp.einsum('ab,lgts->lgatbs', eye, wm[:, :, :t, :t]).reshape(depth, GMLP_GROUPS, ms, ms)
    wmix_s = wmix_s.astype(BF16)
    bmix_s = jnp.tile(bmix_p[:, :t], (1, db, 1))
    wbb, wbc, wba = (w.astype(BF16) for w in (w_br_gmlp, w_br_conv, w_br_attn))
    wo, wup, wdn = (w.astype(BF16) for w in (w_out, w_ff_up, w_ff_down))
    r2 = lambda a, l: a[l][None, :]

    cos_p, sin_p = _rope_tables(jnp.arange(s, dtype=F32))
    cos_s, sin_s = _rope_tables(jnp.tile(jnp.arange(t, dtype=F32) + past, db))
    topk_p = min(TOPK_MAX, s // 4)
    topk_s = min(TOPK_MAX, (past + t) // 4)
    head_eye = jnp.eye(N_HEADS, dtype=BF16)

    xp = x_prompt.reshape(b * s, D_MODEL)
    xs = x_sample.reshape(ms, D_MODEL)
    outs = [[] for _ in range(9)]
    for l in range(depth):
        common = (r2(norm_mix_pre, l), w_in_p[l])
        tail = (r2(gmlp_ln_g, l), r2(gmlp_ln_b, l))
        conv = (conv_w[l], r2(conv_b, l), wbb[l], wbc[l])
        post_w = (wba[l], wo[l], wup[l], wdn[l], r2(norm_mix_post, l), r2(norm_ffn_pre, l),
                  r2(norm_ffn_post, l))
        (qT, qiT, wT, kb, kib, vTb, kT, vT, kiT, g0, mbc, ctail) = _proj(
            xp, *common, cos_p, sin_p, *tail, wmix_p[l], bmix_p[l], *conv, sample=False, seq_len=s)
        oa = _attn_prompt(qT, qiT, wT, kib.reshape(b, s, IDX_DIM), kb.reshape(b, s, E_A), vTb, topk=topk_p)
        xp = _post(xp, oa.reshape(b * s, E_A), g0, mbc, *post_w)
        outs[0].append(kT.reshape(b, N_HEADS, HEAD_DIM, s))
        outs[1].append(vT.reshape(b, N_HEADS, HEAD_DIM, s))
        outs[2].append(kiT)
        outs[3].append(ctail[:, SUBLANE - (CONV_W - 1):, :])
        st = state_conv[l]
        z = jnp.zeros((db, 1, E_C), st.dtype)
        st1 = jnp.concatenate([st[:, 1:2]] + [z] * (t - 1), axis=1)
        st2 = jnp.concatenate([st[:, 0:1], st[:, 1:2]] + [z] * (t - 2), axis=1)
        st12 = jnp.concatenate([st1, st2], axis=2).reshape(ms, 2 * E_C)
        (q_s, k_s, v_s, qi_s, kw_s, g0_s, mbc_s, cin_s, vn_s) = _proj(
            xs, *common, cos_s, sin_s, *tail, wmix_s[l], bmix_s[l], *conv, sample=True, seq_len=t,
            st=st12)
        qi_h = _heads_first(qi_s, t)
        w_h = _heads_first(kw_s[:, IDX_DIM:IDX_DIM + IDX_HEADS][:, :, None], t)
        w_h = jnp.broadcast_to(w_h, w_h.shape[:2] + (LANE,))
        scores = _sample_scores(page_table, qi_h, w_h, _new_keys_T(kw_s[:, :IDX_DIM], db, t, page),
                                cache_iT, l, t_new=t)
        thr, take = _sample_threshold(scores.reshape(db * SROWS, -1).T, topk=topk_s)
        per_q = lambda a: jnp.broadcast_to(a[0].reshape(db, SROWS, 1), (db, SROWS, LANE))
        q4 = jnp.pad(q_s.reshape(db, t, N_HEADS, HEAD_DIM), ((0, 0), (0, SROWS - t), (0, 0), (0, 0)))
        q_bd = jnp.einsum('bqhd,hg->bhqgd', q4, head_eye).reshape(db, N_HEADS * SROWS, E_A)
        oa_s = _sample_attn(page_table, q_bd, scores, per_q(thr), per_q(take),
                            _new_keys_T(k_s, db, t, page), _new_keys_T(v_s, db, t, page),
                            cache_kT, cache_vT, l)
        oa_s = oa_s[:, :t].reshape(ms, E_A).astype(BF16)
        xs = _post(xs, oa_s, g0_s, mbc_s, *post_w)
        outs[4].append(k_s.reshape(db, t, N_HEADS, HEAD_DIM))
        outs[5].append(v_s.reshape(db, t, N_HEADS, HEAD_DIM))
        outs[6].append(kw_s[:, :IDX_DIM].reshape(db, t, IDX_DIM))
        outs[7].append(cin_s.reshape(db, t, E_C)[:, t - (CONV_W - 1):])
        outs[8].append(vn_s.reshape(db, t, E_B))
    stacked = [jnp.stack(o) for o in outs]
    stacked[0] = stacked[0].transpose(0, 1, 4, 2, 3)
    stacked[1] = stacked[1].transpose(0, 1, 4, 2, 3)
    stacked[2] = stacked[2].transpose(0, 1, 3, 2)
    return (xp.reshape(b, s, D_MODEL), xs.reshape(db, t, D_MODEL)) + tuple(stacked)
```

```python
import functools
import math

import jax
import jax.numpy as jnp
from jax import lax
from jax.experimental import pallas as pl
from jax.experimental.pallas import tpu as pltpu

F32 = jnp.float32
BF16 = jnp.bfloat16

D_MODEL = 1024
N_HEADS = 8
HEAD_DIM = 64
E_A = N_HEADS * HEAD_DIM
IDX_HEADS = 8
IDX_DIM = 64
IDX_W_SCALE = 1.0 / math.sqrt(IDX_HEADS * IDX_DIM)
TOPK_MAX = 256
Q_BLOCK = 256
ROPE_THETA = 10000.0
CHUNK = 128
GMLP_GROUPS = 4
E_B = 512
E_C = 512
CONV_W = 3
D_FF = 4 * D_MODEL
EPS = 1e-6

OFF_KI = 3 * E_A + IDX_HEADS * IDX_DIM
OFF_B = OFF_KI + IDX_DIM + IDX_HEADS
LANE = 128
SUBLANE = 8
C_KW = OFF_KI
C_B = C_KW + LANE
C_C = C_B + 2 * E_B
C_G = C_C + 3 * E_C
N_PAD = C_G + 3 * D_MODEL

VMEM_LIMIT = 56 * 1024 * 1024
ROW_TILE = 256
POST_ROW_TILE = 512
NEG_BIG = -1e30
KEY_NEG_INF = -2139095041
KEY_POS_INF = 2139095040
CK = 256
COUNT_ELEMS = 2 * CK * LANE
ACC_ROWS = HEAD_DIM + 16
LOG2_E = math.log2(math.e)
PAGES_PER_STEP = 16
SROWS = SUBLANE


def _rms(x, g):
    return x * lax.rsqrt(jnp.mean(x * x, axis=-1, keepdims=True) + EPS) * g


def _rope128(z, cos, sin):
    lane = lax.broadcasted_iota(jnp.int32, z.shape, 1)
    partner = jnp.where((lane % HEAD_DIM) < HEAD_DIM // 2,
                        pltpu.roll(z, LANE - HEAD_DIM // 2, 1),
                        pltpu.roll(z, HEAD_DIM // 2, 1))
    return z * cos + partner * sin


def _rope(z, cos, sin):
    return jnp.concatenate(
        [_rope128(z[:, i:i + LANE], cos, sin) for i in range(0, z.shape[1], LANE)], axis=1)


def _proj_kernel(*refs, sample, tm, seq_rows):
    if sample:
        (x_ref, gpre_ref, w_ref, cos_ref, sin_ref, lng_ref, lnb_ref, wmix_ref, bmix_ref,
         cw_ref, cbias_ref, wbb_ref, wbc_ref, st_ref,
         q_ref, k_ref, v_ref, qi_ref, kw_ref, g0_ref, mbc_ref, cin_ref, vn_ref) = refs
    else:
        (x_ref, gpre_ref, w_ref, cos_ref, sin_ref, lng_ref, lnb_ref, wmix_ref, bmix_ref,
         cw_ref, cbias_ref, wbb_ref, wbc_ref,
         qT_ref, qiT_ref, wT_ref, kb_ref, kib_ref, vTb_ref, kT_ref, vT_ref, kiT_ref,
         g0_ref, mbc_ref, ctail_ref, carry_ref) = refs

    h = _rms(x_ref[...], gpre_ref[...]).astype(BF16)
    cos = cos_ref[...]
    sin = sin_ref[...]

    z = jnp.dot(h, w_ref[:, 0:C_KW], preferred_element_type=F32)
    q = _rope(z[:, 0:E_A], cos, sin) * (HEAD_DIM ** -0.5 * (1.0 if sample else LOG2_E))
    k = _rope(z[:, E_A:2 * E_A], cos, sin)
    v = z[:, 2 * E_A:3 * E_A]
    qi = _rope(z[:, 3 * E_A:4 * E_A], cos, sin)

    zk = jnp.dot(h, w_ref[:, C_KW:C_B], preferred_element_type=F32)
    lane = lax.broadcasted_iota(jnp.int32, zk.shape, 1)
    kw = jnp.where(lane < IDX_DIM, _rope128(zk, cos, sin), zk * IDX_W_SCALE)
    if sample:
        q_ref[...] = q.astype(BF16)
        k_ref[...] = k
        v_ref[...] = v
        qi_ref[...] = qi.astype(BF16)
        kw_ref[...] = kw
    else:
        qT_ref[0] = q.T.astype(BF16)
        qiT_ref[0] = qi.T.astype(BF16)
        kT = k.T
        kT_ref[0] = kT
        kb_ref[...] = k.astype(BF16)
        vT = v.T
        vT_ref[0] = vT
        vTb_ref[0] = vT.astype(BF16)
        kwT = kw.T
        kiT_ref[0] = kwT[0:IDX_DIM, :]
        wT_ref[0] = kwT[IDX_DIM:IDX_DIM + IDX_HEADS, :]
        kib_ref[...] = kw[:, 0:IDX_DIM].astype(BF16)

    gl = jax.nn.gelu(jnp.dot(h, w_ref[:, C_B:C_C], preferred_element_type=F32))
    u = gl[:, 0:E_B]
    vg = gl[:, E_B:2 * E_B]
    mu = jnp.mean(vg, axis=-1, keepdims=True)
    vc = vg - mu
    vn = vc * lax.rsqrt(jnp.mean(vc * vc, axis=-1, keepdims=True) + EPS) * lng_ref[...] + lnb_ref[...]
    if sample:
        vn_ref[...] = vn
    vnb = vn.astype(BF16)
    gw = E_B // GMLP_GROUPS
    rows = []
    for c in range(tm // CHUNK):
        cols = []
        for g in range(GMLP_GROUPS):
            cols.append(jnp.dot(wmix_ref[g], vnb[c * CHUNK:(c + 1) * CHUNK, g * gw:(g + 1) * gw],
                                preferred_element_type=F32))
        rows.append(jnp.concatenate(cols, axis=1) + bmix_ref[...])
    mix = rows[0] if len(rows) == 1 else jnp.concatenate(rows, axis=0)
    ob = (u * mix).astype(BF16)

    zc = jnp.dot(h, w_ref[:, C_C:C_G], preferred_element_type=F32)
    cb = zc[:, 0:E_C]
    cin = zc[:, E_C:2 * E_C] * zc[:, 2 * E_C:3 * E_C]
    row = lax.broadcasted_iota(jnp.int32, cin.shape, 0)
    if sample:
        cin_ref[...] = cin
        i_in_seq = row % seq_rows
        s1 = jnp.where(i_in_seq >= 1, pltpu.roll(cin, 1, 0), st_ref[:, 0:E_C])
        s2 = jnp.where(i_in_seq >= 2, pltpu.roll(cin, 2, 0), st_ref[:, E_C:2 * E_C])
    else:
        @pl.when(pl.program_id(0) % (seq_rows // tm) == 0)
        def _():
            carry_ref[...] = jnp.zeros_like(carry_ref)
        p2 = carry_ref[0:1, :]
        p1 = carry_ref[1:2, :]
        s1 = jnp.where(row >= 1, pltpu.roll(cin, 1, 0), p1)
        s2 = jnp.where(row >= 2, pltpu.roll(cin, 2, 0), jnp.where(row == 0, p2, p1))
        carry_ref[0:2, :] = cin[tm - 2:tm, :]
        ctail_ref[0] = cin[tm - SUBLANE:tm, :]
    y = s2 * cw_ref[0:1, :] + s1 * cw_ref[1:2, :] + cin * cw_ref[2:3, :] + cbias_ref[...]
    oc = (cb * y).astype(BF16)

    g0_ref[...] = jax.nn.sigmoid(
        jnp.dot(h, w_ref[:, C_G:C_G + D_MODEL], preferred_element_type=F32)).astype(BF16)
    g1 = jax.nn.sigmoid(jnp.dot(h, w_ref[:, C_G + D_MODEL:C_G + 2 * D_MODEL], preferred_element_type=F32))
    mbc = g1 * jnp.dot(ob, wbb_ref[...], preferred_element_type=F32)
    g2 = jax.nn.sigmoid(jnp.dot(h, w_ref[:, C_G + 2 * D_MODEL:N_PAD], preferred_element_type=F32))
    mbc_ref[...] = mbc + g2 * jnp.dot(oc, wbc_ref[...], preferred_element_type=F32)


def _layer_spec(layer, shape):
    nd = len(shape)
    return pl.BlockSpec((None,) + shape, lambda *_: (layer,) + (0,) * nd, pipeline_mode=pl.Buffered(1))


def _proj(x, gpre, w, cos, sin, lng, lnb, wmix, bmix, cw, cbias, wbb, wbc, *, layer, sample, seq_len,
          st=None):
    m = x.shape[0]
    _const_spec = functools.partial(_layer_spec, layer)
    tm = m if sample else ROW_TILE
    tps = max(seq_len // tm, 1)
    n_seq = m // seq_len
    row = lambda n: pl.BlockSpec((tm, n), lambda i: (i, 0))
    pos_spec = pl.BlockSpec((tm, LANE), (lambda i: (0, 0)) if sample else (lambda i: (i % tps, 0)))
    in_specs = [row(D_MODEL), _const_spec((1, D_MODEL)), _const_spec((D_MODEL, N_PAD)),
                pos_spec, pos_spec, _const_spec((1, E_B)), _const_spec((1, E_B)),
                _const_spec((GMLP_GROUPS, CHUNK, CHUNK)), _const_spec((CHUNK, E_B)),
                _const_spec((CONV_W, E_C)), _const_spec((1, E_C)),
                _const_spec((E_B, D_MODEL)), _const_spec((E_C, D_MODEL))]
    args = [x, gpre, w, cos, sin, lng, lnb, wmix, bmix, cw, cbias, wbb, wbc]
    sds = jax.ShapeDtypeStruct
    if sample:
        in_specs.append(pl.BlockSpec((None, tm, 2 * E_C), lambda i: (layer, i, 0)))
        args.append(st)
        out_shape = [sds((m, E_A), BF16), sds((m, E_A), F32), sds((m, E_A), F32), sds((m, E_A), BF16),
                     sds((m, LANE), F32), sds((m, D_MODEL), BF16), sds((m, D_MODEL), F32),
                     sds((m, E_C), F32), sds((m, E_B), F32)]
        out_specs = [row(E_A), row(E_A), row(E_A), row(E_A), row(LANE), row(D_MODEL), row(D_MODEL),
                     row(E_C), row(E_B)]
        scratch = []
    else:
        colT = lambda n: pl.BlockSpec((1, n, tm), lambda i: (i // tps, 0, i % tps))
        out_shape = [sds((n_seq, E_A, seq_len), BF16), sds((n_seq, E_A, seq_len), BF16),
                     sds((n_seq, IDX_HEADS, seq_len), F32), sds((m, E_A), BF16), sds((m, IDX_DIM), BF16),
                     sds((n_seq, E_A, seq_len), BF16), sds((n_seq, E_A, seq_len), F32),
                     sds((n_seq, E_A, seq_len), F32), sds((n_seq, IDX_DIM, seq_len), F32),
                     sds((m, D_MODEL), BF16), sds((m, D_MODEL), F32), sds((n_seq, SUBLANE, E_C), F32)]
        out_specs = [colT(E_A), colT(E_A), colT(IDX_HEADS), row(E_A), row(IDX_DIM),
                     colT(E_A), colT(E_A), colT(E_A), colT(IDX_DIM),
                     row(D_MODEL), row(D_MODEL),
                     pl.BlockSpec((1, SUBLANE, E_C), lambda i: (i // tps, 0, 0))]
        scratch = [pltpu.VMEM((SUBLANE, E_C), F32)]
    return pl.pallas_call(
        functools.partial(_proj_kernel, sample=sample, tm=tm, seq_rows=seq_len),
        grid=(m // tm,), in_specs=in_specs, out_specs=out_specs, out_shape=out_shape,
        scratch_shapes=scratch,
        compiler_params=pltpu.CompilerParams(dimension_semantics=("arbitrary",),
                                             vmem_limit_bytes=VMEM_LIMIT),
        name="proj_sample" if sample else "proj_prompt",
    )(*args)


def _key_to_f32(key):
    bits = key ^ ((key >> 31) & jnp.int32(0x7FFFFFFF))
    return lax.bitcast_convert_type(bits, F32)


def _truncate_bf16(x):
    bits = lax.bitcast_convert_type(x, jnp.int32) & jnp.int32(-65536)
    return lax.bitcast_convert_type(bits, F32).astype(BF16)


def _topk_threshold(sc_ref, tb_ref, n_chunks, k_row):
    lanes = sc_ref.shape[1]
    k_f = k_row.astype(F32)
    count_rows = COUNT_ELEMS // lanes
    n_count = n_chunks * CK // count_rows
    acc_rows = count_rows // 8

    def count(ref, pred_fn, dtype):
        one, zero = jnp.ones((), dtype), jnp.zeros((), dtype)

        def body(c, acc):
            off = pl.multiple_of(c * count_rows, count_rows)
            hit = jnp.where(pred_fn(ref[pl.ds(off, count_rows), :]), one, zero)
            p = [hit[i * acc_rows:(i + 1) * acc_rows] for i in range(8)]
            return acc + (((p[0] + p[1]) + (p[2] + p[3])) + ((p[4] + p[5]) + (p[6] + p[7])))
        acc = lax.fori_loop(0, n_count, body, jnp.zeros((acc_rows, lanes), dtype))
        return jnp.sum(acc.astype(F32), axis=0, keepdims=True)

    def bisect_high(_, lohi):
        lo, hi = lohi
        mid = (lo + hi) >> 1
        rep = (mid << 16) | jnp.where(mid < 0, jnp.int32(0xFFFF), jnp.int32(0))
        midb = _key_to_f32(rep).astype(BF16)
        ge = count(tb_ref, lambda t: t >= midb, BF16) >= k_f
        return jnp.where(ge, mid, lo), jnp.where(ge, hi, mid)

    def bisect_low(_, lohi):
        lo, hi = lohi
        mid = (lo >> 1) + (hi >> 1) + (lo & hi & 1)
        midf = _key_to_f32(mid)
        ge = count(sc_ref, lambda s: s >= midf, F32) >= k_f
        return jnp.where(ge, mid, lo), jnp.where(ge, hi, mid)

    lo0 = jnp.full((1, lanes), KEY_NEG_INF >> 16, jnp.int32)
    hi0 = jnp.full((1, lanes), (KEY_POS_INF >> 16) + 1, jnp.int32)
    hi_half, _ = lax.fori_loop(0, 16, bisect_high, (lo0, hi0))
    lo, _ = lax.fori_loop(0, 16, bisect_low, (hi_half << 16, (hi_half << 16) + 65536))
    thr = _key_to_f32(lo)
    n_eq_take = k_f - count(sc_ref, lambda s: s > thr, F32)
    return thr, n_eq_take


def _select_cols(s, thr, n_eq_take, eq_seen, tril):
    eq = s == thr
    prefix = jnp.dot(tril, jnp.where(eq, 1.0, 0.0).astype(BF16), preferred_element_type=F32) + eq_seen
    sel = (s > thr) | (eq & (prefix <= n_eq_take))
    return sel, prefix[s.shape[0] - 1:s.shape[0], :]


def _select_rows(s, thr, n_eq_take, eq_seen, triu):
    blk = triu.shape[0]
    eq = s == thr
    eqf = jnp.where(eq, 1.0, 0.0)
    prefixes = []
    for i in range(s.shape[1] // blk):
        e = eqf[:, i * blk:(i + 1) * blk]
        prefixes.append(jnp.dot(e.astype(BF16), triu, preferred_element_type=F32) + eq_seen)
        eq_seen = eq_seen + jnp.sum(e, axis=1, keepdims=True)
    prefix = prefixes[0] if len(prefixes) == 1 else jnp.concatenate(prefixes, axis=1)
    return (s > thr) | (eq & (prefix <= n_eq_take)), eq_seen


def _tri(n, lower):
    r = lax.broadcasted_iota(jnp.int32, (n, n), 0)
    c = lax.broadcasted_iota(jnp.int32, (n, n), 1)
    return jnp.where((c <= r) if lower else (r <= c), 1.0, 0.0).astype(BF16)


def _attn_prompt_kernel(qT_ref, qiT_ref, wT_ref, ki_ref, k_ref, vT_ref, o_ref, sc_ref, tb_ref, acc_ref,
                        lga_ref, lgb_ref, wq_ref, tri_ref, *, topk):
    j = pl.program_id(1)
    nck = (j * Q_BLOCK) // CK + 1
    qiT = qiT_ref[0]
    w_idx = jnp.concatenate([qiT[h * IDX_DIM:(h + 1) * IDX_DIM, :] for h in range(IDX_HEADS)], axis=1)
    wT = wT_ref[0]
    q_pos = j * Q_BLOCK + lax.broadcasted_iota(jnp.int32, (1, Q_BLOCK), 1)

    def score_chunk(c, carry):
        off = pl.multiple_of(c * CK, CK)
        d = jnp.dot(ki_ref[0, pl.ds(off, CK), :], w_idx, preferred_element_type=F32)
        acc = jnp.maximum(d[:, 0:Q_BLOCK], 0.0) * wT[0:1, :]
        for h in range(1, IDX_HEADS):
            acc = acc + jnp.maximum(d[:, h * Q_BLOCK:(h + 1) * Q_BLOCK], 0.0) * wT[h:h + 1, :]
        key_pos = off + lax.broadcasted_iota(jnp.int32, (CK, 1), 0)
        acc = jnp.where(key_pos <= q_pos, acc, -jnp.inf)
        sc_ref[pl.ds(off, CK), :] = acc
        tb_ref[pl.ds(off, CK), :] = _truncate_bf16(acc)
        return carry

    lax.fori_loop(0, nck, score_chunk, 0)

    @pl.when(nck % 2 == 1)
    def _():
        off = pl.multiple_of(nck * CK, CK)
        sc_ref[pl.ds(off, CK), :] = jnp.full((CK, Q_BLOCK), -jnp.inf, F32)
        tb_ref[pl.ds(off, CK), :] = jnp.full((CK, Q_BLOCK), -jnp.inf, BF16)

    n_pairs = (nck + 1) // 2
    thr, n_eq_take = _topk_threshold(sc_ref, tb_ref, 2 * n_pairs, jnp.minimum(q_pos + 1, topk))

    qT = qT_ref[0].astype(F32)
    upper = lax.broadcasted_iota(jnp.int32, (LANE, Q_BLOCK), 0) < HEAD_DIM
    for h in range(N_HEADS):
        slab = qT[(h // 2) * LANE:(h // 2 + 1) * LANE, :]
        wq_ref[h] = jnp.where(upper if h % 2 == 0 else ~upper, slab, 0.0).astype(BF16)
    tri_ref[...] = _tri(CK, lower=True)
    acc_ref[...] = jnp.zeros_like(acc_ref)

    last_chunk = k_ref.shape[1] // CK - 1

    def chunk_offset(c):
        return pl.multiple_of(jnp.minimum(c, last_chunk) * CK, CK)

    def masked_logits(c, eq_seen, lg_ref):
        off = chunk_offset(c)
        sel, eq_seen = _select_cols(sc_ref[pl.ds(off, CK), :], thr, n_eq_take, eq_seen, tri_ref[...])
        sc_ref[pl.ds(off, CK), :] = jnp.where(sel, 0.0, -jnp.inf)
        for h in range(N_HEADS):
            kh = k_ref[0, pl.ds(off, CK), (h // 2) * LANE:(h // 2 + 1) * LANE]
            lg_ref[h] = jnp.dot(kh, wq_ref[h], preferred_element_type=F32) + sc_ref[pl.ds(off, CK), :]
        return eq_seen

    def softmax_update(c, lg_ref, lg_next_ref, carry):
        eq_seen, m_all = carry
        m_new = jnp.maximum(m_all, jnp.concatenate(
            [jnp.max(lg_ref[h], axis=0, keepdims=True) for h in range(N_HEADS)], axis=0))
        alpha = jnp.exp2(m_all - m_new)
        eq_seen = masked_logits(c + 1, eq_seen, lg_next_ref)
        off = chunk_offset(c)
        ones_rows = jnp.ones((ACC_ROWS - HEAD_DIM, CK), BF16)
        for h in range(N_HEADS):
            p = jnp.exp2(lg_ref[h] - m_new[h:h + 1, :])
            lhs = jnp.concatenate([vT_ref[0, h * HEAD_DIM:(h + 1) * HEAD_DIM, pl.ds(off, CK)], ones_rows],
                                  axis=0)
            acc_ref[h] = alpha[h:h + 1, :] * acc_ref[h] + jnp.dot(lhs, p.astype(BF16),
                                                                  preferred_element_type=F32)
        return eq_seen, m_new

    def attend_pair(i, carry):
        carry = softmax_update(2 * i, lga_ref, lgb_ref, carry)
        return softmax_update(2 * i + 1, lgb_ref, lga_ref, carry)

    eq_seen0 = masked_logits(0, jnp.zeros((1, Q_BLOCK), F32), lga_ref)
    lax.fori_loop(0, n_pairs, attend_pair, (eq_seen0, jnp.full((N_HEADS, Q_BLOCK), NEG_BIG, F32)))
    o_ref[0] = jnp.concatenate(
        [acc_ref[h][0:HEAD_DIM, :] / acc_ref[h][HEAD_DIM:HEAD_DIM + 1, :] for h in range(N_HEADS)],
        axis=0).T.astype(BF16)


def _attn_prompt(qT, qiT, wT, kib, kb, vTb, *, topk):
    b, _, s = qT.shape
    colT = lambda n: pl.BlockSpec((1, n, Q_BLOCK), lambda i, j: (i, 0, j))
    return pl.pallas_call(
        functools.partial(_attn_prompt_kernel, topk=topk),
        grid=(b, s // Q_BLOCK),
        in_specs=[colT(E_A), colT(E_A), colT(IDX_HEADS),
                  pl.BlockSpec((1, s, IDX_DIM), lambda i, j: (i, 0, 0)),
                  pl.BlockSpec((1, s, E_A), lambda i, j: (i, 0, 0)),
                  pl.BlockSpec((1, E_A, s), lambda i, j: (i, 0, 0))],
        out_specs=pl.BlockSpec((1, Q_BLOCK, E_A), lambda i, j: (i, j, 0)),
        out_shape=jax.ShapeDtypeStruct((b, s, E_A), BF16),
        scratch_shapes=[pltpu.VMEM((s, Q_BLOCK), F32), pltpu.VMEM((s, Q_BLOCK), BF16),
                        pltpu.VMEM((N_HEADS, ACC_ROWS, Q_BLOCK), F32),
                        pltpu.VMEM((N_HEADS, CK, Q_BLOCK), F32), pltpu.VMEM((N_HEADS, CK, Q_BLOCK), F32),
                        pltpu.VMEM((N_HEADS, LANE, Q_BLOCK), BF16), pltpu.VMEM((CK, CK), BF16)],
        compiler_params=pltpu.CompilerParams(dimension_semantics=("arbitrary", "arbitrary"),
                                             vmem_limit_bytes=VMEM_LIMIT),
        name="attn_prompt",
    )(qT, qiT, wT, kib, kb, vTb)


def _sample_scores_kernel(pt_ref, qi_ref, w_ref, kinew_ref, *refs, n_pages, t_new):
    pages = refs[:n_pages]
    out_ref = refs[n_pages]
    qi = qi_ref[0]
    w = w_ref[0][:, 0:1]

    def scores(keysT):
        t = jnp.maximum(jnp.dot(qi, keysT.astype(BF16), preferred_element_type=F32), 0.0) * w
        acc = t[0:SROWS]
        for h in range(1, IDX_HEADS):
            acc = acc + t[h * SROWS:(h + 1) * SROWS]
        return acc

    past = scores(jnp.concatenate([p[0, 0] for p in pages], axis=1))
    new = scores(kinew_ref[0])
    qrow = lax.broadcasted_iota(jnp.int32, new.shape, 0)
    kcol = lax.broadcasted_iota(jnp.int32, new.shape, 1)
    new = jnp.where(kcol <= jnp.minimum(qrow, t_new - 1), new, -jnp.inf)
    pad = jnp.full((SROWS, out_ref.shape[2] - past.shape[1] - new.shape[1]), -jnp.inf, F32)
    out_ref[0] = jnp.concatenate([past, new, pad], axis=1)


def _sample_scores(page_table, qi_h, w_h, kiT_new, cache_iT, layer, *, t_new):
    db, n_pages = page_table.shape
    page = cache_iT.shape[3]
    width = (n_pages // PAGES_PER_STEP + 1) * PAGES_PER_STEP * page

    def page_spec(i):
        return pl.BlockSpec((1, 1, IDX_DIM, page), lambda b, pt: (layer, pt[b, i], 0, 0))

    per_b = lambda *shape: pl.BlockSpec((1,) + shape, lambda b, pt: (b,) + (0,) * len(shape))
    grid_spec = pltpu.PrefetchScalarGridSpec(
        num_scalar_prefetch=1, grid=(db,),
        in_specs=[per_b(IDX_HEADS * SROWS, IDX_DIM), per_b(IDX_HEADS * SROWS, LANE), per_b(IDX_DIM, page)]
                 + [page_spec(i) for i in range(n_pages)],
        out_specs=per_b(SROWS, width))
    return pl.pallas_call(
        functools.partial(_sample_scores_kernel, n_pages=n_pages, t_new=t_new),
        grid_spec=grid_spec,
        out_shape=jax.ShapeDtypeStruct((db, SROWS, width), F32),
        compiler_params=pltpu.CompilerParams(dimension_semantics=("arbitrary",),
                                             vmem_limit_bytes=VMEM_LIMIT),
        name="sample_scores",
    )(page_table, qi_h, w_h, kiT_new, *([cache_iT] * n_pages))


def _sample_threshold_kernel(sc_ref, thr_ref, take_ref, tb_ref, *, topk):
    keys, lanes = sc_ref.shape

    def truncate_chunk(c, carry):
        off = pl.multiple_of(c * CK, CK)
        tb_ref[pl.ds(off, CK), :] = _truncate_bf16(sc_ref[pl.ds(off, CK), :])
        return carry

    lax.fori_loop(0, keys // CK, truncate_chunk, 0)
    thr, n_eq_take = _topk_threshold(sc_ref, tb_ref, keys // CK, jnp.full((1, lanes), topk, jnp.int32))
    thr_ref[...] = jnp.broadcast_to(thr, thr_ref.shape)
    take_ref[...] = jnp.broadcast_to(n_eq_take, take_ref.shape)


def _sample_threshold(scoresT, *, topk):
    keys, nq = scoresT.shape
    return pl.pallas_call(
        functools.partial(_sample_threshold_kernel, topk=topk),
        grid=(nq // LANE,),
        in_specs=[pl.BlockSpec((keys, LANE), lambda i: (0, i))],
        out_specs=[pl.BlockSpec((SUBLANE, LANE), lambda i: (0, i))] * 2,
        out_shape=[jax.ShapeDtypeStruct((SUBLANE, nq), F32)] * 2,
        scratch_shapes=[pltpu.VMEM((keys, LANE), BF16)],
        compiler_params=pltpu.CompilerParams(dimension_semantics=("arbitrary",),
                                             vmem_limit_bytes=VMEM_LIMIT),
        name="sample_threshold",
    )(scoresT)


def _sample_attn_kernel(pt_ref, q_ref, sc_ref, thr_ref, take_ref, kTnew_ref, vTnew_ref, *refs, nch):
    kpages = refs[:PAGES_PER_STEP]
    vpages = refs[PAGES_PER_STEP:2 * PAGES_PER_STEP]
    o_ref, m_ref, l_ref, acc_ref, seen_ref = refs[2 * PAGES_PER_STEP:]
    c = pl.program_id(1)

    @pl.when(c == 0)
    def _():
        m_ref[...] = jnp.full_like(m_ref, NEG_BIG)
        l_ref[...] = jnp.zeros_like(l_ref)
        acc_ref[...] = jnp.zeros_like(acc_ref)
        seen_ref[...] = jnp.zeros_like(seen_ref)

    thr = thr_ref[0][:, 0:1]
    take = take_ref[0][:, 0:1]

    def attend(kT, vT):
        width = kT.shape[1]
        sel, seen = _select_rows(sc_ref[0][:, 0:width], thr, take, seen_ref[:, 0:1],
                                 _tri(min(CK, width), lower=False))
        seen_ref[...] = jnp.broadcast_to(seen, seen_ref.shape)
        sel = jnp.tile(sel, (N_HEADS, 1))
        lg = jnp.where(sel, jnp.dot(q_ref[0], kT.astype(BF16), preferred_element_type=F32), -jnp.inf)
        m_old = m_ref[:, 0:1]
        m_new = jnp.maximum(m_old, jnp.max(lg, axis=1, keepdims=True))
        p = jnp.exp(lg - m_new)
        alpha = jnp.exp(m_old - m_new)
        l_new = alpha * l_ref[:, 0:1] + jnp.sum(p, axis=1, keepdims=True)
        pv = lax.dot_general(p.astype(BF16), vT.astype(BF16), (((1,), (1,)), ((), ())),
                             preferred_element_type=F32)
        acc_ref[...] = alpha * acc_ref[...] + pv
        m_ref[...] = jnp.broadcast_to(m_new, m_ref.shape)
        l_ref[...] = jnp.broadcast_to(l_new, l_ref.shape)

    def stack(pages):
        return jnp.concatenate([p[0, 0].reshape(E_A, p.shape[4]) for p in pages], axis=1)

    @pl.when(c < nch)
    def _():
        attend(stack(kpages), stack(vpages))

    @pl.when(c == nch)
    def _():
        attend(kTnew_ref[0], vTnew_ref[0])
        o = acc_ref[...] / l_ref[:, 0:1]
        o_ref[0] = jnp.concatenate(
            [o[h * SROWS:(h + 1) * SROWS, h * HEAD_DIM:(h + 1) * HEAD_DIM] for h in range(N_HEADS)], axis=1)


def _sample_attn(page_table, q_bd, scores, thr, take, kT_new, vT_new, cache_kT, cache_vT, layer):
    db, n_pages = page_table.shape
    nch = n_pages // PAGES_PER_STEP
    page = cache_kT.shape[4]
    sck = PAGES_PER_STEP * page
    hq = N_HEADS * SROWS

    def page_spec(i):
        return pl.BlockSpec(
            (1, 1, N_HEADS, HEAD_DIM, page),
            lambda b, c, pt: (layer, pt[b, jnp.minimum(c, nch - 1) * PAGES_PER_STEP + i], 0, 0, 0))

    per_b = lambda *shape: pl.BlockSpec((1,) + shape, lambda b, c, pt: (b,) + (0,) * len(shape))
    grid_spec = pltpu.PrefetchScalarGridSpec(
        num_scalar_prefetch=1, grid=(db, nch + 1),
        in_specs=[per_b(hq, E_A),
                  pl.BlockSpec((1, SROWS, sck), lambda b, c, pt: (b, 0, c)),
                  per_b(SROWS, LANE), per_b(SROWS, LANE),
                  per_b(E_A, page), per_b(E_A, page)]
                 + [page_spec(i) for i in range(PAGES_PER_STEP)] * 2,
        out_specs=per_b(SROWS, E_A),
        scratch_shapes=[pltpu.VMEM((hq, LANE), F32), pltpu.VMEM((hq, LANE), F32),
                        pltpu.VMEM((hq, E_A), F32), pltpu.VMEM((SROWS, LANE), F32)])
    return pl.pallas_call(
        functools.partial(_sample_attn_kernel, nch=nch),
        grid_spec=grid_spec,
        out_shape=jax.ShapeDtypeStruct((db, SROWS, E_A), F32),
        compiler_params=pltpu.CompilerParams(dimension_semantics=("arbitrary", "arbitrary"),
                                             vmem_limit_bytes=VMEM_LIMIT),
        name="sample_attn",
    )(page_table, q_bd, scores, thr, take, kT_new, vT_new,
      *([cache_kT] * PAGES_PER_STEP), *([cache_vT] * PAGES_PER_STEP))


def _post_kernel(x_ref, oa_ref, g0_ref, mbc_ref, wba_ref, wo_ref, wup_ref, wdn_ref,
                 gpost_ref, gfpre_ref, gfpost_ref, y_ref):
    m = g0_ref[...].astype(F32) * jnp.dot(oa_ref[...], wba_ref[...], preferred_element_type=F32) \
        + mbc_ref[...]
    y = jnp.dot(m.astype(BF16), wo_ref[...], preferred_element_type=F32)
    x1 = x_ref[...] + _rms(y, gpost_ref[...])
    a = jnp.maximum(jnp.dot(_rms(x1, gfpre_ref[...]).astype(BF16), wup_ref[...],
                            preferred_element_type=F32), 0.0)
    f = jnp.dot((a * a).astype(BF16), wdn_ref[...], preferred_element_type=F32)
    y_ref[...] = x1 + _rms(f, gfpost_ref[...])


def _post(x, oa, g0, mbc, wba, wo, wup, wdn, gpost, gfpre, gfpost, *, layer):
    m = x.shape[0]
    _const_spec = functools.partial(_layer_spec, layer)
    tm = min(POST_ROW_TILE, m)
    row = lambda n: pl.BlockSpec((tm, n), lambda i: (i, 0))
    return pl.pallas_call(
        _post_kernel, grid=(m // tm,),
        in_specs=[row(D_MODEL), row(E_A), row(D_MODEL), row(D_MODEL),
                  _const_spec((E_A, D_MODEL)), _const_spec((D_MODEL, D_MODEL)),
                  _const_spec((D_MODEL, D_FF)), _const_spec((D_FF, D_MODEL)),
                  _const_spec((1, D_MODEL)), _const_spec((1, D_MODEL)), _const_spec((1, D_MODEL))],
        out_specs=row(D_MODEL),
        out_shape=jax.ShapeDtypeStruct((m, D_MODEL), F32),
        compiler_params=pltpu.CompilerParams(dimension_semantics=("arbitrary",),
                                             vmem_limit_bytes=VMEM_LIMIT),
        name="post",
    )(x, oa, g0, mbc, wba, wo, wup, wdn, gpost, gfpre, gfpost)


def _rope_tables(pos):
    half = HEAD_DIM // 2
    inv = ROPE_THETA ** (-jnp.arange(half, dtype=F32) * (2.0 / HEAD_DIM))
    ang = pos[:, None] * inv[None, :]
    c, s = jnp.cos(ang), jnp.sin(ang)
    return jnp.tile(c, (1, LANE // half)), jnp.tile(jnp.concatenate([-s, s], axis=1), (1, LANE // HEAD_DIM))


def _heads_first(a, t):
    db = a.shape[0] // t
    a = a.reshape(db, t, N_HEADS, -1).transpose(0, 2, 1, 3)
    a = jnp.pad(a, ((0, 0), (0, 0), (0, SROWS - t), (0, 0)))
    return a.reshape(db, N_HEADS * SROWS, -1)


def _new_keys_T(a, db, t, page):
    a = jnp.pad(a.reshape(db, t, -1), ((0, 0), (0, page - t), (0, 0)))
    return a.transpose(0, 2, 1)


def kernel(x_prompt, x_sample, cache_k, cache_v, cache_idx_k, state_conv, page_table, norm_mix_pre, norm_mix_post, norm_ffn_pre, norm_ffn_post, w_in, gmlp_ln_g, gmlp_ln_b, gmlp_ws, gmlp_bs, conv_w, conv_b, w_br_attn, w_br_gmlp, w_br_conv, w_out, w_ff_up, w_ff_down):
    depth = w_in.shape[0]
    b, s, _ = x_prompt.shape
    db, t, _ = x_sample.shape
    page = cache_k.shape[2]
    n_pages = page_table.shape[1]
    past = n_pages * page
    ms = db * t
    assert s % POST_ROW_TILE == 0 and s % (2 * CK) == 0 and ms == CHUNK and CONV_W - 1 <= t <= SROWS
    assert page == LANE and n_pages % PAGES_PER_STEP == 0

    cache_kT = cache_k.transpose(0, 1, 3, 4, 2)
    cache_vT = cache_v.transpose(0, 1, 3, 4, 2)
    cache_iT = cache_idx_k.transpose(0, 1, 3, 2)

    w_in_p = jnp.concatenate(
        [w_in[:, :, :OFF_B], jnp.zeros((depth, D_MODEL, C_B - OFF_B), w_in.dtype), w_in[:, :, OFF_B:]],
        axis=2).astype(BF16)
    causal = jnp.tril(jnp.ones((CHUNK, CHUNK), bool))
    wm = jnp.where(causal[None, None], gmlp_ws, 0)
    wmix_p = wm.astype(BF16)
    bmix_p = jnp.repeat(jnp.swapaxes(gmlp_bs, 1, 2), E_B // GMLP_GROUPS, axis=2)
    eye = jnp.eye(db, dtype=wm.dtype)
    wmix_s = jnp.einsum('ab,lgts->lgatbs', eye, wm[:, :, :t, :t]).reshape(depth, GMLP_GROUPS, ms, ms)
    wmix_s = wmix_s.astype(BF16)
    bmix_s = jnp.tile(bmix_p[:, :t], (1, db, 1))
    wbb, wbc, wba = (w.astype(BF16) for w in (w_br_gmlp, w_br_conv, w_br_attn))
    wo, wup, wdn = (w.astype(BF16) for w in (w_out, w_ff_up, w_ff_down))
    row_stack = lambda a: a[:, None, :]
    common = (row_stack(norm_mix_pre), w_in_p)
    tail = (row_stack(gmlp_ln_g), row_stack(gmlp_ln_b))
    conv = (conv_w, row_stack(conv_b), wbb, wbc)
    post_w = (wba, wo, wup, wdn, row_stack(norm_mix_post), row_stack(norm_ffn_pre),
              row_stack(norm_ffn_post))
    zs = jnp.zeros((depth, db, 1, E_C), state_conv.dtype)
    st1 = jnp.concatenate([state_conv[:, :, 1:2]] + [zs] * (t - 1), axis=2)
    st2 = jnp.concatenate([state_conv[:, :, 0:1], state_conv[:, :, 1:2]] + [zs] * (t - 2), axis=2)
    st12 = jnp.concatenate([st1, st2], axis=3).reshape(depth, ms, 2 * E_C)

    cos_p, sin_p = _rope_tables(jnp.arange(s, dtype=F32))
    cos_s, sin_s = _rope_tables(jnp.tile(jnp.arange(t, dtype=F32) + past, db))
    topk_p = min(TOPK_MAX, s // 4)
    topk_s = min(TOPK_MAX, (past + t) // 4)
    head_eye = jnp.eye(N_HEADS, dtype=BF16)

    xp = x_prompt.reshape(b * s, D_MODEL)
    xs = x_sample.reshape(ms, D_MODEL)
    outs = [[] for _ in range(9)]
    for l in range(depth):
        (qT, qiT, wT, kb, kib, vTb, kT, vT, kiT, g0, mbc, ctail) = _proj(
            xp, *common, cos_p, sin_p, *tail, wmix_p, bmix_p, *conv, layer=l, sample=False, seq_len=s)
        oa = _attn_prompt(qT, qiT, wT, kib.reshape(b, s, IDX_DIM), kb.reshape(b, s, E_A), vTb, topk=topk_p)
        xp = _post(xp, oa.reshape(b * s, E_A), g0, mbc, *post_w, layer=l)
        outs[0].append(kT.reshape(b, N_HEADS, HEAD_DIM, s))
        outs[1].append(vT.reshape(b, N_HEADS, HEAD_DIM, s))
        outs[2].append(kiT)
        outs[3].append(ctail[:, SUBLANE - (CONV_W - 1):, :])
        (q_s, k_s, v_s, qi_s, kw_s, g0_s, mbc_s, cin_s, vn_s) = _proj(
            xs, *common, cos_s, sin_s, *tail, wmix_s, bmix_s, *conv, layer=l, sample=True, seq_len=t,
            st=st12)
        qi_h = _heads_first(qi_s, t)
        w_h = _heads_first(kw_s[:, IDX_DIM:IDX_DIM + IDX_HEADS][:, :, None], t)
        w_h = jnp.broadcast_to(w_h, w_h.shape[:2] + (LANE,))
        scores = _sample_scores(page_table, qi_h, w_h, _new_keys_T(kw_s[:, :IDX_DIM], db, t, page),
                                cache_iT, l, t_new=t)
        thr, take = _sample_threshold(scores.reshape(db * SROWS, -1).T, topk=topk_s)
        per_q = lambda a: jnp.broadcast_to(a[0].reshape(db, SROWS, 1), (db, SROWS, LANE))
        q4 = jnp.pad(q_s.reshape(db, t, N_HEADS, HEAD_DIM), ((0, 0), (0, SROWS - t), (0, 0), (0, 0)))
        q_bd = jnp.einsum('bqhd,hg->bhqgd', q4, head_eye).reshape(db, N_HEADS * SROWS, E_A)
        oa_s = _sample_attn(page_table, q_bd, scores, per_q(thr), per_q(take),
                            _new_keys_T(k_s, db, t, page), _new_keys_T(v_s, db, t, page),
                            cache_kT, cache_vT, l)
        oa_s = oa_s[:, :t].reshape(ms, E_A).astype(BF16)
        xs = _post(xs, oa_s, g0_s, mbc_s, *post_w, layer=l)
        outs[4].append(k_s.reshape(db, t, N_HEADS, HEAD_DIM))
        outs[5].append(v_s.reshape(db, t, N_HEADS, HEAD_DIM))
        outs[6].append(kw_s[:, :IDX_DIM].reshape(db, t, IDX_DIM))
        outs[7].append(cin_s.reshape(db, t, E_C)[:, t - (CONV_W - 1):])
        outs[8].append(vn_s.reshape(db, t, E_B))
    stacked = [jnp.stack(o) for o in outs]
    stacked[0] = stacked[0].transpose(0, 1, 4, 2, 3)
    stacked[1] = stacked[1].transpose(0, 1, 4, 2, 3)
    stacked[2] = stacked[2].transpose(0, 1, 3, 2)
    return (xp.reshape(b, s, D_MODEL), xs.reshape(db, t, D_MODEL)) + tuple(stacked)
```

```python
import functools
import math

import jax
import jax.numpy as jnp
from jax import lax
from jax.experimental import pallas as pl
from jax.experimental.pallas import tpu as pltpu

F32 = jnp.float32
BF16 = jnp.bfloat16

D_MODEL = 1024
N_HEADS = 8
HEAD_DIM = 64
E_A = N_HEADS * HEAD_DIM
IDX_HEADS = 8
IDX_DIM = 64
IDX_W_SCALE = 1.0 / math.sqrt(IDX_HEADS * IDX_DIM)
TOPK_MAX = 256
Q_BLOCK = 256
ROPE_THETA = 10000.0
CHUNK = 128
GMLP_GROUPS = 4
E_B = 512
E_C = 512
CONV_W = 3
D_FF = 4 * D_MODEL
EPS = 1e-6

OFF_KI = 3 * E_A + IDX_HEADS * IDX_DIM
OFF_B = OFF_KI + IDX_DIM + IDX_HEADS
LANE = 128
SUBLANE = 8
C_KW = OFF_KI
C_B = C_KW + LANE
C_C = C_B + 2 * E_B
C_G = C_C + 3 * E_C
N_PAD = C_G + 3 * D_MODEL

VMEM_LIMIT = 56 * 1024 * 1024
ROW_TILE = 256
POST_ROW_TILE = 512
NEG_BIG = -1e30
KEY_NEG_INF = -2139095041
KEY_POS_INF = 2139095040
CK = 256
COUNT_ELEMS = 64 * SUBLANE * LANE
ACC_ROWS = HEAD_DIM + 16
LOG2_E = math.log2(math.e)
PAGES_PER_STEP = 16
SROWS = SUBLANE


def _rms(x, g):
    return x * lax.rsqrt(jnp.mean(x * x, axis=-1, keepdims=True) + EPS) * g


def _rope128(z, cos, sin):
    lane = lax.broadcasted_iota(jnp.int32, z.shape, 1)
    partner = jnp.where((lane % HEAD_DIM) < HEAD_DIM // 2,
                        pltpu.roll(z, LANE - HEAD_DIM // 2, 1),
                        pltpu.roll(z, HEAD_DIM // 2, 1))
    return z * cos + partner * sin


def _rope(z, cos, sin):
    return jnp.concatenate(
        [_rope128(z[:, i:i + LANE], cos, sin) for i in range(0, z.shape[1], LANE)], axis=1)


def _proj_kernel(*refs, sample, tm, seq_rows):
    if sample:
        (x_ref, gpre_ref, w_ref, cos_ref, sin_ref, lng_ref, lnb_ref, wmix_ref, bmix_ref,
         cw_ref, cbias_ref, wbb_ref, wbc_ref, st_ref,
         q_ref, k_ref, v_ref, qi_ref, kw_ref, g0_ref, mbc_ref, cin_ref, vn_ref) = refs
    else:
        (x_ref, gpre_ref, w_ref, cos_ref, sin_ref, lng_ref, lnb_ref, wmix_ref, bmix_ref,
         cw_ref, cbias_ref, wbb_ref, wbc_ref,
         qT_ref, qiT_ref, wT_ref, kb_ref, kib_ref, vTb_ref, kT_ref, vT_ref, kiT_ref,
         g0_ref, mbc_ref, ctail_ref, carry_ref) = refs

    h = _rms(x_ref[...], gpre_ref[...]).astype(BF16)
    cos = cos_ref[...]
    sin = sin_ref[...]

    z = jnp.dot(h, w_ref[:, 0:C_KW], preferred_element_type=F32)
    q = _rope(z[:, 0:E_A], cos, sin) * (HEAD_DIM ** -0.5 * (1.0 if sample else LOG2_E))
    k = _rope(z[:, E_A:2 * E_A], cos, sin)
    v = z[:, 2 * E_A:3 * E_A]
    qi = _rope(z[:, 3 * E_A:4 * E_A], cos, sin)

    zk = jnp.dot(h, w_ref[:, C_KW:C_B], preferred_element_type=F32)
    lane = lax.broadcasted_iota(jnp.int32, zk.shape, 1)
    kw = jnp.where(lane < IDX_DIM, _rope128(zk, cos, sin), zk * IDX_W_SCALE)
    if sample:
        q_ref[...] = q.astype(BF16)
        k_ref[...] = k
        v_ref[...] = v
        qi_ref[...] = qi.astype(BF16)
        kw_ref[...] = kw
    else:
        qT_ref[0] = q.T.astype(BF16)
        qiT_ref[0] = qi.T.astype(BF16)
        kT = k.T
        kT_ref[0] = kT
        kb_ref[...] = k.astype(BF16)
        vT = v.T
        vT_ref[0] = vT
        vTb_ref[0] = vT.astype(BF16)
        kwT = kw.T
        kiT_ref[0] = kwT[0:IDX_DIM, :]
        wT_ref[0] = kwT[IDX_DIM:IDX_DIM + IDX_HEADS, :]
        kib_ref[...] = kw[:, 0:IDX_DIM].astype(BF16)

    gl = jax.nn.gelu(jnp.dot(h, w_ref[:, C_B:C_C], preferred_element_type=F32))
    u = gl[:, 0:E_B]
    vg = gl[:, E_B:2 * E_B]
    mu = jnp.mean(vg, axis=-1, keepdims=True)
    vc = vg - mu
    vn = vc * lax.rsqrt(jnp.mean(vc * vc, axis=-1, keepdims=True) + EPS) * lng_ref[...] + lnb_ref[...]
    if sample:
        vn_ref[...] = vn
    vnb = vn.astype(BF16)
    gw = E_B // GMLP_GROUPS
    rows = []
    for c in range(tm // CHUNK):
        cols = []
        for g in range(GMLP_GROUPS):
            cols.append(jnp.dot(wmix_ref[g], vnb[c * CHUNK:(c + 1) * CHUNK, g * gw:(g + 1) * gw],
                                preferred_element_type=F32))
        rows.append(jnp.concatenate(cols, axis=1) + bmix_ref[...])
    mix = rows[0] if len(rows) == 1 else jnp.concatenate(rows, axis=0)
    ob = (u * mix).astype(BF16)

    zc = jnp.dot(h, w_ref[:, C_C:C_G], preferred_element_type=F32)
    cb = zc[:, 0:E_C]
    cin = zc[:, E_C:2 * E_C] * zc[:, 2 * E_C:3 * E_C]
    row = lax.broadcasted_iota(jnp.int32, cin.shape, 0)
    if sample:
        cin_ref[...] = cin
        i_in_seq = row % seq_rows
        s1 = jnp.where(i_in_seq >= 1, pltpu.roll(cin, 1, 0), st_ref[:, 0:E_C])
        s2 = jnp.where(i_in_seq >= 2, pltpu.roll(cin, 2, 0), st_ref[:, E_C:2 * E_C])
    else:
        @pl.when(pl.program_id(0) % (seq_rows // tm) == 0)
        def _():
            carry_ref[...] = jnp.zeros_like(carry_ref)
        p2 = carry_ref[0:1, :]
        p1 = carry_ref[1:2, :]
        s1 = jnp.where(row >= 1, pltpu.roll(cin, 1, 0), p1)
        s2 = jnp.where(row >= 2, pltpu.roll(cin, 2, 0), jnp.where(row == 0, p2, p1))
        carry_ref[0:2, :] = cin[tm - 2:tm, :]
        ctail_ref[0] = cin[tm - SUBLANE:tm, :]
    y = s2 * cw_ref[0:1, :] + s1 * cw_ref[1:2, :] + cin * cw_ref[2:3, :] + cbias_ref[...]
    oc = (cb * y).astype(BF16)

    g0_ref[...] = jax.nn.sigmoid(
        jnp.dot(h, w_ref[:, C_G:C_G + D_MODEL], preferred_element_type=F32)).astype(BF16)
    g1 = jax.nn.sigmoid(jnp.dot(h, w_ref[:, C_G + D_MODEL:C_G + 2 * D_MODEL], preferred_element_type=F32))
    mbc = g1 * jnp.dot(ob, wbb_ref[...], preferred_element_type=F32)
    g2 = jax.nn.sigmoid(jnp.dot(h, w_ref[:, C_G + 2 * D_MODEL:N_PAD], preferred_element_type=F32))
    mbc_ref[...] = mbc + g2 * jnp.dot(oc, wbc_ref[...], preferred_element_type=F32)


def _layer_spec(layer, shape):
    nd = len(shape)
    return pl.BlockSpec((None,) + shape, lambda *_: (layer,) + (0,) * nd, pipeline_mode=pl.Buffered(1))


def _proj(x, gpre, w, cos, sin, lng, lnb, wmix, bmix, cw, cbias, wbb, wbc, *, layer, sample, seq_len,
          st=None):
    m = x.shape[0]
    _const_spec = functools.partial(_layer_spec, layer)
    tm = m if sample else ROW_TILE
    tps = max(seq_len // tm, 1)
    n_seq = m // seq_len
    row = lambda n: pl.BlockSpec((tm, n), lambda i: (i, 0))
    pos_spec = pl.BlockSpec((tm, LANE), (lambda i: (0, 0)) if sample else (lambda i: (i % tps, 0)))
    in_specs = [row(D_MODEL), _const_spec((1, D_MODEL)), _const_spec((D_MODEL, N_PAD)),
                pos_spec, pos_spec, _const_spec((1, E_B)), _const_spec((1, E_B)),
                _const_spec((GMLP_GROUPS, CHUNK, CHUNK)), _const_spec((CHUNK, E_B)),
                _const_spec((CONV_W, E_C)), _const_spec((1, E_C)),
                _const_spec((E_B, D_MODEL)), _const_spec((E_C, D_MODEL))]
    args = [x, gpre, w, cos, sin, lng, lnb, wmix, bmix, cw, cbias, wbb, wbc]
    sds = jax.ShapeDtypeStruct
    if sample:
        in_specs.append(pl.BlockSpec((None, tm, 2 * E_C), lambda i: (layer, i, 0)))
        args.append(st)
        out_shape = [sds((m, E_A), BF16), sds((m, E_A), F32), sds((m, E_A), F32), sds((m, E_A), BF16),
                     sds((m, LANE), F32), sds((m, D_MODEL), BF16), sds((m, D_MODEL), F32),
                     sds((m, E_C), F32), sds((m, E_B), F32)]
        out_specs = [row(E_A), row(E_A), row(E_A), row(E_A), row(LANE), row(D_MODEL), row(D_MODEL),
                     row(E_C), row(E_B)]
        scratch = []
    else:
        colT = lambda n: pl.BlockSpec((1, n, tm), lambda i: (i // tps, 0, i % tps))
        out_shape = [sds((n_seq, E_A, seq_len), BF16), sds((n_seq, E_A, seq_len), BF16),
                     sds((n_seq, IDX_HEADS, seq_len), F32), sds((m, E_A), BF16), sds((m, IDX_DIM), BF16),
                     sds((n_seq, E_A, seq_len), BF16), sds((n_seq, E_A, seq_len), F32),
                     sds((n_seq, E_A, seq_len), F32), sds((n_seq, IDX_DIM, seq_len), F32),
                     sds((m, D_MODEL), BF16), sds((m, D_MODEL), F32), sds((n_seq, SUBLANE, E_C), F32)]
        out_specs = [colT(E_A), colT(E_A), colT(IDX_HEADS), row(E_A), row(IDX_DIM),
                     colT(E_A), colT(E_A), colT(E_A), colT(IDX_DIM),
                     row(D_MODEL), row(D_MODEL),
                     pl.BlockSpec((1, SUBLANE, E_C), lambda i: (i // tps, 0, 0))]
        scratch = [pltpu.VMEM((SUBLANE, E_C), F32)]
    return pl.pallas_call(
        functools.partial(_proj_kernel, sample=sample, tm=tm, seq_rows=seq_len),
        grid=(m // tm,), in_specs=in_specs, out_specs=out_specs, out_shape=out_shape,
        scratch_shapes=scratch,
        compiler_params=pltpu.CompilerParams(dimension_semantics=("arbitrary",),
                                             vmem_limit_bytes=VMEM_LIMIT),
        name="proj_sample" if sample else "proj_prompt",
    )(*args)


def _key_to_f32(key):
    bits = key ^ ((key >> 31) & jnp.int32(0x7FFFFFFF))
    return lax.bitcast_convert_type(bits, F32)


def _truncate_bf16(x):
    bits = lax.bitcast_convert_type(x, jnp.int32) & jnp.int32(-65536)
    return lax.bitcast_convert_type(bits, F32).astype(BF16)


def _topk_threshold(sc_ref, tb_ref, n_chunks, n_chunks_packed, k_row):
    lanes = sc_ref.shape[1]
    k_f = k_row.astype(F32)
    n_chunks_f32 = n_chunks

    def count(ref, pred_fn, dtype):
        n_chunks = n_chunks_packed if dtype == BF16 else n_chunks_f32
        count_rows = COUNT_ELEMS // lanes * (4 // jnp.dtype(dtype).itemsize)
        acc_rows = count_rows // 8
        one, zero = jnp.ones((), dtype), jnp.zeros((), dtype)

        def body(c, acc):
            off = pl.multiple_of(c * count_rows, count_rows)
            hit = jnp.where(pred_fn(ref[pl.ds(off, count_rows), :]), one, zero)
            p = [hit[i * acc_rows:(i + 1) * acc_rows] for i in range(8)]
            return acc + (((p[0] + p[1]) + (p[2] + p[3])) + ((p[4] + p[5]) + (p[6] + p[7])))
        acc = lax.fori_loop(0, n_chunks * CK // count_rows, body, jnp.zeros((acc_rows, lanes), dtype))
        return jnp.sum(acc.astype(F32), axis=0, keepdims=True)

    def bisect_high(_, lohi):
        lo, hi = lohi
        mid = (lo + hi) >> 1
        rep = (mid << 16) | jnp.where(mid < 0, jnp.int32(0xFFFF), jnp.int32(0))
        midb = _key_to_f32(rep).astype(BF16)
        ge = count(tb_ref, lambda t: t >= midb, BF16) >= k_f
        return jnp.where(ge, mid, lo), jnp.where(ge, hi, mid)

    def bisect_low(_, lohi):
        lo, hi = lohi
        mid = (lo >> 1) + (hi >> 1) + (lo & hi & 1)
        midf = _key_to_f32(mid)
        ge = count(sc_ref, lambda s: s >= midf, F32) >= k_f
        return jnp.where(ge, mid, lo), jnp.where(ge, hi, mid)

    lo0 = jnp.full((1, lanes), KEY_NEG_INF >> 16, jnp.int32)
    hi0 = jnp.full((1, lanes), (KEY_POS_INF >> 16) + 1, jnp.int32)
    hi_half, _ = lax.fori_loop(0, 16, bisect_high, (lo0, hi0))
    lo, _ = lax.fori_loop(0, 16, bisect_low, (hi_half << 16, (hi_half << 16) + 65536))
    thr = _key_to_f32(lo)
    n_eq_take = k_f - count(sc_ref, lambda s: s > thr, F32)
    return thr, n_eq_take


def _select_cols(s, thr, n_eq_take, eq_seen, tril):
    eq = s == thr
    prefix = jnp.dot(tril, jnp.where(eq, 1.0, 0.0).astype(BF16), preferred_element_type=F32) + eq_seen
    sel = (s > thr) | (eq & (prefix <= n_eq_take))
    return sel, prefix[s.shape[0] - 1:s.shape[0], :]


def _select_rows(s, thr, n_eq_take, eq_seen, triu):
    blk = triu.shape[0]
    eq = s == thr
    eqf = jnp.where(eq, 1.0, 0.0)
    prefixes = []
    for i in range(s.shape[1] // blk):
        e = eqf[:, i * blk:(i + 1) * blk]
        prefixes.append(jnp.dot(e.astype(BF16), triu, preferred_element_type=F32) + eq_seen)
        eq_seen = eq_seen + jnp.sum(e, axis=1, keepdims=True)
    prefix = prefixes[0] if len(prefixes) == 1 else jnp.concatenate(prefixes, axis=1)
    return (s > thr) | (eq & (prefix <= n_eq_take)), eq_seen


def _tri(n, lower):
    r = lax.broadcasted_iota(jnp.int32, (n, n), 0)
    c = lax.broadcasted_iota(jnp.int32, (n, n), 1)
    return jnp.where((c <= r) if lower else (r <= c), 1.0, 0.0).astype(BF16)


def _attn_prompt_kernel(qT_ref, qiT_ref, wT_ref, ki_ref, k_ref, vT_ref, o_ref, sc_ref, tb_ref, acc_ref,
                        lga_ref, lgb_ref, wq_ref, tri_ref, *, topk):
    j = pl.program_id(1)
    nck = (j * Q_BLOCK) // CK + 1
    qiT = qiT_ref[0]
    w_idx = jnp.concatenate([qiT[h * IDX_DIM:(h + 1) * IDX_DIM, :] for h in range(IDX_HEADS)], axis=1)
    wT = wT_ref[0]
    q_pos = j * Q_BLOCK + lax.broadcasted_iota(jnp.int32, (1, Q_BLOCK), 1)

    def score_chunk(c, carry):
        off = pl.multiple_of(c * CK, CK)
        d = jnp.dot(ki_ref[0, pl.ds(off, CK), :], w_idx, preferred_element_type=F32)
        acc = jnp.maximum(d[:, 0:Q_BLOCK], 0.0) * wT[0:1, :]
        for h in range(1, IDX_HEADS):
            acc = acc + jnp.maximum(d[:, h * Q_BLOCK:(h + 1) * Q_BLOCK], 0.0) * wT[h:h + 1, :]
        key_pos = off + lax.broadcasted_iota(jnp.int32, (CK, 1), 0)
        acc = jnp.where(key_pos <= q_pos, acc, -jnp.inf)
        sc_ref[pl.ds(off, CK), :] = acc
        tb_ref[pl.ds(off, CK), :] = _truncate_bf16(acc)
        return carry

    lax.fori_loop(0, nck, score_chunk, 0)

    @pl.when(nck * CK < sc_ref.shape[0])
    def _():
        off = pl.multiple_of(nck * CK, CK)
        sc_ref[pl.ds(off, CK), :] = jnp.full((CK, Q_BLOCK), -jnp.inf, F32)
        tb_ref[pl.ds(off, CK), :] = jnp.full((CK, Q_BLOCK), -jnp.inf, BF16)

    thr, n_eq_take = _topk_threshold(sc_ref, tb_ref, nck, 2 * ((nck + 1) // 2),
                                     jnp.minimum(q_pos + 1, topk))

    qT = qT_ref[0].astype(F32)
    upper = lax.broadcasted_iota(jnp.int32, (LANE, Q_BLOCK), 0) < HEAD_DIM
    for h in range(N_HEADS):
        slab = qT[(h // 2) * LANE:(h // 2 + 1) * LANE, :]
        wq_ref[h] = jnp.where(upper if h % 2 == 0 else ~upper, slab, 0.0).astype(BF16)
    tri_ref[...] = _tri(CK, lower=True)
    acc_ref[...] = jnp.zeros_like(acc_ref)

    last_chunk = k_ref.shape[1] // CK - 1

    def chunk_offset(c):
        return pl.multiple_of(jnp.minimum(c, last_chunk) * CK, CK)

    def masked_logits(c, eq_seen, lg_ref):
        off = chunk_offset(c)
        sel, eq_seen = _select_cols(sc_ref[pl.ds(off, CK), :], thr, n_eq_take, eq_seen, tri_ref[...])
        sc_ref[pl.ds(off, CK), :] = jnp.where(sel, 0.0, -jnp.inf)
        col_max = []
        for h in range(N_HEADS):
            kh = k_ref[0, pl.ds(off, CK), (h // 2) * LANE:(h // 2 + 1) * LANE]
            lg = jnp.dot(kh, wq_ref[h], preferred_element_type=F32) + sc_ref[pl.ds(off, CK), :]
            lg_ref[h] = lg
            col_max.append(jnp.max(lg, axis=0, keepdims=True))
        return eq_seen, jnp.concatenate(col_max, axis=0)

    def softmax_update(c, lg_ref, lg_next_ref, carry):
        eq_seen, m_all, chunk_max = carry
        m_new = jnp.maximum(m_all, chunk_max)
        alpha = jnp.exp2(m_all - m_new)
        if lg_next_ref is not None:
            eq_seen, chunk_max = masked_logits(c + 1, eq_seen, lg_next_ref)
        off = chunk_offset(c)
        ones_rows = jnp.ones((ACC_ROWS - HEAD_DIM, CK), BF16)
        for h in range(N_HEADS):
            p = jnp.exp2(lg_ref[h] - m_new[h:h + 1, :])
            lhs = jnp.concatenate([vT_ref[0, h * HEAD_DIM:(h + 1) * HEAD_DIM, pl.ds(off, CK)], ones_rows],
                                  axis=0)
            acc_ref[h] = alpha[h:h + 1, :] * acc_ref[h] + jnp.dot(lhs, p.astype(BF16),
                                                                  preferred_element_type=F32)
        return eq_seen, m_new, chunk_max

    def attend_pair(i, carry):
        carry = softmax_update(2 * i, lga_ref, lgb_ref, carry)
        return softmax_update(2 * i + 1, lgb_ref, lga_ref, carry)

    eq_seen0, chunk_max0 = masked_logits(0, jnp.zeros((1, Q_BLOCK), F32), lga_ref)
    carry = lax.fori_loop(0, nck // 2, attend_pair,
                          (eq_seen0, jnp.full((N_HEADS, Q_BLOCK), NEG_BIG, F32), chunk_max0))

    @pl.when(nck % 2 == 1)
    def _():
        softmax_update(nck - 1, lga_ref, None, carry)

    o_ref[0] = jnp.concatenate(
        [acc_ref[h][0:HEAD_DIM, :] / acc_ref[h][HEAD_DIM:HEAD_DIM + 1, :] for h in range(N_HEADS)],
        axis=0).T.astype(BF16)


def _attn_prompt(qT, qiT, wT, kib, kb, vTb, *, topk):
    b, _, s = qT.shape
    colT = lambda n: pl.BlockSpec((1, n, Q_BLOCK), lambda i, j: (i, 0, j))
    return pl.pallas_call(
        functools.partial(_attn_prompt_kernel, topk=topk),
        grid=(b, s // Q_BLOCK),
        in_specs=[colT(E_A), colT(E_A), colT(IDX_HEADS),
                  pl.BlockSpec((1, s, IDX_DIM), lambda i, j: (i, 0, 0)),
                  pl.BlockSpec((1, s, E_A), lambda i, j: (i, 0, 0)),
                  pl.BlockSpec((1, E_A, s), lambda i, j: (i, 0, 0))],
        out_specs=pl.BlockSpec((1, Q_BLOCK, E_A), lambda i, j: (i, j, 0)),
        out_shape=jax.ShapeDtypeStruct((b, s, E_A), BF16),
        scratch_shapes=[pltpu.VMEM((s, Q_BLOCK), F32), pltpu.VMEM((s, Q_BLOCK), BF16),
                        pltpu.VMEM((N_HEADS, ACC_ROWS, Q_BLOCK), F32),
                        pltpu.VMEM((N_HEADS, CK, Q_BLOCK), F32), pltpu.VMEM((N_HEADS, CK, Q_BLOCK), F32),
                        pltpu.VMEM((N_HEADS, LANE, Q_BLOCK), BF16), pltpu.VMEM((CK, CK), BF16)],
        compiler_params=pltpu.CompilerParams(dimension_semantics=("arbitrary", "arbitrary"),
                                             vmem_limit_bytes=VMEM_LIMIT),
        name="attn_prompt",
    )(qT, qiT, wT, kib, kb, vTb)


def _sample_scores_kernel(pt_ref, qi_ref, w_ref, kinew_ref, *refs, n_pages, t_new):
    pages = refs[:n_pages]
    out_ref = refs[n_pages]
    qi = qi_ref[0]
    w = w_ref[0][:, 0:1]

    def scores(keysT):
        t = jnp.maximum(jnp.dot(qi, keysT.astype(BF16), preferred_element_type=F32), 0.0) * w
        acc = t[0:SROWS]
        for h in range(1, IDX_HEADS):
            acc = acc + t[h * SROWS:(h + 1) * SROWS]
        return acc

    past = scores(jnp.concatenate([p[0, 0] for p in pages], axis=1))
    new = scores(kinew_ref[0])
    qrow = lax.broadcasted_iota(jnp.int32, new.shape, 0)
    kcol = lax.broadcasted_iota(jnp.int32, new.shape, 1)
    new = jnp.where(kcol <= jnp.minimum(qrow, t_new - 1), new, -jnp.inf)
    pad = jnp.full((SROWS, out_ref.shape[2] - past.shape[1] - new.shape[1]), -jnp.inf, F32)
    out_ref[0] = jnp.concatenate([past, new, pad], axis=1)


def _sample_scores(page_table, qi_h, w_h, kiT_new, cache_iT, layer, *, t_new):
    db, n_pages = page_table.shape
    page = cache_iT.shape[3]
    width = (n_pages // PAGES_PER_STEP + 1) * PAGES_PER_STEP * page

    def page_spec(i):
        return pl.BlockSpec((1, 1, IDX_DIM, page), lambda b, pt: (layer, pt[b, i], 0, 0))

    per_b = lambda *shape: pl.BlockSpec((1,) + shape, lambda b, pt: (b,) + (0,) * len(shape))
    grid_spec = pltpu.PrefetchScalarGridSpec(
        num_scalar_prefetch=1, grid=(db,),
        in_specs=[per_b(IDX_HEADS * SROWS, IDX_DIM), per_b(IDX_HEADS * SROWS, LANE), per_b(IDX_DIM, page)]
                 + [page_spec(i) for i in range(n_pages)],
        out_specs=per_b(SROWS, width))
    return pl.pallas_call(
        functools.partial(_sample_scores_kernel, n_pages=n_pages, t_new=t_new),
        grid_spec=grid_spec,
        out_shape=jax.ShapeDtypeStruct((db, SROWS, width), F32),
        compiler_params=pltpu.CompilerParams(dimension_semantics=("arbitrary",),
                                             vmem_limit_bytes=VMEM_LIMIT),
        name="sample_scores",
    )(page_table, qi_h, w_h, kiT_new, *([cache_iT] * n_pages))


def _sample_threshold_kernel(sc_ref, thr_ref, take_ref, tb_ref, *, topk):
    keys, lanes = sc_ref.shape

    def truncate_chunk(c, carry):
        off = pl.multiple_of(c * CK, CK)
        tb_ref[pl.ds(off, CK), :] = _truncate_bf16(sc_ref[pl.ds(off, CK), :])
        return carry

    lax.fori_loop(0, keys // CK, truncate_chunk, 0)
    thr, n_eq_take = _topk_threshold(sc_ref, tb_ref, keys // CK, keys // CK,
                                     jnp.full((1, lanes), topk, jnp.int32))
    thr_ref[...] = jnp.broadcast_to(thr, thr_ref.shape)
    take_ref[...] = jnp.broadcast_to(n_eq_take, take_ref.shape)


def _sample_threshold(scoresT, *, topk):
    keys, nq = scoresT.shape
    return pl.pallas_call(
        functools.partial(_sample_threshold_kernel, topk=topk),
        grid=(nq // LANE,),
        in_specs=[pl.BlockSpec((keys, LANE), lambda i: (0, i))],
        out_specs=[pl.BlockSpec((SUBLANE, LANE), lambda i: (0, i))] * 2,
        out_shape=[jax.ShapeDtypeStruct((SUBLANE, nq), F32)] * 2,
        scratch_shapes=[pltpu.VMEM((keys, LANE), BF16)],
        compiler_params=pltpu.CompilerParams(dimension_semantics=("arbitrary",),
                                             vmem_limit_bytes=VMEM_LIMIT),
        name="sample_threshold",
    )(scoresT)


def _sample_attn_kernel(pt_ref, q_ref, sc_ref, thr_ref, take_ref, kTnew_ref, vTnew_ref, *refs, nch):
    kpages = refs[:PAGES_PER_STEP]
    vpages = refs[PAGES_PER_STEP:2 * PAGES_PER_STEP]
    o_ref, m_ref, l_ref, acc_ref, seen_ref = refs[2 * PAGES_PER_STEP:]
    c = pl.program_id(1)

    @pl.when(c == 0)
    def _():
        m_ref[...] = jnp.full_like(m_ref, NEG_BIG)
        l_ref[...] = jnp.zeros_like(l_ref)
        acc_ref[...] = jnp.zeros_like(acc_ref)
        seen_ref[...] = jnp.zeros_like(seen_ref)

    thr = thr_ref[0][:, 0:1]
    take = take_ref[0][:, 0:1]

    def attend(kT, vT):
        width = kT.shape[1]
        sel, seen = _select_rows(sc_ref[0][:, 0:width], thr, take, seen_ref[:, 0:1],
                                 _tri(min(CK, width), lower=False))
        seen_ref[...] = jnp.broadcast_to(seen, seen_ref.shape)
        sel = jnp.tile(sel, (N_HEADS, 1))
        lg = jnp.where(sel, jnp.dot(q_ref[0], kT.astype(BF16), preferred_element_type=F32), -jnp.inf)
        m_old = m_ref[:, 0:1]
        m_new = jnp.maximum(m_old, jnp.max(lg, axis=1, keepdims=True))
        p = jnp.exp(lg - m_new)
        alpha = jnp.exp(m_old - m_new)
        l_new = alpha * l_ref[:, 0:1] + jnp.sum(p, axis=1, keepdims=True)
        pv = lax.dot_general(p.astype(BF16), vT.astype(BF16), (((1,), (1,)), ((), ())),
                             preferred_element_type=F32)
        acc_ref[...] = alpha * acc_ref[...] + pv
        m_ref[...] = jnp.broadcast_to(m_new, m_ref.shape)
        l_ref[...] = jnp.broadcast_to(l_new, l_ref.shape)

    def stack(pages):
        return jnp.concatenate([p[0, 0].reshape(E_A, p.shape[4]) for p in pages], axis=1)

    @pl.when(c < nch)
    def _():
        attend(stack(kpages), stack(vpages))

    @pl.when(c == nch)
    def _():
        attend(kTnew_ref[0], vTnew_ref[0])
        o = acc_ref[...] / l_ref[:, 0:1]
        o_ref[0] = jnp.concatenate(
            [o[h * SROWS:(h + 1) * SROWS, h * HEAD_DIM:(h + 1) * HEAD_DIM] for h in range(N_HEADS)], axis=1)


def _sample_attn(page_table, q_bd, scores, thr, take, kT_new, vT_new, cache_kT, cache_vT, layer):
    db, n_pages = page_table.shape
    nch = n_pages // PAGES_PER_STEP
    page = cache_kT.shape[4]
    sck = PAGES_PER_STEP * page
    hq = N_HEADS * SROWS

    def page_spec(i):
        return pl.BlockSpec(
            (1, 1, N_HEADS, HEAD_DIM, page),
            lambda b, c, pt: (layer, pt[b, jnp.minimum(c, nch - 1) * PAGES_PER_STEP + i], 0, 0, 0))

    per_b = lambda *shape: pl.BlockSpec((1,) + shape, lambda b, c, pt: (b,) + (0,) * len(shape))
    grid_spec = pltpu.PrefetchScalarGridSpec(
        num_scalar_prefetch=1, grid=(db, nch + 1),
        in_specs=[per_b(hq, E_A),
                  pl.BlockSpec((1, SROWS, sck), lambda b, c, pt: (b, 0, c)),
                  per_b(SROWS, LANE), per_b(SROWS, LANE),
                  per_b(E_A, page), per_b(E_A, page)]
                 + [page_spec(i) for i in range(PAGES_PER_STEP)] * 2,
        out_specs=per_b(SROWS, E_A),
        scratch_shapes=[pltpu.VMEM((hq, LANE), F32), pltpu.VMEM((hq, LANE), F32),
                        pltpu.VMEM((hq, E_A), F32), pltpu.VMEM((SROWS, LANE), F32)])
    return pl.pallas_call(
        functools.partial(_sample_attn_kernel, nch=nch),
        grid_spec=grid_spec,
        out_shape=jax.ShapeDtypeStruct((db, SROWS, E_A), F32),
        compiler_params=pltpu.CompilerParams(dimension_semantics=("arbitrary", "arbitrary"),
                                             vmem_limit_bytes=VMEM_LIMIT),
        name="sample_attn",
    )(page_table, q_bd, scores, thr, take, kT_new, vT_new,
      *([cache_kT] * PAGES_PER_STEP), *([cache_vT] * PAGES_PER_STEP))


def _post_kernel(x_ref, oa_ref, g0_ref, mbc_ref, wba_ref, wo_ref, wup_ref, wdn_ref,
                 gpost_ref, gfpre_ref, gfpost_ref, y_ref):
    m = g0_ref[...].astype(F32) * jnp.dot(oa_ref[...], wba_ref[...], preferred_element_type=F32) \
        + mbc_ref[...]
    y = jnp.dot(m.astype(BF16), wo_ref[...], preferred_element_type=F32)
    x1 = x_ref[...] + _rms(y, gpost_ref[...])
    a = jnp.maximum(jnp.dot(_rms(x1, gfpre_ref[...]).astype(BF16), wup_ref[...],
                            preferred_element_type=F32), 0.0)
    f = jnp.dot((a * a).astype(BF16), wdn_ref[...], preferred_element_type=F32)
    y_ref[...] = x1 + _rms(f, gfpost_ref[...])


def _post(x, oa, g0, mbc, wba, wo, wup, wdn, gpost, gfpre, gfpost, *, layer):
    m = x.shape[0]
    _const_spec = functools.partial(_layer_spec, layer)
    tm = min(POST_ROW_TILE, m)
    row = lambda n: pl.BlockSpec((tm, n), lambda i: (i, 0))
    return pl.pallas_call(
        _post_kernel, grid=(m // tm,),
        in_specs=[row(D_MODEL), row(E_A), row(D_MODEL), row(D_MODEL),
                  _const_spec((E_A, D_MODEL)), _const_spec((D_MODEL, D_MODEL)),
                  _const_spec((D_MODEL, D_FF)), _const_spec((D_FF, D_MODEL)),
                  _const_spec((1, D_MODEL)), _const_spec((1, D_MODEL)), _const_spec((1, D_MODEL))],
        out_specs=row(D_MODEL),
        out_shape=jax.ShapeDtypeStruct((m, D_MODEL), F32),
        compiler_params=pltpu.CompilerParams(dimension_semantics=("arbitrary",),
                                             vmem_limit_bytes=VMEM_LIMIT),
        name="post",
    )(x, oa, g0, mbc, wba, wo, wup, wdn, gpost, gfpre, gfpost)


def _rope_tables(pos):
    half = HEAD_DIM // 2
    inv = ROPE_THETA ** (-jnp.arange(half, dtype=F32) * (2.0 / HEAD_DIM))
    ang = pos[:, None] * inv[None, :]
    c, s = jnp.cos(ang), jnp.sin(ang)
    return jnp.tile(c, (1, LANE // half)), jnp.tile(jnp.concatenate([-s, s], axis=1), (1, LANE // HEAD_DIM))


def _heads_first(a, t):
    db = a.shape[0] // t
    a = a.reshape(db, t, N_HEADS, -1).transpose(0, 2, 1, 3)
    a = jnp.pad(a, ((0, 0), (0, 0), (0, SROWS - t), (0, 0)))
    return a.reshape(db, N_HEADS * SROWS, -1)


def _new_keys_T(a, db, t, page):
    a = jnp.pad(a.reshape(db, t, -1), ((0, 0), (0, page - t), (0, 0)))
    return a.transpose(0, 2, 1)


def kernel(x_prompt, x_sample, cache_k, cache_v, cache_idx_k, state_conv, page_table, norm_mix_pre, norm_mix_post, norm_ffn_pre, norm_ffn_post, w_in, gmlp_ln_g, gmlp_ln_b, gmlp_ws, gmlp_bs, conv_w, conv_b, w_br_attn, w_br_gmlp, w_br_conv, w_out, w_ff_up, w_ff_down):
    depth = w_in.shape[0]
    b, s, _ = x_prompt.shape
    db, t, _ = x_sample.shape
    page = cache_k.shape[2]
    n_pages = page_table.shape[1]
    past = n_pages * page
    ms = db * t
    assert s % POST_ROW_TILE == 0 and s % (2 * CK) == 0 and ms == CHUNK and CONV_W - 1 <= t <= SROWS
    assert page == LANE and n_pages % PAGES_PER_STEP == 0

    cache_kT = cache_k.transpose(0, 1, 3, 4, 2)
    cache_vT = cache_v.transpose(0, 1, 3, 4, 2)
    cache_iT = cache_idx_k.transpose(0, 1, 3, 2)

    w_in_p = jnp.concatenate(
        [w_in[:, :, :OFF_B], jnp.zeros((depth, D_MODEL, C_B - OFF_B), w_in.dtype), w_in[:, :, OFF_B:]],
        axis=2).astype(BF16)
    causal = jnp.tril(jnp.ones((CHUNK, CHUNK), bool))
    wm = jnp.where(causal[None, None], gmlp_ws, 0)
    wmix_p = wm.astype(BF16)
    bmix_p = jnp.repeat(jnp.swapaxes(gmlp_bs, 1, 2), E_B // GMLP_GROUPS, axis=2)
    eye = jnp.eye(db, dtype=wm.dtype)
    wmix_s = jnp.einsum('ab,lgts->lgatbs', eye, wm[:, :, :t, :t]).reshape(depth, GMLP_GROUPS, ms, ms)
    wmix_s = wmix_s.astype(BF16)
    bmix_s = jnp.tile(bmix_p[:, :t], (1, db, 1))
    wbb, wbc, wba = (w.astype(BF16) for w in (w_br_gmlp, w_br_conv, w_br_attn))
    wo, wup, wdn = (w.astype(BF16) for w in (w_out, w_ff_up, w_ff_down))
    row_stack = lambda a: a[:, None, :]
    common = (row_stack(norm_mix_pre), w_in_p)
    tail = (row_stack(gmlp_ln_g), row_stack(gmlp_ln_b))
    conv = (conv_w, row_stack(conv_b), wbb, wbc)
    post_w = (wba, wo, wup, wdn, row_stack(norm_mix_post), row_stack(norm_ffn_pre),
              row_stack(norm_ffn_post))
    zs = jnp.zeros((depth, db, 1, E_C), state_conv.dtype)
    st1 = jnp.concatenate([state_conv[:, :, 1:2]] + [zs] * (t - 1), axis=2)
    st2 = jnp.concatenate([state_conv[:, :, 0:1], state_conv[:, :, 1:2]] + [zs] * (t - 2), axis=2)
    st12 = jnp.concatenate([st1, st2], axis=3).reshape(depth, ms, 2 * E_C)

    cos_p, sin_p = _rope_tables(jnp.arange(s, dtype=F32))
    cos_s, sin_s = _rope_tables(jnp.tile(jnp.arange(t, dtype=F32) + past, db))
    topk_p = min(TOPK_MAX, s // 4)
    topk_s = min(TOPK_MAX, (past + t) // 4)
    head_eye = jnp.eye(N_HEADS, dtype=BF16)

    xp = x_prompt.reshape(b * s, D_MODEL)
    xs = x_sample.reshape(ms, D_MODEL)
    outs = [[] for _ in range(9)]
    for l in range(depth):
        (qT, qiT, wT, kb, kib, vTb, kT, vT, kiT, g0, mbc, ctail) = _proj(
            xp, *common, cos_p, sin_p, *tail, wmix_p, bmix_p, *conv, layer=l, sample=False, seq_len=s)
        oa = _attn_prompt(qT, qiT, wT, kib.reshape(b, s, IDX_DIM), kb.reshape(b, s, E_A), vTb, topk=topk_p)
        xp = _post(xp, oa.reshape(b * s, E_A), g0, mbc, *post_w, layer=l)
        outs[0].append(kT.reshape(b, N_HEADS, HEAD_DIM, s))
        outs[1].append(vT.reshape(b, N_HEADS, HEAD_DIM, s))
        outs[2].append(kiT)
        outs[3].append(ctail[:, SUBLANE - (CONV_W - 1):, :])
        (q_s, k_s, v_s, qi_s, kw_s, g0_s, mbc_s, cin_s, vn_s) = _proj(
            xs, *common, cos_s, sin_s, *tail, wmix_s, bmix_s, *conv, layer=l, sample=True, seq_len=t,
            st=st12)
        qi_h = _heads_first(qi_s, t)
        w_h = _heads_first(kw_s[:, IDX_DIM:IDX_DIM + IDX_HEADS][:, :, None], t)
        w_h = jnp.broadcast_to(w_h, w_h.shape[:2] + (LANE,))
        scores = _sample_scores(page_table, qi_h, w_h, _new_keys_T(kw_s[:, :IDX_DIM], db, t, page),
                                cache_iT, l, t_new=t)
        thr, take = _sample_threshold(scores.reshape(db * SROWS, -1).T, topk=topk_s)
        per_q = lambda a: jnp.broadcast_to(a[0].reshape(db, SROWS, 1), (db, SROWS, LANE))
        q4 = jnp.pad(q_s.reshape(db, t, N_HEADS, HEAD_DIM), ((0, 0), (0, SROWS - t), (0, 0), (0, 0)))
        q_bd = jnp.einsum('bqhd,hg->bhqgd', q4, head_eye).reshape(db, N_HEADS * SROWS, E_A)
        oa_s = _sample_attn(page_table, q_bd, scores, per_q(thr), per_q(take),
                            _new_keys_T(k_s, db, t, page), _new_keys_T(v_s, db, t, page),
                            cache_kT, cache_vT, l)
        oa_s = oa_s[:, :t].reshape(ms, E_A).astype(BF16)
        xs = _post(xs, oa_s, g0_s, mbc_s, *post_w, layer=l)
        outs[4].append(k_s.reshape(db, t, N_HEADS, HEAD_DIM))
        outs[5].append(v_s.reshape(db, t, N_HEADS, HEAD_DIM))
        outs[6].append(kw_s[:, :IDX_DIM].reshape(db, t, IDX_DIM))
        outs[7].append(cin_s.reshape(db, t, E_C)[:, t - (CONV_W - 1):])
        outs[8].append(vn_s.reshape(db, t, E_B))
    stacked = [jnp.stack(o) for o in outs]
    stacked[0] = stacked[0].transpose(0, 1, 4, 2, 3)
    stacked[1] = stacked[1].transpose(0, 1, 4, 2, 3)
    stacked[2] = stacked[2].transpose(0, 1, 3, 2)
    return (xp.reshape(b, s, D_MODEL), xs.reshape(db, t, D_MODEL)) + tuple(stacked)
```

```python
import functools
import math

import jax
import jax.numpy as jnp
from jax import lax
from jax.experimental import pallas as pl
from jax.experimental.pallas import tpu as pltpu

F32 = jnp.float32
BF16 = jnp.bfloat16

D_MODEL = 1024
N_HEADS = 8
HEAD_DIM = 64
E_A = N_HEADS * HEAD_DIM
IDX_HEADS = 8
IDX_DIM = 64
IDX_W_SCALE = 1.0 / math.sqrt(IDX_HEADS * IDX_DIM)
TOPK_MAX = 256
Q_BLOCK = 256
ROPE_THETA = 10000.0
CHUNK = 128
GMLP_GROUPS = 4
E_B = 512
E_C = 512
CONV_W = 3
D_FF = 4 * D_MODEL
EPS = 1e-6

OFF_KI = 3 * E_A + IDX_HEADS * IDX_DIM
OFF_B = OFF_KI + IDX_DIM + IDX_HEADS
LANE = 128
SUBLANE = 8
C_KW = OFF_KI
C_B = C_KW + LANE
C_C = C_B + 2 * E_B
C_G = C_C + 3 * E_C
N_PAD = C_G + 3 * D_MODEL

VMEM_LIMIT = 56 * 1024 * 1024
ROW_TILE = 256
POST_ROW_TILE = 512
NEG_BIG = -1e30
KEY_NEG_INF = -2139095041
KEY_POS_INF = 2139095040
CK = 256
COUNT_ELEMS = 64 * SUBLANE * LANE
ACC_ROWS = HEAD_DIM + 16
LOG2_E = math.log2(math.e)
PAGES_PER_STEP = 16
SROWS = SUBLANE


def _rms(x, g):
    return x * lax.rsqrt(jnp.mean(x * x, axis=-1, keepdims=True) + EPS) * g


def _rope128(z, cos, sin):
    lane = lax.broadcasted_iota(jnp.int32, z.shape, 1)
    partner = jnp.where((lane % HEAD_DIM) < HEAD_DIM // 2,
                        pltpu.roll(z, LANE - HEAD_DIM // 2, 1),
                        pltpu.roll(z, HEAD_DIM // 2, 1))
    return z * cos + partner * sin


def _rope(z, cos, sin):
    return jnp.concatenate(
        [_rope128(z[:, i:i + LANE], cos, sin) for i in range(0, z.shape[1], LANE)], axis=1)


def _proj_kernel(*refs, sample, tm, seq_rows, n_aliased):
    n_in = 13
    refs = refs[:n_in] + refs[n_in + n_aliased:]
    if sample:
        (x_ref, gpre_ref, w_ref, cos_ref, sin_ref, lng_ref, lnb_ref, wmix_ref, bmix_ref,
         cw_ref, cbias_ref, wbb_ref, wbc_ref, st_ref,
         q_ref, k_ref, v_ref, qi_ref, kw_ref, g0_ref, mbc_ref, cin_ref, vn_ref) = refs
    else:
        (x_ref, gpre_ref, w_ref, cos_ref, sin_ref, lng_ref, lnb_ref, wmix_ref, bmix_ref,
         cw_ref, cbias_ref, wbb_ref, wbc_ref,
         qT_ref, qiT_ref, wT_ref, kb_ref, kib_ref, vTb_ref, kT_ref, vT_ref, kiT_ref,
         g0_ref, mbc_ref, ctail_ref, carry_ref) = refs

    h = _rms(x_ref[...], gpre_ref[...]).astype(BF16)
    cos = cos_ref[...]
    sin = sin_ref[...]

    z = jnp.dot(h, w_ref[:, 0:C_KW], preferred_element_type=F32)
    q = _rope(z[:, 0:E_A], cos, sin) * (HEAD_DIM ** -0.5 * (1.0 if sample else LOG2_E))
    k = _rope(z[:, E_A:2 * E_A], cos, sin)
    v = z[:, 2 * E_A:3 * E_A]
    qi = _rope(z[:, 3 * E_A:4 * E_A], cos, sin)

    zk = jnp.dot(h, w_ref[:, C_KW:C_B], preferred_element_type=F32)
    lane = lax.broadcasted_iota(jnp.int32, zk.shape, 1)
    kw = jnp.where(lane < IDX_DIM, _rope128(zk, cos, sin), zk * IDX_W_SCALE)
    if sample:
        q_ref[...] = q.astype(BF16)
        k_ref[...] = k
        v_ref[...] = v
        qi_ref[...] = qi.astype(BF16)
        kw_ref[...] = kw
    else:
        qT_ref[0] = q.T.astype(BF16)
        qiT_ref[0] = qi.T.astype(BF16)
        kT = k.T
        kT_ref[0] = kT
        kb_ref[...] = k.astype(BF16)
        vT = v.T
        vT_ref[0] = vT
        vTb_ref[0] = vT.astype(BF16)
        kwT = kw.T
        kiT_ref[0] = kwT[0:IDX_DIM, :]
        wT_ref[0] = kwT[IDX_DIM:IDX_DIM + IDX_HEADS, :]
        kib_ref[...] = kw[:, 0:IDX_DIM].astype(BF16)

    gl = jax.nn.gelu(jnp.dot(h, w_ref[:, C_B:C_C], preferred_element_type=F32))
    u = gl[:, 0:E_B]
    vg = gl[:, E_B:2 * E_B]
    mu = jnp.mean(vg, axis=-1, keepdims=True)
    vc = vg - mu
    vn = vc * lax.rsqrt(jnp.mean(vc * vc, axis=-1, keepdims=True) + EPS) * lng_ref[...] + lnb_ref[...]
    if sample:
        vn_ref[...] = vn
    vnb = vn.astype(BF16)
    gw = E_B // GMLP_GROUPS
    rows = []
    for c in range(tm // CHUNK):
        cols = []
        for g in range(GMLP_GROUPS):
            cols.append(jnp.dot(wmix_ref[g], vnb[c * CHUNK:(c + 1) * CHUNK, g * gw:(g + 1) * gw],
                                preferred_element_type=F32))
        rows.append(jnp.concatenate(cols, axis=1) + bmix_ref[...])
    mix = rows[0] if len(rows) == 1 else jnp.concatenate(rows, axis=0)
    ob = (u * mix).astype(BF16)

    zc = jnp.dot(h, w_ref[:, C_C:C_G], preferred_element_type=F32)
    cb = zc[:, 0:E_C]
    cin = zc[:, E_C:2 * E_C] * zc[:, 2 * E_C:3 * E_C]
    row = lax.broadcasted_iota(jnp.int32, cin.shape, 0)
    if sample:
        cin_ref[...] = cin
        i_in_seq = row % seq_rows
        s1 = jnp.where(i_in_seq >= 1, pltpu.roll(cin, 1, 0), st_ref[:, 0:E_C])
        s2 = jnp.where(i_in_seq >= 2, pltpu.roll(cin, 2, 0), st_ref[:, E_C:2 * E_C])
    else:
        @pl.when(pl.program_id(0) % (seq_rows // tm) == 0)
        def _():
            carry_ref[...] = jnp.zeros_like(carry_ref)
        p2 = carry_ref[0:1, :]
        p1 = carry_ref[1:2, :]
        s1 = jnp.where(row >= 1, pltpu.roll(cin, 1, 0), p1)
        s2 = jnp.where(row >= 2, pltpu.roll(cin, 2, 0), jnp.where(row == 0, p2, p1))
        carry_ref[0:2, :] = cin[tm - 2:tm, :]
        ctail_ref[0] = cin[tm - SUBLANE:tm, :]
    y = s2 * cw_ref[0:1, :] + s1 * cw_ref[1:2, :] + cin * cw_ref[2:3, :] + cbias_ref[...]
    oc = (cb * y).astype(BF16)

    g0_ref[...] = jax.nn.sigmoid(
        jnp.dot(h, w_ref[:, C_G:C_G + D_MODEL], preferred_element_type=F32)).astype(BF16)
    g1 = jax.nn.sigmoid(jnp.dot(h, w_ref[:, C_G + D_MODEL:C_G + 2 * D_MODEL], preferred_element_type=F32))
    mbc = g1 * jnp.dot(ob, wbb_ref[...], preferred_element_type=F32)
    g2 = jax.nn.sigmoid(jnp.dot(h, w_ref[:, C_G + 2 * D_MODEL:N_PAD], preferred_element_type=F32))
    mbc_ref[...] = mbc + g2 * jnp.dot(oc, wbc_ref[...], preferred_element_type=F32)


def _layer_spec(layer, shape):
    nd = len(shape)
    return pl.BlockSpec((None,) + shape, lambda *_: (layer,) + (0,) * nd, pipeline_mode=pl.Buffered(1))


STACK_OUTPUTS = (6, 7, 8)


def _proj(x, gpre, w, cos, sin, lng, lnb, wmix, bmix, cw, cbias, wbb, wbc, *, layer, sample, seq_len,
          st=None, stacks=None):
    m = x.shape[0]
    aliases = {}
    _const_spec = functools.partial(_layer_spec, layer)
    tm = m if sample else ROW_TILE
    tps = max(seq_len // tm, 1)
    n_seq = m // seq_len
    row = lambda n: pl.BlockSpec((tm, n), lambda i: (i, 0))
    pos_spec = pl.BlockSpec((tm, LANE), (lambda i: (0, 0)) if sample else (lambda i: (i % tps, 0)))
    in_specs = [row(D_MODEL), _const_spec((1, D_MODEL)), _const_spec((D_MODEL, N_PAD)),
                pos_spec, pos_spec, _const_spec((1, E_B)), _const_spec((1, E_B)),
                _const_spec((GMLP_GROUPS, CHUNK, CHUNK)), _const_spec((CHUNK, E_B)),
                _const_spec((CONV_W, E_C)), _const_spec((1, E_C)),
                _const_spec((E_B, D_MODEL)), _const_spec((E_C, D_MODEL))]
    args = [x, gpre, w, cos, sin, lng, lnb, wmix, bmix, cw, cbias, wbb, wbc]
    sds = jax.ShapeDtypeStruct
    if sample:
        in_specs.append(pl.BlockSpec((None, tm, 2 * E_C), lambda i: (layer, i, 0)))
        args.append(st)
        out_shape = [sds((m, E_A), BF16), sds((m, E_A), F32), sds((m, E_A), F32), sds((m, E_A), BF16),
                     sds((m, LANE), F32), sds((m, D_MODEL), BF16), sds((m, D_MODEL), F32),
                     sds((m, E_C), F32), sds((m, E_B), F32)]
        out_specs = [row(E_A), row(E_A), row(E_A), row(E_A), row(LANE), row(D_MODEL), row(D_MODEL),
                     row(E_C), row(E_B)]
        scratch = []
    else:
        depth = w.shape[0]
        colT = lambda n: pl.BlockSpec((1, n, tm), lambda i: (i // tps, 0, i % tps))
        slabT = lambda n: pl.BlockSpec((None, 1, n, tm), lambda i: (layer, i // tps, 0, i % tps))
        out_shape = [sds((n_seq, E_A, seq_len), BF16), sds((n_seq, E_A, seq_len), BF16),
                     sds((n_seq, IDX_HEADS, seq_len), F32), sds((m, E_A), BF16), sds((m, IDX_DIM), BF16),
                     sds((n_seq, E_A, seq_len), BF16), sds((depth, n_seq, E_A, seq_len), F32),
                     sds((depth, n_seq, E_A, seq_len), F32), sds((depth, n_seq, IDX_DIM, seq_len), F32),
                     sds((m, D_MODEL), BF16), sds((m, D_MODEL), F32), sds((n_seq, SUBLANE, E_C), F32)]
        out_specs = [colT(E_A), colT(E_A), colT(IDX_HEADS), row(E_A), row(IDX_DIM),
                     colT(E_A), slabT(E_A), slabT(E_A), slabT(IDX_DIM),
                     row(D_MODEL), row(D_MODEL),
                     pl.BlockSpec((1, SUBLANE, E_C), lambda i: (i // tps, 0, 0))]
        scratch = [pltpu.VMEM((SUBLANE, E_C), F32)]
        if stacks is not None:
            n_in = len(args)
            in_specs += [pl.BlockSpec(memory_space=pl.ANY)] * len(stacks)
            args += list(stacks)
            aliases = {n_in + i: STACK_OUTPUTS[i] for i in range(len(stacks))}
    return pl.pallas_call(
        functools.partial(_proj_kernel, sample=sample, tm=tm, seq_rows=seq_len, n_aliased=len(aliases)),
        grid=(m // tm,), in_specs=in_specs, out_specs=out_specs, out_shape=out_shape,
        scratch_shapes=scratch, input_output_aliases=aliases,
        compiler_params=pltpu.CompilerParams(dimension_semantics=("arbitrary",),
                                             vmem_limit_bytes=VMEM_LIMIT),
        name="proj_sample" if sample else "proj_prompt",
    )(*args)


def _key_to_f32(key):
    bits = key ^ ((key >> 31) & jnp.int32(0x7FFFFFFF))
    return lax.bitcast_convert_type(bits, F32)


def _truncate_bf16(x):
    bits = lax.bitcast_convert_type(x, jnp.int32) & jnp.int32(-65536)
    return lax.bitcast_convert_type(bits, F32).astype(BF16)


def _topk_threshold(sc_ref, tb_ref, n_chunks, n_chunks_packed, k_row):
    lanes = sc_ref.shape[1]
    k_f = k_row.astype(F32)
    n_chunks_f32 = n_chunks

    def count(ref, pred_fn, dtype):
        n_chunks = n_chunks_packed if dtype == BF16 else n_chunks_f32
        count_rows = COUNT_ELEMS // lanes * (4 // jnp.dtype(dtype).itemsize)
        acc_rows = count_rows // 8
        one, zero = jnp.ones((), dtype), jnp.zeros((), dtype)

        def body(c, acc):
            off = pl.multiple_of(c * count_rows, count_rows)
            hit = jnp.where(pred_fn(ref[pl.ds(off, count_rows), :]), one, zero)
            p = [hit[i * acc_rows:(i + 1) * acc_rows] for i in range(8)]
            return acc + (((p[0] + p[1]) + (p[2] + p[3])) + ((p[4] + p[5]) + (p[6] + p[7])))
        acc = lax.fori_loop(0, n_chunks * CK // count_rows, body, jnp.zeros((acc_rows, lanes), dtype))
        return jnp.sum(acc.astype(F32), axis=0, keepdims=True)

    def bisect_high(_, lohi):
        lo, hi = lohi
        mid = (lo + hi) >> 1
        rep = (mid << 16) | jnp.where(mid < 0, jnp.int32(0xFFFF), jnp.int32(0))
        midb = _key_to_f32(rep).astype(BF16)
        ge = count(tb_ref, lambda t: t >= midb, BF16) >= k_f
        return jnp.where(ge, mid, lo), jnp.where(ge, hi, mid)

    def bisect_low(_, lohi):
        lo, hi = lohi
        mid = (lo >> 1) + (hi >> 1) + (lo & hi & 1)
        midf = _key_to_f32(mid)
        ge = count(sc_ref, lambda s: s >= midf, F32) >= k_f
        return jnp.where(ge, mid, lo), jnp.where(ge, hi, mid)

    lo0 = jnp.full((1, lanes), KEY_NEG_INF >> 16, jnp.int32)
    hi0 = jnp.full((1, lanes), (KEY_POS_INF >> 16) + 1, jnp.int32)
    hi_half, _ = lax.fori_loop(0, 16, bisect_high, (lo0, hi0))
    lo, _ = lax.fori_loop(0, 16, bisect_low, (hi_half << 16, (hi_half << 16) + 65536))
    thr = _key_to_f32(lo)
    n_eq_take = k_f - count(sc_ref, lambda s: s > thr, F32)
    return thr, n_eq_take


def _select_cols(s, thr, n_eq_take, eq_seen, tril):
    eq = s == thr
    prefix = jnp.dot(tril, jnp.where(eq, 1.0, 0.0).astype(BF16), preferred_element_type=F32) + eq_seen
    sel = (s > thr) | (eq & (prefix <= n_eq_take))
    return sel, prefix[s.shape[0] - 1:s.shape[0], :]


def _select_rows(s, thr, n_eq_take, eq_seen, triu):
    blk = triu.shape[0]
    eq = s == thr
    eqf = jnp.where(eq, 1.0, 0.0)
    prefixes = []
    for i in range(s.shape[1] // blk):
        e = eqf[:, i * blk:(i + 1) * blk]
        prefixes.append(jnp.dot(e.astype(BF16), triu, preferred_element_type=F32) + eq_seen)
        eq_seen = eq_seen + jnp.sum(e, axis=1, keepdims=True)
    prefix = prefixes[0] if len(prefixes) == 1 else jnp.concatenate(prefixes, axis=1)
    return (s > thr) | (eq & (prefix <= n_eq_take)), eq_seen


def _tri(n, lower):
    r = lax.broadcasted_iota(jnp.int32, (n, n), 0)
    c = lax.broadcasted_iota(jnp.int32, (n, n), 1)
    return jnp.where((c <= r) if lower else (r <= c), 1.0, 0.0).astype(BF16)


def _attn_prompt_kernel(qT_ref, qiT_ref, wT_ref, ki_ref, k_ref, vT_ref, o_ref, sc_ref, tb_ref, acc_ref,
                        lga_ref, lgb_ref, wq_ref, tri_ref, *, topk):
    j = pl.program_id(1)
    nck = (j * Q_BLOCK) // CK + 1
    qiT = qiT_ref[0]
    w_idx = jnp.concatenate([qiT[h * IDX_DIM:(h + 1) * IDX_DIM, :] for h in range(IDX_HEADS)], axis=1)
    wT = wT_ref[0]
    q_pos = j * Q_BLOCK + lax.broadcasted_iota(jnp.int32, (1, Q_BLOCK), 1)

    def score_chunk(c, carry):
        off = pl.multiple_of(c * CK, CK)
        d = jnp.dot(ki_ref[0, pl.ds(off, CK), :], w_idx, preferred_element_type=F32)
        acc = jnp.maximum(d[:, 0:Q_BLOCK], 0.0) * wT[0:1, :]
        for h in range(1, IDX_HEADS):
            acc = acc + jnp.maximum(d[:, h * Q_BLOCK:(h + 1) * Q_BLOCK], 0.0) * wT[h:h + 1, :]
        key_pos = off + lax.broadcasted_iota(jnp.int32, (CK, 1), 0)
        acc = jnp.where(key_pos <= q_pos, acc, -jnp.inf)
        sc_ref[pl.ds(off, CK), :] = acc
        tb_ref[pl.ds(off, CK), :] = _truncate_bf16(acc)
        return carry

    lax.fori_loop(0, nck, score_chunk, 0)

    @pl.when(nck * CK < sc_ref.shape[0])
    def _():
        off = pl.multiple_of(nck * CK, CK)
        sc_ref[pl.ds(off, CK), :] = jnp.full((CK, Q_BLOCK), -jnp.inf, F32)
        tb_ref[pl.ds(off, CK), :] = jnp.full((CK, Q_BLOCK), -jnp.inf, BF16)

    thr, n_eq_take = _topk_threshold(sc_ref, tb_ref, nck, 2 * ((nck + 1) // 2),
                                     jnp.minimum(q_pos + 1, topk))

    qT = qT_ref[0].astype(F32)
    upper = lax.broadcasted_iota(jnp.int32, (LANE, Q_BLOCK), 0) < HEAD_DIM
    for h in range(N_HEADS):
        slab = qT[(h // 2) * LANE:(h // 2 + 1) * LANE, :]
        wq_ref[h] = jnp.where(upper if h % 2 == 0 else ~upper, slab, 0.0).astype(BF16)
    tri_ref[...] = _tri(CK, lower=True)
    acc_ref[...] = jnp.zeros_like(acc_ref)

    last_chunk = k_ref.shape[1] // CK - 1

    def chunk_offset(c):
        return pl.multiple_of(jnp.minimum(c, last_chunk) * CK, CK)

    def masked_logits(c, eq_seen, lg_ref):
        off = chunk_offset(c)
        sel, eq_seen = _select_cols(sc_ref[pl.ds(off, CK), :], thr, n_eq_take, eq_seen, tri_ref[...])
        sc_ref[pl.ds(off, CK), :] = jnp.where(sel, 0.0, -jnp.inf)
        col_max = []
        for h in range(N_HEADS):
            kh = k_ref[0, pl.ds(off, CK), (h // 2) * LANE:(h // 2 + 1) * LANE]
            lg = jnp.dot(kh, wq_ref[h], preferred_element_type=F32) + sc_ref[pl.ds(off, CK), :]
            lg_ref[h] = lg
            col_max.append(jnp.max(lg, axis=0, keepdims=True))
        return eq_seen, jnp.concatenate(col_max, axis=0)

    def softmax_update(c, lg_ref, lg_next_ref, carry):
        eq_seen, m_all, chunk_max = carry
        m_new = jnp.maximum(m_all, chunk_max)
        alpha = jnp.exp2(m_all - m_new)
        if lg_next_ref is not None:
            eq_seen, chunk_max = masked_logits(c + 1, eq_seen, lg_next_ref)
        off = chunk_offset(c)
        ones_rows = jnp.ones((ACC_ROWS - HEAD_DIM, CK), BF16)
        for h in range(N_HEADS):
            p = jnp.exp2(lg_ref[h] - m_new[h:h + 1, :])
            lhs = jnp.concatenate([vT_ref[0, h * HEAD_DIM:(h + 1) * HEAD_DIM, pl.ds(off, CK)], ones_rows],
                                  axis=0)
            acc_ref[h] = alpha[h:h + 1, :] * acc_ref[h] + jnp.dot(lhs, p.astype(BF16),
                                                                  preferred_element_type=F32)
        return eq_seen, m_new, chunk_max

    def attend_pair(i, carry):
        carry = softmax_update(2 * i, lga_ref, lgb_ref, carry)
        return softmax_update(2 * i + 1, lgb_ref, lga_ref, carry)

    eq_seen0, chunk_max0 = masked_logits(0, jnp.zeros((1, Q_BLOCK), F32), lga_ref)
    carry = lax.fori_loop(0, nck // 2, attend_pair,
                          (eq_seen0, jnp.full((N_HEADS, Q_BLOCK), NEG_BIG, F32), chunk_max0))

    @pl.when(nck % 2 == 1)
    def _():
        softmax_update(nck - 1, lga_ref, None, carry)

    o_ref[0] = jnp.concatenate(
        [acc_ref[h][0:HEAD_DIM, :] / acc_ref[h][HEAD_DIM:HEAD_DIM + 1, :] for h in range(N_HEADS)],
        axis=0).T.astype(BF16)


def _attn_prompt(qT, qiT, wT, kib, kb, vTb, *, topk):
    b, _, s = qT.shape
    colT = lambda n: pl.BlockSpec((1, n, Q_BLOCK), lambda i, j: (i, 0, j))
    return pl.pallas_call(
        functools.partial(_attn_prompt_kernel, topk=topk),
        grid=(b, s // Q_BLOCK),
        in_specs=[colT(E_A), colT(E_A), colT(IDX_HEADS),
                  pl.BlockSpec((1, s, IDX_DIM), lambda i, j: (i, 0, 0)),
                  pl.BlockSpec((1, s, E_A), lambda i, j: (i, 0, 0)),
                  pl.BlockSpec((1, E_A, s), lambda i, j: (i, 0, 0))],
        out_specs=pl.BlockSpec((1, Q_BLOCK, E_A), lambda i, j: (i, j, 0)),
        out_shape=jax.ShapeDtypeStruct((b, s, E_A), BF16),
        scratch_shapes=[pltpu.VMEM((s, Q_BLOCK), F32), pltpu.VMEM((s, Q_BLOCK), BF16),
                        pltpu.VMEM((N_HEADS, ACC_ROWS, Q_BLOCK), F32),
                        pltpu.VMEM((N_HEADS, CK, Q_BLOCK), F32), pltpu.VMEM((N_HEADS, CK, Q_BLOCK), F32),
                        pltpu.VMEM((N_HEADS, LANE, Q_BLOCK), BF16), pltpu.VMEM((CK, CK), BF16)],
        compiler_params=pltpu.CompilerParams(dimension_semantics=("arbitrary", "arbitrary"),
                                             vmem_limit_bytes=VMEM_LIMIT),
        name="attn_prompt",
    )(qT, qiT, wT, kib, kb, vTb)


def _sample_scores_kernel(pt_ref, qi_ref, w_ref, kinew_ref, *refs, n_pages, t_new):
    pages = refs[:n_pages]
    out_ref = refs[n_pages]
    qi = qi_ref[0]
    w = w_ref[0][:, 0:1]

    def scores(keysT):
        t = jnp.maximum(jnp.dot(qi, keysT.astype(BF16), preferred_element_type=F32), 0.0) * w
        acc = t[0:SROWS]
        for h in range(1, IDX_HEADS):
            acc = acc + t[h * SROWS:(h + 1) * SROWS]
        return acc

    past = scores(jnp.concatenate([p[0, 0] for p in pages], axis=1))
    new = scores(kinew_ref[0])
    qrow = lax.broadcasted_iota(jnp.int32, new.shape, 0)
    kcol = lax.broadcasted_iota(jnp.int32, new.shape, 1)
    new = jnp.where(kcol <= jnp.minimum(qrow, t_new - 1), new, -jnp.inf)
    pad = jnp.full((SROWS, out_ref.shape[2] - past.shape[1] - new.shape[1]), -jnp.inf, F32)
    out_ref[0] = jnp.concatenate([past, new, pad], axis=1)


def _sample_scores(page_table, qi_h, w_h, kiT_new, cache_iT, layer, *, t_new):
    db, n_pages = page_table.shape
    page = cache_iT.shape[3]
    width = (n_pages // PAGES_PER_STEP + 1) * PAGES_PER_STEP * page

    def page_spec(i):
        return pl.BlockSpec((1, 1, IDX_DIM, page), lambda b, pt: (layer, pt[b, i], 0, 0))

    per_b = lambda *shape: pl.BlockSpec((1,) + shape, lambda b, pt: (b,) + (0,) * len(shape))
    grid_spec = pltpu.PrefetchScalarGridSpec(
        num_scalar_prefetch=1, grid=(db,),
        in_specs=[per_b(IDX_HEADS * SROWS, IDX_DIM), per_b(IDX_HEADS * SROWS, LANE), per_b(IDX_DIM, page)]
                 + [page_spec(i) for i in range(n_pages)],
        out_specs=per_b(SROWS, width))
    return pl.pallas_call(
        functools.partial(_sample_scores_kernel, n_pages=n_pages, t_new=t_new),
        grid_spec=grid_spec,
        out_shape=jax.ShapeDtypeStruct((db, SROWS, width), F32),
        compiler_params=pltpu.CompilerParams(dimension_semantics=("arbitrary",),
                                             vmem_limit_bytes=VMEM_LIMIT),
        name="sample_scores",
    )(page_table, qi_h, w_h, kiT_new, *([cache_iT] * n_pages))


def _sample_threshold_kernel(sc_ref, thr_ref, take_ref, tb_ref, *, topk):
    keys, lanes = sc_ref.shape

    def truncate_chunk(c, carry):
        off = pl.multiple_of(c * CK, CK)
        tb_ref[pl.ds(off, CK), :] = _truncate_bf16(sc_ref[pl.ds(off, CK), :])
        return carry

    lax.fori_loop(0, keys // CK, truncate_chunk, 0)
    thr, n_eq_take = _topk_threshold(sc_ref, tb_ref, keys // CK, keys // CK,
                                     jnp.full((1, lanes), topk, jnp.int32))
    thr_ref[...] = jnp.broadcast_to(thr, thr_ref.shape)
    take_ref[...] = jnp.broadcast_to(n_eq_take, take_ref.shape)


def _sample_threshold(scoresT, *, topk):
    keys, nq = scoresT.shape
    return pl.pallas_call(
        functools.partial(_sample_threshold_kernel, topk=topk),
        grid=(nq // LANE,),
        in_specs=[pl.BlockSpec((keys, LANE), lambda i: (0, i))],
        out_specs=[pl.BlockSpec((SUBLANE, LANE), lambda i: (0, i))] * 2,
        out_shape=[jax.ShapeDtypeStruct((SUBLANE, nq), F32)] * 2,
        scratch_shapes=[pltpu.VMEM((keys, LANE), BF16)],
        compiler_params=pltpu.CompilerParams(dimension_semantics=("arbitrary",),
                                             vmem_limit_bytes=VMEM_LIMIT),
        name="sample_threshold",
    )(scoresT)


def _sample_attn_kernel(pt_ref, q_ref, sc_ref, thr_ref, take_ref, kTnew_ref, vTnew_ref, *refs, nch):
    kpages = refs[:PAGES_PER_STEP]
    vpages = refs[PAGES_PER_STEP:2 * PAGES_PER_STEP]
    o_ref, m_ref, l_ref, acc_ref, seen_ref = refs[2 * PAGES_PER_STEP:]
    c = pl.program_id(1)

    @pl.when(c == 0)
    def _():
        m_ref[...] = jnp.full_like(m_ref, NEG_BIG)
        l_ref[...] = jnp.zeros_like(l_ref)
        acc_ref[...] = jnp.zeros_like(acc_ref)
        seen_ref[...] = jnp.zeros_like(seen_ref)

    thr = thr_ref[0][:, 0:1]
    take = take_ref[0][:, 0:1]

    def attend(kT, vT):
        width = kT.shape[1]
        sel, seen = _select_rows(sc_ref[0][:, 0:width], thr, take, seen_ref[:, 0:1],
                                 _tri(min(CK, width), lower=False))
        seen_ref[...] = jnp.broadcast_to(seen, seen_ref.shape)
        sel = jnp.tile(sel, (N_HEADS, 1))
        lg = jnp.where(sel, jnp.dot(q_ref[0], kT.astype(BF16), preferred_element_type=F32), -jnp.inf)
        m_old = m_ref[:, 0:1]
        m_new = jnp.maximum(m_old, jnp.max(lg, axis=1, keepdims=True))
        p = jnp.exp(lg - m_new)
        alpha = jnp.exp(m_old - m_new)
        l_new = alpha * l_ref[:, 0:1] + jnp.sum(p, axis=1, keepdims=True)
        pv = lax.dot_general(p.astype(BF16), vT.astype(BF16), (((1,), (1,)), ((), ())),
                             preferred_element_type=F32)
        acc_ref[...] = alpha * acc_ref[...] + pv
        m_ref[...] = jnp.broadcast_to(m_new, m_ref.shape)
        l_ref[...] = jnp.broadcast_to(l_new, l_ref.shape)

    def stack(pages):
        return jnp.concatenate([p[0, 0].reshape(E_A, p.shape[4]) for p in pages], axis=1)

    @pl.when(c < nch)
    def _():
        attend(stack(kpages), stack(vpages))

    @pl.when(c == nch)
    def _():
        attend(kTnew_ref[0], vTnew_ref[0])
        o = acc_ref[...] / l_ref[:, 0:1]
        o_ref[0] = jnp.concatenate(
            [o[h * SROWS:(h + 1) * SROWS, h * HEAD_DIM:(h + 1) * HEAD_DIM] for h in range(N_HEADS)], axis=1)


def _sample_attn(page_table, q_bd, scores, thr, take, kT_new, vT_new, cache_kT, cache_vT, layer):
    db, n_pages = page_table.shape
    nch = n_pages // PAGES_PER_STEP
    page = cache_kT.shape[4]
    sck = PAGES_PER_STEP * page
    hq = N_HEADS * SROWS

    def page_spec(i):
        return pl.BlockSpec(
            (1, 1, N_HEADS, HEAD_DIM, page),
            lambda b, c, pt: (layer, pt[b, jnp.minimum(c, nch - 1) * PAGES_PER_STEP + i], 0, 0, 0))

    per_b = lambda *shape: pl.BlockSpec((1,) + shape, lambda b, c, pt: (b,) + (0,) * len(shape))
    grid_spec = pltpu.PrefetchScalarGridSpec(
        num_scalar_prefetch=1, grid=(db, nch + 1),
        in_specs=[per_b(hq, E_A),
                  pl.BlockSpec((1, SROWS, sck), lambda b, c, pt: (b, 0, c)),
                  per_b(SROWS, LANE), per_b(SROWS, LANE),
                  per_b(E_A, page), per_b(E_A, page)]
                 + [page_spec(i) for i in range(PAGES_PER_STEP)] * 2,
        out_specs=per_b(SROWS, E_A),
        scratch_shapes=[pltpu.VMEM((hq, LANE), F32), pltpu.VMEM((hq, LANE), F32),
                        pltpu.VMEM((hq, E_A), F32), pltpu.VMEM((SROWS, LANE), F32)])
    return pl.pallas_call(
        functools.partial(_sample_attn_kernel, nch=nch),
        grid_spec=grid_spec,
        out_shape=jax.ShapeDtypeStruct((db, SROWS, E_A), F32),
        compiler_params=pltpu.CompilerParams(dimension_semantics=("arbitrary", "arbitrary"),
                                             vmem_limit_bytes=VMEM_LIMIT),
        name="sample_attn",
    )(page_table, q_bd, scores, thr, take, kT_new, vT_new,
      *([cache_kT] * PAGES_PER_STEP), *([cache_vT] * PAGES_PER_STEP))


def _post_kernel(x_ref, oa_ref, g0_ref, mbc_ref, wba_ref, wo_ref, wup_ref, wdn_ref,
                 gpost_ref, gfpre_ref, gfpost_ref, y_ref):
    m = g0_ref[...].astype(F32) * jnp.dot(oa_ref[...], wba_ref[...], preferred_element_type=F32) \
        + mbc_ref[...]
    y = jnp.dot(m.astype(BF16), wo_ref[...], preferred_element_type=F32)
    x1 = x_ref[...] + _rms(y, gpost_ref[...])
    a = jnp.maximum(jnp.dot(_rms(x1, gfpre_ref[...]).astype(BF16), wup_ref[...],
                            preferred_element_type=F32), 0.0)
    f = jnp.dot((a * a).astype(BF16), wdn_ref[...], preferred_element_type=F32)
    y_ref[...] = x1 + _rms(f, gfpost_ref[...])


def _post(x, oa, g0, mbc, wba, wo, wup, wdn, gpost, gfpre, gfpost, *, layer):
    m = x.shape[0]
    _const_spec = functools.partial(_layer_spec, layer)
    tm = min(POST_ROW_TILE, m)
    row = lambda n: pl.BlockSpec((tm, n), lambda i: (i, 0))
    return pl.pallas_call(
        _post_kernel, grid=(m // tm,),
        in_specs=[row(D_MODEL), row(E_A), row(D_MODEL), row(D_MODEL),
                  _const_spec((E_A, D_MODEL)), _const_spec((D_MODEL, D_MODEL)),
                  _const_spec((D_MODEL, D_FF)), _const_spec((D_FF, D_MODEL)),
                  _const_spec((1, D_MODEL)), _const_spec((1, D_MODEL)), _const_spec((1, D_MODEL))],
        out_specs=row(D_MODEL),
        out_shape=jax.ShapeDtypeStruct((m, D_MODEL), F32),
        compiler_params=pltpu.CompilerParams(dimension_semantics=("arbitrary",),
                                             vmem_limit_bytes=VMEM_LIMIT),
        name="post",
    )(x, oa, g0, mbc, wba, wo, wup, wdn, gpost, gfpre, gfpost)


def _rope_tables(pos):
    half = HEAD_DIM // 2
    inv = ROPE_THETA ** (-jnp.arange(half, dtype=F32) * (2.0 / HEAD_DIM))
    ang = pos[:, None] * inv[None, :]
    c, s = jnp.cos(ang), jnp.sin(ang)
    return jnp.tile(c, (1, LANE // half)), jnp.tile(jnp.concatenate([-s, s], axis=1), (1, LANE // HEAD_DIM))


def _heads_first(a, t):
    db = a.shape[0] // t
    a = a.reshape(db, t, N_HEADS, -1).transpose(0, 2, 1, 3)
    a = jnp.pad(a, ((0, 0), (0, 0), (0, SROWS - t), (0, 0)))
    return a.reshape(db, N_HEADS * SROWS, -1)


def _new_keys_T(a, db, t, page):
    a = jnp.pad(a.reshape(db, t, -1), ((0, 0), (0, page - t), (0, 0)))
    return a.transpose(0, 2, 1)


def kernel(x_prompt, x_sample, cache_k, cache_v, cache_idx_k, state_conv, page_table, norm_mix_pre, norm_mix_post, norm_ffn_pre, norm_ffn_post, w_in, gmlp_ln_g, gmlp_ln_b, gmlp_ws, gmlp_bs, conv_w, conv_b, w_br_attn, w_br_gmlp, w_br_conv, w_out, w_ff_up, w_ff_down):
    depth = w_in.shape[0]
    b, s, _ = x_prompt.shape
    db, t, _ = x_sample.shape
    page = cache_k.shape[2]
    n_pages = page_table.shape[1]
    past = n_pages * page
    ms = db * t
    assert s % POST_ROW_TILE == 0 and s % (2 * CK) == 0 and ms == CHUNK and CONV_W - 1 <= t <= SROWS
    assert page == LANE and n_pages % PAGES_PER_STEP == 0

    cache_kT = cache_k.transpose(0, 1, 3, 4, 2)
    cache_vT = cache_v.transpose(0, 1, 3, 4, 2)
    cache_iT = cache_idx_k.transpose(0, 1, 3, 2)

    w_in_p = jnp.concatenate(
        [w_in[:, :, :OFF_B], jnp.zeros((depth, D_MODEL, C_B - OFF_B), w_in.dtype), w_in[:, :, OFF_B:]],
        axis=2).astype(BF16)
    causal = jnp.tril(jnp.ones((CHUNK, CHUNK), bool))
    wm = jnp.where(causal[None, None], gmlp_ws, 0)
    wmix_p = wm.astype(BF16)
    bmix_p = jnp.repeat(jnp.swapaxes(gmlp_bs, 1, 2), E_B // GMLP_GROUPS, axis=2)
    eye = jnp.eye(db, dtype=wm.dtype)
    wmix_s = jnp.einsum('ab,lgts->lgatbs', eye, wm[:, :, :t, :t]).reshape(depth, GMLP_GROUPS, ms, ms)
    wmix_s = wmix_s.astype(BF16)
    bmix_s = jnp.tile(bmix_p[:, :t], (1, db, 1))
    wbb, wbc, wba = (w.astype(BF16) for w in (w_br_gmlp, w_br_conv, w_br_attn))
    wo, wup, wdn = (w.astype(BF16) for w in (w_out, w_ff_up, w_ff_down))
    row_stack = lambda a: a[:, None, :]
    common = (row_stack(norm_mix_pre), w_in_p)
    tail = (row_stack(gmlp_ln_g), row_stack(gmlp_ln_b))
    conv = (conv_w, row_stack(conv_b), wbb, wbc)
    post_w = (wba, wo, wup, wdn, row_stack(norm_mix_post), row_stack(norm_ffn_pre),
              row_stack(norm_ffn_post))
    zs = jnp.zeros((depth, db, 1, E_C), state_conv.dtype)
    st1 = jnp.concatenate([state_conv[:, :, 1:2]] + [zs] * (t - 1), axis=2)
    st2 = jnp.concatenate([state_conv[:, :, 0:1], state_conv[:, :, 1:2]] + [zs] * (t - 2), axis=2)
    st12 = jnp.concatenate([st1, st2], axis=3).reshape(depth, ms, 2 * E_C)

    cos_p, sin_p = _rope_tables(jnp.arange(s, dtype=F32))
    cos_s, sin_s = _rope_tables(jnp.tile(jnp.arange(t, dtype=F32) + past, db))
    topk_p = min(TOPK_MAX, s // 4)
    topk_s = min(TOPK_MAX, (past + t) // 4)
    head_eye = jnp.eye(N_HEADS, dtype=BF16)

    xp = x_prompt.reshape(b * s, D_MODEL)
    xs = x_sample.reshape(ms, D_MODEL)
    outs = [[] for _ in range(9)]
    stacks = None
    for l in range(depth):
        (qT, qiT, wT, kb, kib, vTb, kT_all, vT_all, kiT_all, g0, mbc, ctail) = _proj(
            xp, *common, cos_p, sin_p, *tail, wmix_p, bmix_p, *conv, layer=l, sample=False, seq_len=s,
            stacks=stacks)
        stacks = (kT_all, vT_all, kiT_all)
        oa = _attn_prompt(qT, qiT, wT, kib.reshape(b, s, IDX_DIM), kb.reshape(b, s, E_A), vTb, topk=topk_p)
        xp = _post(xp, oa.reshape(b * s, E_A), g0, mbc, *post_w, layer=l)
        outs[3].append(ctail[:, SUBLANE - (CONV_W - 1):, :])
        (q_s, k_s, v_s, qi_s, kw_s, g0_s, mbc_s, cin_s, vn_s) = _proj(
            xs, *common, cos_s, sin_s, *tail, wmix_s, bmix_s, *conv, layer=l, sample=True, seq_len=t,
            st=st12)
        qi_h = _heads_first(qi_s, t)
        w_h = _heads_first(kw_s[:, IDX_DIM:IDX_DIM + IDX_HEADS][:, :, None], t)
        w_h = jnp.broadcast_to(w_h, w_h.shape[:2] + (LANE,))
        scores = _sample_scores(page_table, qi_h, w_h, _new_keys_T(kw_s[:, :IDX_DIM], db, t, page),
                                cache_iT, l, t_new=t)
        thr, take = _sample_threshold(scores[:, :t].reshape(ms, -1).T, topk=topk_s)
        per_q = lambda a, fill: jnp.broadcast_to(
            jnp.pad(a[0].reshape(db, t, 1), ((0, 0), (0, SROWS - t), (0, 0)), constant_values=fill),
            (db, SROWS, LANE))
        q4 = jnp.pad(q_s.reshape(db, t, N_HEADS, HEAD_DIM), ((0, 0), (0, SROWS - t), (0, 0), (0, 0)))
        q_bd = jnp.einsum('bqhd,hg->bhqgd', q4, head_eye).reshape(db, N_HEADS * SROWS, E_A)
        oa_s = _sample_attn(page_table, q_bd, scores, per_q(thr, NEG_BIG), per_q(take, 0.0),
                            _new_keys_T(k_s, db, t, page), _new_keys_T(v_s, db, t, page),
                            cache_kT, cache_vT, l)
        oa_s = oa_s[:, :t].reshape(ms, E_A).astype(BF16)
        xs = _post(xs, oa_s, g0_s, mbc_s, *post_w, layer=l)
        outs[4].append(k_s.reshape(db, t, N_HEADS, HEAD_DIM))
        outs[5].append(v_s.reshape(db, t, N_HEADS, HEAD_DIM))
        outs[6].append(kw_s[:, :IDX_DIM].reshape(db, t, IDX_DIM))
        outs[7].append(cin_s.reshape(db, t, E_C)[:, t - (CONV_W - 1):])
        outs[8].append(vn_s.reshape(db, t, E_B))
    kT_all, vT_all, kiT_all = stacks
    stacked = [None] * 3 + [jnp.stack(o) for o in outs[3:]]
    per_head = (depth, b, N_HEADS, HEAD_DIM, s)
    stacked[0] = kT_all.reshape(per_head).transpose(0, 1, 4, 2, 3)
    stacked[1] = vT_all.reshape(per_head).transpose(0, 1, 4, 2, 3)
    stacked[2] = kiT_all.transpose(0, 1, 3, 2)
    return (xp.reshape(b, s, D_MODEL), xs.reshape(db, t, D_MODEL)) + tuple(stacked)
```

```python
import functools
import math

import jax
import jax.numpy as jnp
from jax import lax
from jax.experimental import pallas as pl
from jax.experimental.pallas import tpu as pltpu

F32 = jnp.float32
BF16 = jnp.bfloat16

D_MODEL = 1024
N_HEADS = 8
HEAD_DIM = 64
E_A = N_HEADS * HEAD_DIM
IDX_HEADS = 8
IDX_DIM = 64
IDX_W_SCALE = 1.0 / math.sqrt(IDX_HEADS * IDX_DIM)
TOPK_MAX = 256
Q_BLOCK = 256
ROPE_THETA = 10000.0
CHUNK = 128
GMLP_GROUPS = 4
E_B = 512
E_C = 512
CONV_W = 3
D_FF = 4 * D_MODEL
EPS = 1e-6

OFF_KI = 3 * E_A + IDX_HEADS * IDX_DIM
OFF_B = OFF_KI + IDX_DIM + IDX_HEADS
LANE = 128
SUBLANE = 8
C_KW = OFF_KI
C_B = C_KW + LANE
C_C = C_B + 2 * E_B
C_G = C_C + 3 * E_C
N_PAD = C_G + 3 * D_MODEL

VMEM_LIMIT = 56 * 1024 * 1024
ROW_TILE = 256
POST_ROW_TILE = 512
NEG_BIG = -1e30
KEY_NEG_INF = -2139095041
KEY_POS_INF = 2139095040
CK = 256
COUNT_ELEMS = 64 * SUBLANE * LANE
ACC_ROWS = HEAD_DIM + 16
LOG2_E = math.log2(math.e)
PAGES_PER_STEP = 16
SROWS = SUBLANE


def _rms(x, g):
    return x * lax.rsqrt(jnp.mean(x * x, axis=-1, keepdims=True) + EPS) * g


def _rope128(z, cos, sin):
    lane = lax.broadcasted_iota(jnp.int32, z.shape, 1)
    partner = jnp.where((lane % HEAD_DIM) < HEAD_DIM // 2,
                        pltpu.roll(z, LANE - HEAD_DIM // 2, 1),
                        pltpu.roll(z, HEAD_DIM // 2, 1))
    return z * cos + partner * sin


def _rope(z, cos, sin):
    return jnp.concatenate(
        [_rope128(z[:, i:i + LANE], cos, sin) for i in range(0, z.shape[1], LANE)], axis=1)


def _proj_kernel(*refs, sample, tm, seq_rows, n_aliased):
    n_in = 13
    refs = refs[:n_in] + refs[n_in + n_aliased:]
    if sample:
        (x_ref, gpre_ref, w_ref, cos_ref, sin_ref, lng_ref, lnb_ref, wmix_ref, bmix_ref,
         cw_ref, cbias_ref, wbb_ref, wbc_ref, st_ref,
         q_ref, k_ref, v_ref, qi_ref, kw_ref, g0_ref, mbc_ref, cin_ref, vn_ref) = refs
    else:
        (x_ref, gpre_ref, w_ref, cos_ref, sin_ref, lng_ref, lnb_ref, wmix_ref, bmix_ref,
         cw_ref, cbias_ref, wbb_ref, wbc_ref,
         qT_ref, qiT_ref, wT_ref, kb_ref, kib_ref, vTb_ref, kT_ref, vT_ref, kiT_ref,
         g0_ref, mbc_ref, ctail_ref, carry_ref) = refs

    h = _rms(x_ref[...], gpre_ref[...]).astype(BF16)
    cos = cos_ref[...]
    sin = sin_ref[...]

    z = jnp.dot(h, w_ref[:, 0:C_KW], preferred_element_type=F32)
    q = _rope(z[:, 0:E_A], cos, sin) * (HEAD_DIM ** -0.5 * (1.0 if sample else LOG2_E))
    k = _rope(z[:, E_A:2 * E_A], cos, sin)
    v = z[:, 2 * E_A:3 * E_A]
    qi = _rope(z[:, 3 * E_A:4 * E_A], cos, sin)

    zk = jnp.dot(h, w_ref[:, C_KW:C_B], preferred_element_type=F32)
    lane = lax.broadcasted_iota(jnp.int32, zk.shape, 1)
    kw = jnp.where(lane < IDX_DIM, _rope128(zk, cos, sin), zk * IDX_W_SCALE)
    if sample:
        q_ref[...] = q.astype(BF16)
        k_ref[...] = k
        v_ref[...] = v
        qi_ref[...] = qi.astype(BF16)
        kw_ref[...] = kw
    else:
        qT_ref[0] = q.T.astype(BF16)
        qiT_ref[0] = qi.T.astype(BF16)
        kT = k.T
        kT_ref[0] = kT
        kb_ref[...] = k.astype(BF16)
        vT = v.T
        vT_ref[0] = vT
        vTb_ref[0] = vT.astype(BF16)
        kwT = kw.T
        kiT_ref[0] = kwT[0:IDX_DIM, :]
        wT_ref[0] = kwT[IDX_DIM:IDX_DIM + IDX_HEADS, :]
        kib_ref[...] = kw[:, 0:IDX_DIM].astype(BF16)

    gl = jax.nn.gelu(jnp.dot(h, w_ref[:, C_B:C_C], preferred_element_type=F32))
    u = gl[:, 0:E_B]
    vg = gl[:, E_B:2 * E_B]
    mu = jnp.mean(vg, axis=-1, keepdims=True)
    vc = vg - mu
    vn = vc * lax.rsqrt(jnp.mean(vc * vc, axis=-1, keepdims=True) + EPS) * lng_ref[...] + lnb_ref[...]
    if sample:
        vn_ref[...] = vn
    vnb = vn.astype(BF16)
    gw = E_B // GMLP_GROUPS
    rows = []
    for c in range(tm // CHUNK):
        cols = []
        for g in range(GMLP_GROUPS):
            cols.append(jnp.dot(wmix_ref[g], vnb[c * CHUNK:(c + 1) * CHUNK, g * gw:(g + 1) * gw],
                                preferred_element_type=F32))
        rows.append(jnp.concatenate(cols, axis=1) + bmix_ref[...])
    mix = rows[0] if len(rows) == 1 else jnp.concatenate(rows, axis=0)
    ob = (u * mix).astype(BF16)

    zc = jnp.dot(h, w_ref[:, C_C:C_G], preferred_element_type=F32)
    cb = zc[:, 0:E_C]
    cin = zc[:, E_C:2 * E_C] * zc[:, 2 * E_C:3 * E_C]
    row = lax.broadcasted_iota(jnp.int32, cin.shape, 0)
    if sample:
        cin_ref[...] = cin
        i_in_seq = row % seq_rows
        s1 = jnp.where(i_in_seq >= 1, pltpu.roll(cin, 1, 0), st_ref[:, 0:E_C])
        s2 = jnp.where(i_in_seq >= 2, pltpu.roll(cin, 2, 0), st_ref[:, E_C:2 * E_C])
    else:
        @pl.when(pl.program_id(0) % (seq_rows // tm) == 0)
        def _():
            carry_ref[...] = jnp.zeros_like(carry_ref)
        p2 = carry_ref[0:1, :]
        p1 = carry_ref[1:2, :]
        s1 = jnp.where(row >= 1, pltpu.roll(cin, 1, 0), p1)
        s2 = jnp.where(row >= 2, pltpu.roll(cin, 2, 0), jnp.where(row == 0, p2, p1))
        carry_ref[0:2, :] = cin[tm - 2:tm, :]
        ctail_ref[0] = cin[tm - SUBLANE:tm, :]
    y = s2 * cw_ref[0:1, :] + s1 * cw_ref[1:2, :] + cin * cw_ref[2:3, :] + cbias_ref[...]
    oc = (cb * y).astype(BF16)

    g0_ref[...] = jax.nn.sigmoid(
        jnp.dot(h, w_ref[:, C_G:C_G + D_MODEL], preferred_element_type=F32)).astype(BF16)
    g1 = jax.nn.sigmoid(jnp.dot(h, w_ref[:, C_G + D_MODEL:C_G + 2 * D_MODEL], preferred_element_type=F32))
    mbc = g1 * jnp.dot(ob, wbb_ref[...], preferred_element_type=F32)
    g2 = jax.nn.sigmoid(jnp.dot(h, w_ref[:, C_G + 2 * D_MODEL:N_PAD], preferred_element_type=F32))
    mbc_ref[...] = mbc + g2 * jnp.dot(oc, wbc_ref[...], preferred_element_type=F32)


def _layer_spec(layer, shape):
    nd = len(shape)
    return pl.BlockSpec((None,) + shape, lambda *_: (layer,) + (0,) * nd, pipeline_mode=pl.Buffered(1))


STACK_OUTPUTS = (6, 7, 8)


def _proj(x, gpre, w, cos, sin, lng, lnb, wmix, bmix, cw, cbias, wbb, wbc, *, layer, sample, seq_len,
          st=None, stacks=None):
    m = x.shape[0]
    aliases = {}
    _const_spec = functools.partial(_layer_spec, layer)
    tm = m if sample else ROW_TILE
    tps = max(seq_len // tm, 1)
    n_seq = m // seq_len
    row = lambda n: pl.BlockSpec((tm, n), lambda i: (i, 0))
    pos_spec = pl.BlockSpec((tm, LANE), (lambda i: (0, 0)) if sample else (lambda i: (i % tps, 0)))
    in_specs = [row(D_MODEL), _const_spec((1, D_MODEL)), _const_spec((D_MODEL, N_PAD)),
                pos_spec, pos_spec, _const_spec((1, E_B)), _const_spec((1, E_B)),
                _const_spec((GMLP_GROUPS, CHUNK, CHUNK)), _const_spec((CHUNK, E_B)),
                _const_spec((CONV_W, E_C)), _const_spec((1, E_C)),
                _const_spec((E_B, D_MODEL)), _const_spec((E_C, D_MODEL))]
    args = [x, gpre, w, cos, sin, lng, lnb, wmix, bmix, cw, cbias, wbb, wbc]
    sds = jax.ShapeDtypeStruct
    if sample:
        in_specs.append(pl.BlockSpec((None, tm, 2 * E_C), lambda i: (layer, i, 0)))
        args.append(st)
        out_shape = [sds((m, E_A), BF16), sds((m, E_A), F32), sds((m, E_A), F32), sds((m, E_A), BF16),
                     sds((m, LANE), F32), sds((m, D_MODEL), BF16), sds((m, D_MODEL), F32),
                     sds((m, E_C), F32), sds((m, E_B), F32)]
        out_specs = [row(E_A), row(E_A), row(E_A), row(E_A), row(LANE), row(D_MODEL), row(D_MODEL),
                     row(E_C), row(E_B)]
        scratch = []
    else:
        depth = w.shape[0]
        colT = lambda n: pl.BlockSpec((1, n, tm), lambda i: (i // tps, 0, i % tps))
        slabT = lambda n: pl.BlockSpec((None, 1, n, tm), lambda i: (layer, i // tps, 0, i % tps))
        out_shape = [sds((n_seq, E_A, seq_len), BF16), sds((n_seq, E_A, seq_len), BF16),
                     sds((n_seq, IDX_HEADS, seq_len), F32), sds((m, E_A), BF16), sds((m, IDX_DIM), BF16),
                     sds((n_seq, E_A, seq_len), BF16), sds((depth, n_seq, E_A, seq_len), F32),
                     sds((depth, n_seq, E_A, seq_len), F32), sds((depth, n_seq, IDX_DIM, seq_len), F32),
                     sds((m, D_MODEL), BF16), sds((m, D_MODEL), F32), sds((n_seq, SUBLANE, E_C), F32)]
        out_specs = [colT(E_A), colT(E_A), colT(IDX_HEADS), row(E_A), row(IDX_DIM),
                     colT(E_A), slabT(E_A), slabT(E_A), slabT(IDX_DIM),
                     row(D_MODEL), row(D_MODEL),
                     pl.BlockSpec((1, SUBLANE, E_C), lambda i: (i // tps, 0, 0))]
        scratch = [pltpu.VMEM((SUBLANE, E_C), F32)]
        if stacks is not None:
            n_in = len(args)
            in_specs += [pl.BlockSpec(memory_space=pl.ANY)] * len(stacks)
            args += list(stacks)
            aliases = {n_in + i: STACK_OUTPUTS[i] for i in range(len(stacks))}
    return pl.pallas_call(
        functools.partial(_proj_kernel, sample=sample, tm=tm, seq_rows=seq_len, n_aliased=len(aliases)),
        grid=(m // tm,), in_specs=in_specs, out_specs=out_specs, out_shape=out_shape,
        scratch_shapes=scratch, input_output_aliases=aliases,
        compiler_params=pltpu.CompilerParams(dimension_semantics=("arbitrary",),
                                             vmem_limit_bytes=VMEM_LIMIT),
        name="proj_sample" if sample else "proj_prompt",
    )(*args)


def _key_to_f32(key):
    bits = key ^ ((key >> 31) & jnp.int32(0x7FFFFFFF))
    return lax.bitcast_convert_type(bits, F32)


def _truncate_bf16(x):
    bits = lax.bitcast_convert_type(x, jnp.int32) & jnp.int32(-65536)
    return lax.bitcast_convert_type(bits, F32).astype(BF16)


def _topk_threshold(sc_ref, tb_ref, n_chunks, n_chunks_packed, k_row):
    lanes = sc_ref.shape[1]
    k_f = k_row.astype(F32)
    n_chunks_f32 = n_chunks

    def count(ref, pred_fn, dtype):
        n_chunks = n_chunks_packed if dtype == BF16 else n_chunks_f32
        count_rows = COUNT_ELEMS // lanes * (4 // jnp.dtype(dtype).itemsize)
        acc_rows = count_rows // 8
        one, zero = jnp.ones((), dtype), jnp.zeros((), dtype)

        def body(c, acc):
            off = pl.multiple_of(c * count_rows, count_rows)
            hit = jnp.where(pred_fn(ref[pl.ds(off, count_rows), :]), one, zero)
            p = [hit[i * acc_rows:(i + 1) * acc_rows] for i in range(8)]
            return acc + (((p[0] + p[1]) + (p[2] + p[3])) + ((p[4] + p[5]) + (p[6] + p[7])))
        acc = lax.fori_loop(0, n_chunks * CK // count_rows, body, jnp.zeros((acc_rows, lanes), dtype))
        return jnp.sum(acc.astype(F32), axis=0, keepdims=True)

    def bisect_high(_, carry):
        lo, hi, n_hi = carry
        mid = (lo + hi) >> 1
        rep = (mid << 16) | jnp.where(mid < 0, jnp.int32(0xFFFF), jnp.int32(0))
        midb = _key_to_f32(rep).astype(BF16)
        n_mid = count(tb_ref, lambda t: t >= midb, BF16)
        ge = n_mid >= k_f
        return jnp.where(ge, mid, lo), jnp.where(ge, hi, mid), jnp.where(ge, n_hi, n_mid)

    def bisect_low(_, carry):
        lo, hi, n_hi = carry
        mid = (lo >> 1) + (hi >> 1) + (lo & hi & 1)
        midf = _key_to_f32(mid)
        n_mid = count(sc_ref, lambda s: s >= midf, F32)
        ge = n_mid >= k_f
        return jnp.where(ge, mid, lo), jnp.where(ge, hi, mid), jnp.where(ge, n_hi, n_mid)

    lo0 = jnp.full((1, lanes), KEY_NEG_INF >> 16, jnp.int32)
    hi0 = jnp.full((1, lanes), (KEY_POS_INF >> 16) + 1, jnp.int32)
    hi_half, _, n_hi = lax.fori_loop(0, 16, bisect_high, (lo0, hi0, jnp.zeros((1, lanes), F32)))
    lo, _, n_above = lax.fori_loop(0, 16, bisect_low, (hi_half << 16, (hi_half << 16) + 65536, n_hi))
    return _key_to_f32(lo), k_f - n_above


def _select_cols(s, thr, n_eq_take, eq_seen, tril):
    eq = s == thr
    prefix = jnp.dot(tril, jnp.where(eq, 1.0, 0.0).astype(BF16), preferred_element_type=F32) + eq_seen
    sel = (s > thr) | (eq & (prefix <= n_eq_take))
    return sel, prefix[s.shape[0] - 1:s.shape[0], :]


def _select_rows(s, thr, n_eq_take, eq_seen, triu):
    blk = triu.shape[0]
    eq = s == thr
    eqf = jnp.where(eq, 1.0, 0.0)
    prefixes = []
    for i in range(s.shape[1] // blk):
        e = eqf[:, i * blk:(i + 1) * blk]
        prefixes.append(jnp.dot(e.astype(BF16), triu, preferred_element_type=F32) + eq_seen)
        eq_seen = eq_seen + jnp.sum(e, axis=1, keepdims=True)
    prefix = prefixes[0] if len(prefixes) == 1 else jnp.concatenate(prefixes, axis=1)
    return (s > thr) | (eq & (prefix <= n_eq_take)), eq_seen


def _tri(n, lower):
    r = lax.broadcasted_iota(jnp.int32, (n, n), 0)
    c = lax.broadcasted_iota(jnp.int32, (n, n), 1)
    return jnp.where((c <= r) if lower else (r <= c), 1.0, 0.0).astype(BF16)


def _attn_prompt_kernel(qT_ref, qiT_ref, wT_ref, ki_ref, k_ref, vT_ref, o_ref, sc_ref, tb_ref, acc_ref,
                        lga_ref, lgb_ref, wq_ref, tri_ref, *, topk):
    j = pl.program_id(1)
    nck = (j * Q_BLOCK) // CK + 1
    qiT = qiT_ref[0]
    w_idx = jnp.concatenate([qiT[h * IDX_DIM:(h + 1) * IDX_DIM, :] for h in range(IDX_HEADS)], axis=1)
    wT = wT_ref[0]
    q_pos = j * Q_BLOCK + lax.broadcasted_iota(jnp.int32, (1, Q_BLOCK), 1)

    def score_chunk(c, carry):
        off = pl.multiple_of(c * CK, CK)
        d = jnp.dot(ki_ref[0, pl.ds(off, CK), :], w_idx, preferred_element_type=F32)
        acc = jnp.maximum(d[:, 0:Q_BLOCK], 0.0) * wT[0:1, :]
        for h in range(1, IDX_HEADS):
            acc = acc + jnp.maximum(d[:, h * Q_BLOCK:(h + 1) * Q_BLOCK], 0.0) * wT[h:h + 1, :]
        key_pos = off + lax.broadcasted_iota(jnp.int32, (CK, 1), 0)
        acc = jnp.where(key_pos <= q_pos, acc, -jnp.inf)
        sc_ref[pl.ds(off, CK), :] = acc
        tb_ref[pl.ds(off, CK), :] = _truncate_bf16(acc)
        return carry

    lax.fori_loop(0, nck, score_chunk, 0)

    @pl.when(nck * CK < sc_ref.shape[0])
    def _():
        off = pl.multiple_of(nck * CK, CK)
        sc_ref[pl.ds(off, CK), :] = jnp.full((CK, Q_BLOCK), -jnp.inf, F32)
        tb_ref[pl.ds(off, CK), :] = jnp.full((CK, Q_BLOCK), -jnp.inf, BF16)

    thr, n_eq_take = _topk_threshold(sc_ref, tb_ref, nck, 2 * ((nck + 1) // 2),
                                     jnp.minimum(q_pos + 1, topk))

    qT = qT_ref[0].astype(F32)
    upper = lax.broadcasted_iota(jnp.int32, (LANE, Q_BLOCK), 0) < HEAD_DIM
    for h in range(N_HEADS):
        slab = qT[(h // 2) * LANE:(h // 2 + 1) * LANE, :]
        wq_ref[h] = jnp.where(upper if h % 2 == 0 else ~upper, slab, 0.0).astype(BF16)
    tri_ref[...] = _tri(CK, lower=True)
    acc_ref[...] = jnp.zeros_like(acc_ref)

    last_chunk = k_ref.shape[1] // CK - 1

    def chunk_offset(c):
        return pl.multiple_of(jnp.minimum(c, last_chunk) * CK, CK)

    def masked_logits(c, eq_seen, lg_ref):
        off = chunk_offset(c)
        sel, eq_seen = _select_cols(sc_ref[pl.ds(off, CK), :], thr, n_eq_take, eq_seen, tri_ref[...])
        sc_ref[pl.ds(off, CK), :] = jnp.where(sel, 0.0, -jnp.inf)
        col_max = []
        for h in range(N_HEADS):
            kh = k_ref[0, pl.ds(off, CK), (h // 2) * LANE:(h // 2 + 1) * LANE]
            lg = jnp.dot(kh, wq_ref[h], preferred_element_type=F32) + sc_ref[pl.ds(off, CK), :]
            lg_ref[h] = lg
            col_max.append(jnp.max(lg, axis=0, keepdims=True))
        return eq_seen, jnp.concatenate(col_max, axis=0)

    def softmax_update(c, lg_ref, lg_next_ref, carry):
        eq_seen, m_all, chunk_max = carry
        m_new = jnp.maximum(m_all, chunk_max)
        alpha = jnp.exp2(m_all - m_new)
        if lg_next_ref is not None:
            eq_seen, chunk_max = masked_logits(c + 1, eq_seen, lg_next_ref)
        off = chunk_offset(c)
        ones_rows = jnp.ones((ACC_ROWS - HEAD_DIM, CK), BF16)
        for h in range(N_HEADS):
            p = jnp.exp2(lg_ref[h] - m_new[h:h + 1, :])
            lhs = jnp.concatenate([vT_ref[0, h * HEAD_DIM:(h + 1) * HEAD_DIM, pl.ds(off, CK)], ones_rows],
                                  axis=0)
            acc_ref[h] = alpha[h:h + 1, :] * acc_ref[h] + jnp.dot(lhs, p.astype(BF16),
                                                                  preferred_element_type=F32)
        return eq_seen, m_new, chunk_max

    def attend_pair(i, carry):
        carry = softmax_update(2 * i, lga_ref, lgb_ref, carry)
        return softmax_update(2 * i + 1, lgb_ref, lga_ref, carry)

    eq_seen0, chunk_max0 = masked_logits(0, jnp.zeros((1, Q_BLOCK), F32), lga_ref)
    carry = lax.fori_loop(0, nck // 2, attend_pair,
                          (eq_seen0, jnp.full((N_HEADS, Q_BLOCK), NEG_BIG, F32), chunk_max0))

    @pl.when(nck % 2 == 1)
    def _():
        softmax_update(nck - 1, lga_ref, None, carry)

    o_ref[0] = jnp.concatenate(
        [acc_ref[h][0:HEAD_DIM, :] / acc_ref[h][HEAD_DIM:HEAD_DIM + 1, :] for h in range(N_HEADS)],
        axis=0).T.astype(BF16)


def _attn_prompt(qT, qiT, wT, kib, kb, vTb, *, topk):
    b, _, s = qT.shape
    colT = lambda n: pl.BlockSpec((1, n, Q_BLOCK), lambda i, j: (i, 0, j))
    return pl.pallas_call(
        functools.partial(_attn_prompt_kernel, topk=topk),
        grid=(b, s // Q_BLOCK),
        in_specs=[colT(E_A), colT(E_A), colT(IDX_HEADS),
                  pl.BlockSpec((1, s, IDX_DIM), lambda i, j: (i, 0, 0)),
                  pl.BlockSpec((1, s, E_A), lambda i, j: (i, 0, 0)),
                  pl.BlockSpec((1, E_A, s), lambda i, j: (i, 0, 0))],
        out_specs=pl.BlockSpec((1, Q_BLOCK, E_A), lambda i, j: (i, j, 0)),
        out_shape=jax.ShapeDtypeStruct((b, s, E_A), BF16),
        scratch_shapes=[pltpu.VMEM((s, Q_BLOCK), F32), pltpu.VMEM((s, Q_BLOCK), BF16),
                        pltpu.VMEM((N_HEADS, ACC_ROWS, Q_BLOCK), F32),
                        pltpu.VMEM((N_HEADS, CK, Q_BLOCK), F32), pltpu.VMEM((N_HEADS, CK, Q_BLOCK), F32),
                        pltpu.VMEM((N_HEADS, LANE, Q_BLOCK), BF16), pltpu.VMEM((CK, CK), BF16)],
        compiler_params=pltpu.CompilerParams(dimension_semantics=("arbitrary", "arbitrary"),
                                             vmem_limit_bytes=VMEM_LIMIT),
        name="attn_prompt",
    )(qT, qiT, wT, kib, kb, vTb)


def _sample_scores_kernel(pt_ref, qi_ref, w_ref, kinew_ref, *refs, n_pages, t_new):
    pages = refs[:n_pages]
    out_ref = refs[n_pages]
    qi = qi_ref[0]
    w = w_ref[0][:, 0:1]

    def scores(keysT):
        t = jnp.maximum(jnp.dot(qi, keysT.astype(BF16), preferred_element_type=F32), 0.0) * w
        acc = t[0:SROWS]
        for h in range(1, IDX_HEADS):
            acc = acc + t[h * SROWS:(h + 1) * SROWS]
        return acc

    past = scores(jnp.concatenate([p[0, 0] for p in pages], axis=1))
    new = scores(kinew_ref[0])
    qrow = lax.broadcasted_iota(jnp.int32, new.shape, 0)
    kcol = lax.broadcasted_iota(jnp.int32, new.shape, 1)
    new = jnp.where(kcol <= jnp.minimum(qrow, t_new - 1), new, -jnp.inf)
    pad = jnp.full((SROWS, out_ref.shape[2] - past.shape[1] - new.shape[1]), -jnp.inf, F32)
    out_ref[0] = jnp.concatenate([past, new, pad], axis=1)


def _sample_scores(page_table, qi_h, w_h, kiT_new, cache_iT, layer, *, t_new):
    db, n_pages = page_table.shape
    page = cache_iT.shape[3]
    width = (n_pages // PAGES_PER_STEP + 1) * PAGES_PER_STEP * page

    def page_spec(i):
        return pl.BlockSpec((1, 1, IDX_DIM, page), lambda b, pt: (layer, pt[b, i], 0, 0))

    per_b = lambda *shape: pl.BlockSpec((1,) + shape, lambda b, pt: (b,) + (0,) * len(shape))
    grid_spec = pltpu.PrefetchScalarGridSpec(
        num_scalar_prefetch=1, grid=(db,),
        in_specs=[per_b(IDX_HEADS * SROWS, IDX_DIM), per_b(IDX_HEADS * SROWS, LANE), per_b(IDX_DIM, page)]
                 + [page_spec(i) for i in range(n_pages)],
        out_specs=per_b(SROWS, width))
    return pl.pallas_call(
        functools.partial(_sample_scores_kernel, n_pages=n_pages, t_new=t_new),
        grid_spec=grid_spec,
        out_shape=jax.ShapeDtypeStruct((db, SROWS, width), F32),
        compiler_params=pltpu.CompilerParams(dimension_semantics=("arbitrary",),
                                             vmem_limit_bytes=VMEM_LIMIT),
        name="sample_scores",
    )(page_table, qi_h, w_h, kiT_new, *([cache_iT] * n_pages))


def _sample_threshold_kernel(sc_ref, thr_ref, take_ref, tb_ref, *, topk):
    keys, lanes = sc_ref.shape

    def truncate_chunk(c, carry):
        off = pl.multiple_of(c * CK, CK)
        tb_ref[pl.ds(off, CK), :] = _truncate_bf16(sc_ref[pl.ds(off, CK), :])
        return carry

    lax.fori_loop(0, keys // CK, truncate_chunk, 0)
    thr, n_eq_take = _topk_threshold(sc_ref, tb_ref, keys // CK, keys // CK,
                                     jnp.full((1, lanes), topk, jnp.int32))
    thr_ref[...] = jnp.broadcast_to(thr, thr_ref.shape)
    take_ref[...] = jnp.broadcast_to(n_eq_take, take_ref.shape)


def _sample_threshold(scoresT, *, topk):
    keys, nq = scoresT.shape
    return pl.pallas_call(
        functools.partial(_sample_threshold_kernel, topk=topk),
        grid=(nq // LANE,),
        in_specs=[pl.BlockSpec((keys, LANE), lambda i: (0, i))],
        out_specs=[pl.BlockSpec((SUBLANE, LANE), lambda i: (0, i))] * 2,
        out_shape=[jax.ShapeDtypeStruct((SUBLANE, nq), F32)] * 2,
        scratch_shapes=[pltpu.VMEM((keys, LANE), BF16)],
        compiler_params=pltpu.CompilerParams(dimension_semantics=("arbitrary",),
                                             vmem_limit_bytes=VMEM_LIMIT),
        name="sample_threshold",
    )(scoresT)


def _sample_attn_kernel(pt_ref, q_ref, sc_ref, thr_ref, take_ref, kTnew_ref, vTnew_ref, *refs, nch):
    kpages = refs[:PAGES_PER_STEP]
    vpages = refs[PAGES_PER_STEP:2 * PAGES_PER_STEP]
    o_ref, m_ref, l_ref, acc_ref, seen_ref = refs[2 * PAGES_PER_STEP:]
    c = pl.program_id(1)

    @pl.when(c == 0)
    def _():
        m_ref[...] = jnp.full_like(m_ref, NEG_BIG)
        l_ref[...] = jnp.zeros_like(l_ref)
        acc_ref[...] = jnp.zeros_like(acc_ref)
        seen_ref[...] = jnp.zeros_like(seen_ref)

    thr = thr_ref[0][:, 0:1]
    take = take_ref[0][:, 0:1]

    def attend(kT, vT):
        width = kT.shape[1]
        sel, seen = _select_rows(sc_ref[0][:, 0:width], thr, take, seen_ref[:, 0:1],
                                 _tri(min(CK, width), lower=False))
        seen_ref[...] = jnp.broadcast_to(seen, seen_ref.shape)
        sel = jnp.tile(sel, (N_HEADS, 1))
        lg = jnp.where(sel, jnp.dot(q_ref[0], kT.astype(BF16), preferred_element_type=F32), -jnp.inf)
        m_old = m_ref[:, 0:1]
        m_new = jnp.maximum(m_old, jnp.max(lg, axis=1, keepdims=True))
        p = jnp.exp(lg - m_new)
        alpha = jnp.exp(m_old - m_new)
        l_new = alpha * l_ref[:, 0:1] + jnp.sum(p, axis=1, keepdims=True)
        pv = lax.dot_general(p.astype(BF16), vT.astype(BF16), (((1,), (1,)), ((), ())),
                             preferred_element_type=F32)
        acc_ref[...] = alpha * acc_ref[...] + pv
        m_ref[...] = jnp.broadcast_to(m_new, m_ref.shape)
        l_ref[...] = jnp.broadcast_to(l_new, l_ref.shape)

    def stack(pages):
        return jnp.concatenate([p[0, 0].reshape(E_A, p.shape[4]) for p in pages], axis=1)

    @pl.when(c < nch)
    def _():
        attend(stack(kpages), stack(vpages))

    @pl.when(c == nch)
    def _():
        attend(kTnew_ref[0], vTnew_ref[0])
        o = acc_ref[...] / l_ref[:, 0:1]
        o_ref[0] = jnp.concatenate(
            [o[h * SROWS:(h + 1) * SROWS, h * HEAD_DIM:(h + 1) * HEAD_DIM] for h in range(N_HEADS)], axis=1)


def _sample_attn(page_table, q_bd, scores, thr, take, kT_new, vT_new, cache_kT, cache_vT, layer):
    db, n_pages = page_table.shape
    nch = n_pages // PAGES_PER_STEP
    page = cache_kT.shape[4]
    sck = PAGES_PER_STEP * page
    hq = N_HEADS * SROWS

    def page_spec(i):
        return pl.BlockSpec(
            (1, 1, N_HEADS, HEAD_DIM, page),
            lambda b, c, pt: (layer, pt[b, jnp.minimum(c, nch - 1) * PAGES_PER_STEP + i], 0, 0, 0))

    per_b = lambda *shape: pl.BlockSpec((1,) + shape, lambda b, c, pt: (b,) + (0,) * len(shape))
    grid_spec = pltpu.PrefetchScalarGridSpec(
        num_scalar_prefetch=1, grid=(db, nch + 1),
        in_specs=[per_b(hq, E_A),
                  pl.BlockSpec((1, SROWS, sck), lambda b, c, pt: (b, 0, c)),
                  per_b(SROWS, LANE), per_b(SROWS, LANE),
                  per_b(E_A, page), per_b(E_A, page)]
                 + [page_spec(i) for i in range(PAGES_PER_STEP)] * 2,
        out_specs=per_b(SROWS, E_A),
        scratch_shapes=[pltpu.VMEM((hq, LANE), F32), pltpu.VMEM((hq, LANE), F32),
                        pltpu.VMEM((hq, E_A), F32), pltpu.VMEM((SROWS, LANE), F32)])
    return pl.pallas_call(
        functools.partial(_sample_attn_kernel, nch=nch),
        grid_spec=grid_spec,
        out_shape=jax.ShapeDtypeStruct((db, SROWS, E_A), F32),
        compiler_params=pltpu.CompilerParams(dimension_semantics=("arbitrary", "arbitrary"),
                                             vmem_limit_bytes=VMEM_LIMIT),
        name="sample_attn",
    )(page_table, q_bd, scores, thr, take, kT_new, vT_new,
      *([cache_kT] * PAGES_PER_STEP), *([cache_vT] * PAGES_PER_STEP))


def _post_kernel(x_ref, oa_ref, g0_ref, mbc_ref, wba_ref, wo_ref, wup_ref, wdn_ref,
                 gpost_ref, gfpre_ref, gfpost_ref, y_ref):
    m = g0_ref[...].astype(F32) * jnp.dot(oa_ref[...], wba_ref[...], preferred_element_type=F32) \
        + mbc_ref[...]
    y = jnp.dot(m.astype(BF16), wo_ref[...], preferred_element_type=F32)
    x1 = x_ref[...] + _rms(y, gpost_ref[...])
    a = jnp.maximum(jnp.dot(_rms(x1, gfpre_ref[...]).astype(BF16), wup_ref[...],
                            preferred_element_type=F32), 0.0)
    f = jnp.dot((a * a).astype(BF16), wdn_ref[...], preferred_element_type=F32)
    y_ref[...] = x1 + _rms(f, gfpost_ref[...])


def _post(x, oa, g0, mbc, wba, wo, wup, wdn, gpost, gfpre, gfpost, *, layer):
    m = x.shape[0]
    _const_spec = functools.partial(_layer_spec, layer)
    tm = min(POST_ROW_TILE, m)
    row = lambda n: pl.BlockSpec((tm, n), lambda i: (i, 0))
    return pl.pallas_call(
        _post_kernel, grid=(m // tm,),
        in_specs=[row(D_MODEL), row(E_A), row(D_MODEL), row(D_MODEL),
                  _const_spec((E_A, D_MODEL)), _const_spec((D_MODEL, D_MODEL)),
                  _const_spec((D_MODEL, D_FF)), _const_spec((D_FF, D_MODEL)),
                  _const_spec((1, D_MODEL)), _const_spec((1, D_MODEL)), _const_spec((1, D_MODEL))],
        out_specs=row(D_MODEL),
        out_shape=jax.ShapeDtypeStruct((m, D_MODEL), F32),
        compiler_params=pltpu.CompilerParams(dimension_semantics=("arbitrary",),
                                             vmem_limit_bytes=VMEM_LIMIT),
        name="post",
    )(x, oa, g0, mbc, wba, wo, wup, wdn, gpost, gfpre, gfpost)


def _rope_tables(pos):
    half = HEAD_DIM // 2
    inv = ROPE_THETA ** (-jnp.arange(half, dtype=F32) * (2.0 / HEAD_DIM))
    ang = pos[:, None] * inv[None, :]
    c, s = jnp.cos(ang), jnp.sin(ang)
    return jnp.tile(c, (1, LANE // half)), jnp.tile(jnp.concatenate([-s, s], axis=1), (1, LANE // HEAD_DIM))


def _heads_first(a, t):
    db = a.shape[0] // t
    a = a.reshape(db, t, N_HEADS, -1).transpose(0, 2, 1, 3)
    a = jnp.pad(a, ((0, 0), (0, 0), (0, SROWS - t), (0, 0)))
    return a.reshape(db, N_HEADS * SROWS, -1)


def _new_keys_T(a, db, t, page):
    a = jnp.pad(a.reshape(db, t, -1), ((0, 0), (0, page - t), (0, 0)))
    return a.transpose(0, 2, 1)


def kernel(x_prompt, x_sample, cache_k, cache_v, cache_idx_k, state_conv, page_table, norm_mix_pre, norm_mix_post, norm_ffn_pre, norm_ffn_post, w_in, gmlp_ln_g, gmlp_ln_b, gmlp_ws, gmlp_bs, conv_w, conv_b, w_br_attn, w_br_gmlp, w_br_conv, w_out, w_ff_up, w_ff_down):
    depth = w_in.shape[0]
    b, s, _ = x_prompt.shape
    db, t, _ = x_sample.shape
    page = cache_k.shape[2]
    n_pages = page_table.shape[1]
    past = n_pages * page
    ms = db * t
    assert s % POST_ROW_TILE == 0 and s % (2 * CK) == 0 and ms == CHUNK and CONV_W - 1 <= t <= SROWS
    assert page == LANE and n_pages % PAGES_PER_STEP == 0

    cache_kT = cache_k.transpose(0, 1, 3, 4, 2)
    cache_vT = cache_v.transpose(0, 1, 3, 4, 2)
    cache_iT = cache_idx_k.transpose(0, 1, 3, 2)

    w_in_p = jnp.concatenate(
        [w_in[:, :, :OFF_B], jnp.zeros((depth, D_MODEL, C_B - OFF_B), w_in.dtype), w_in[:, :, OFF_B:]],
        axis=2).astype(BF16)
    causal = jnp.tril(jnp.ones((CHUNK, CHUNK), bool))
    wm = jnp.where(causal[None, None], gmlp_ws, 0)
    wmix_p = wm.astype(BF16)
    bmix_p = jnp.repeat(jnp.swapaxes(gmlp_bs, 1, 2), E_B // GMLP_GROUPS, axis=2)
    eye = jnp.eye(db, dtype=wm.dtype)
    wmix_s = jnp.einsum('ab,lgts->lgatbs', eye, wm[:, :, :t, :t]).reshape(depth, GMLP_GROUPS, ms, ms)
    wmix_s = wmix_s.astype(BF16)
    bmix_s = jnp.tile(bmix_p[:, :t], (1, db, 1))
    wbb, wbc, wba = (w.astype(BF16) for w in (w_br_gmlp, w_br_conv, w_br_attn))
    wo, wup, wdn = (w.astype(BF16) for w in (w_out, w_ff_up, w_ff_down))
    row_stack = lambda a: a[:, None, :]
    common = (row_stack(norm_mix_pre), w_in_p)
    tail = (row_stack(gmlp_ln_g), row_stack(gmlp_ln_b))
    conv = (conv_w, row_stack(conv_b), wbb, wbc)
    post_w = (wba, wo, wup, wdn, row_stack(norm_mix_post), row_stack(norm_ffn_pre),
              row_stack(norm_ffn_post))
    zs = jnp.zeros((depth, db, 1, E_C), state_conv.dtype)
    st1 = jnp.concatenate([state_conv[:, :, 1:2]] + [zs] * (t - 1), axis=2)
    st2 = jnp.concatenate([state_conv[:, :, 0:1], state_conv[:, :, 1:2]] + [zs] * (t - 2), axis=2)
    st12 = jnp.concatenate([st1, st2], axis=3).reshape(depth, ms, 2 * E_C)

    cos_p, sin_p = _rope_tables(jnp.arange(s, dtype=F32))
    cos_s, sin_s = _rope_tables(jnp.tile(jnp.arange(t, dtype=F32) + past, db))
    topk_p = min(TOPK_MAX, s // 4)
    topk_s = min(TOPK_MAX, (past + t) // 4)
    head_eye = jnp.eye(N_HEADS, dtype=BF16)

    xp = x_prompt.reshape(b * s, D_MODEL)
    xs = x_sample.reshape(ms, D_MODEL)
    outs = [[] for _ in range(9)]
    stacks = None
    for l in range(depth):
        (qT, qiT, wT, kb, kib, vTb, kT_all, vT_all, kiT_all, g0, mbc, ctail) = _proj(
            xp, *common, cos_p, sin_p, *tail, wmix_p, bmix_p, *conv, layer=l, sample=False, seq_len=s,
            stacks=stacks)
        stacks = (kT_all, vT_all, kiT_all)
        oa = _attn_prompt(qT, qiT, wT, kib.reshape(b, s, IDX_DIM), kb.reshape(b, s, E_A), vTb, topk=topk_p)
        xp = _post(xp, oa.reshape(b * s, E_A), g0, mbc, *post_w, layer=l)
        outs[3].append(ctail[:, SUBLANE - (CONV_W - 1):, :])
        (q_s, k_s, v_s, qi_s, kw_s, g0_s, mbc_s, cin_s, vn_s) = _proj(
            xs, *common, cos_s, sin_s, *tail, wmix_s, bmix_s, *conv, layer=l, sample=True, seq_len=t,
            st=st12)
        qi_h = _heads_first(qi_s, t)
        w_h = _heads_first(kw_s[:, IDX_DIM:IDX_DIM + IDX_HEADS][:, :, None], t)
        w_h = jnp.broadcast_to(w_h, w_h.shape[:2] + (LANE,))
        scores = _sample_scores(page_table, qi_h, w_h, _new_keys_T(kw_s[:, :IDX_DIM], db, t, page),
                                cache_iT, l, t_new=t)
        thr, take = _sample_threshold(scores[:, :t].reshape(ms, -1).T, topk=topk_s)
        per_q = lambda a, fill: jnp.broadcast_to(
            jnp.pad(a[0].reshape(db, t, 1), ((0, 0), (0, SROWS - t), (0, 0)), constant_values=fill),
            (db, SROWS, LANE))
        q4 = jnp.pad(q_s.reshape(db, t, N_HEADS, HEAD_DIM), ((0, 0), (0, SROWS - t), (0, 0), (0, 0)))
        q_bd = jnp.einsum('bqhd,hg->bhqgd', q4, head_eye).reshape(db, N_HEADS * SROWS, E_A)
        oa_s = _sample_attn(page_table, q_bd, scores, per_q(thr, NEG_BIG), per_q(take, 0.0),
                            _new_keys_T(k_s, db, t, page), _new_keys_T(v_s, db, t, page),
                            cache_kT, cache_vT, l)
        oa_s = oa_s[:, :t].reshape(ms, E_A).astype(BF16)
        xs = _post(xs, oa_s, g0_s, mbc_s, *post_w, layer=l)
        outs[4].append(k_s.reshape(db, t, N_HEADS, HEAD_DIM))
        outs[5].append(v_s.reshape(db, t, N_HEADS, HEAD_DIM))
        outs[6].append(kw_s[:, :IDX_DIM].reshape(db, t, IDX_DIM))
        outs[7].append(cin_s.reshape(db, t, E_C)[:, t - (CONV_W - 1):])
        outs[8].append(vn_s.reshape(db, t, E_B))
    kT_all, vT_all, kiT_all = stacks
    stacked = [None] * 3 + [jnp.stack(o) for o in outs[3:]]
    per_head = (depth, b, N_HEADS, HEAD_DIM, s)
    stacked[0] = kT_all.reshape(per_head).transpose(0, 1, 4, 2, 3)
    stacked[1] = vT_all.reshape(per_head).transpose(0, 1, 4, 2, 3)
    stacked[2] = kiT_all.transpose(0, 1, 3, 2)
    return (xp.reshape(b, s, D_MODEL), xs.reshape(db, t, D_MODEL)) + tuple(stacked)
```

```python
import functools
import math

import jax
import jax.numpy as jnp
from jax import lax
from jax.experimental import pallas as pl
from jax.experimental.pallas import tpu as pltpu

F32 = jnp.float32
BF16 = jnp.bfloat16

D_MODEL = 1024
N_HEADS = 8
HEAD_DIM = 64
E_A = N_HEADS * HEAD_DIM
IDX_HEADS = 8
IDX_DIM = 64
IDX_W_SCALE = 1.0 / math.sqrt(IDX_HEADS * IDX_DIM)
TOPK_MAX = 256
Q_BLOCK = 256
ROPE_THETA = 10000.0
CHUNK = 128
GMLP_GROUPS = 4
E_B = 512
E_C = 512
CONV_W = 3
D_FF = 4 * D_MODEL
EPS = 1e-6

OFF_KI = 3 * E_A + IDX_HEADS * IDX_DIM
OFF_B = OFF_KI + IDX_DIM + IDX_HEADS
LANE = 128
SUBLANE = 8
C_KW = OFF_KI
C_B = C_KW + LANE
C_C = C_B + 2 * E_B
C_G = C_C + 3 * E_C
N_PAD = C_G + 3 * D_MODEL

VMEM_LIMIT = 56 * 1024 * 1024
ROW_TILE = 256
POST_ROW_TILE = 512
NEG_BIG = -1e30
KEY_NEG_INF = -2139095041
KEY_POS_INF = 2139095040
CK = 256
COUNT_ELEMS = 64 * SUBLANE * LANE
ACC_ROWS = HEAD_DIM + 16
LOG2_E = math.log2(math.e)
PAGES_PER_STEP = 16
SROWS = SUBLANE


def _rms(x, g):
    return x * lax.rsqrt(jnp.mean(x * x, axis=-1, keepdims=True) + EPS) * g


def _rope128(z, cos, sin):
    lane = lax.broadcasted_iota(jnp.int32, z.shape, 1)
    partner = jnp.where((lane % HEAD_DIM) < HEAD_DIM // 2,
                        pltpu.roll(z, LANE - HEAD_DIM // 2, 1),
                        pltpu.roll(z, HEAD_DIM // 2, 1))
    return z * cos + partner * sin


def _rope(z, cos, sin):
    return jnp.concatenate(
        [_rope128(z[:, i:i + LANE], cos, sin) for i in range(0, z.shape[1], LANE)], axis=1)


def _proj_kernel(*refs, sample, tm, seq_rows, n_aliased):
    n_in = 13
    refs = refs[:n_in] + refs[n_in + n_aliased:]
    if sample:
        (x_ref, gpre_ref, w_ref, cos_ref, sin_ref, lng_ref, lnb_ref, wmix_ref, bmix_ref,
         cw_ref, cbias_ref, wbb_ref, wbc_ref, st_ref,
         q_ref, k_ref, v_ref, qi_ref, kw_ref, g0_ref, mbc_ref, cin_ref, vn_ref) = refs
    else:
        (x_ref, gpre_ref, w_ref, cos_ref, sin_ref, lng_ref, lnb_ref, wmix_ref, bmix_ref,
         cw_ref, cbias_ref, wbb_ref, wbc_ref,
         qT_ref, qiT_ref, wT_ref, kb_ref, kib_ref, vTb_ref, kT_ref, vT_ref, kiT_ref,
         g0_ref, mbc_ref, ctail_ref, carry_ref) = refs

    h = _rms(x_ref[...], gpre_ref[...]).astype(BF16)
    cos = cos_ref[...]
    sin = sin_ref[...]

    z = jnp.dot(h, w_ref[:, 0:C_KW], preferred_element_type=F32)
    q = _rope(z[:, 0:E_A], cos, sin) * (HEAD_DIM ** -0.5 * (1.0 if sample else LOG2_E))
    k = _rope(z[:, E_A:2 * E_A], cos, sin)
    v = z[:, 2 * E_A:3 * E_A]
    qi = _rope(z[:, 3 * E_A:4 * E_A], cos, sin)

    zk = jnp.dot(h, w_ref[:, C_KW:C_B], preferred_element_type=F32)
    lane = lax.broadcasted_iota(jnp.int32, zk.shape, 1)
    kw = jnp.where(lane < IDX_DIM, _rope128(zk, cos, sin), zk * IDX_W_SCALE)
    if sample:
        q_ref[...] = q.astype(BF16)
        k_ref[...] = k
        v_ref[...] = v
        qi_ref[...] = qi.astype(BF16)
        kw_ref[...] = kw
    else:
        qT_ref[0] = q.T.astype(BF16)
        qiT_ref[0] = qi.T.astype(BF16)
        kT = k.T
        kT_ref[0] = kT
        kb_ref[...] = k.astype(BF16)
        vT = v.T
        vT_ref[0] = vT
        vTb_ref[0] = vT.astype(BF16)
        kwT = kw.T
        kiT_ref[0] = kwT[0:IDX_DIM, :]
        wT_ref[0] = kwT[IDX_DIM:IDX_DIM + IDX_HEADS, :]
        kib_ref[...] = kw[:, 0:IDX_DIM].astype(BF16)

    gl = jax.nn.gelu(jnp.dot(h, w_ref[:, C_B:C_C], preferred_element_type=F32))
    u = gl[:, 0:E_B]
    vg = gl[:, E_B:2 * E_B]
    mu = jnp.mean(vg, axis=-1, keepdims=True)
    vc = vg - mu
    vn = vc * lax.rsqrt(jnp.mean(vc * vc, axis=-1, keepdims=True) + EPS) * lng_ref[...] + lnb_ref[...]
    if sample:
        vn_ref[...] = vn
    vnb = vn.astype(BF16)
    gw = E_B // GMLP_GROUPS
    rows = []
    for c in range(tm // CHUNK):
        cols = []
        for g in range(GMLP_GROUPS):
            cols.append(jnp.dot(wmix_ref[g], vnb[c * CHUNK:(c + 1) * CHUNK, g * gw:(g + 1) * gw],
                                preferred_element_type=F32))
        rows.append(jnp.concatenate(cols, axis=1) + bmix_ref[...])
    mix = rows[0] if len(rows) == 1 else jnp.concatenate(rows, axis=0)
    ob = (u * mix).astype(BF16)

    zc = jnp.dot(h, w_ref[:, C_C:C_G], preferred_element_type=F32)
    cb = zc[:, 0:E_C]
    cin = zc[:, E_C:2 * E_C] * zc[:, 2 * E_C:3 * E_C]
    row = lax.broadcasted_iota(jnp.int32, cin.shape, 0)
    if sample:
        cin_ref[...] = cin
        i_in_seq = row % seq_rows
        s1 = jnp.where(i_in_seq >= 1, pltpu.roll(cin, 1, 0), st_ref[:, 0:E_C])
        s2 = jnp.where(i_in_seq >= 2, pltpu.roll(cin, 2, 0), st_ref[:, E_C:2 * E_C])
    else:
        @pl.when(pl.program_id(0) % (seq_rows // tm) == 0)
        def _():
            carry_ref[...] = jnp.zeros_like(carry_ref)
        p2 = carry_ref[0:1, :]
        p1 = carry_ref[1:2, :]
        s1 = jnp.where(row >= 1, pltpu.roll(cin, 1, 0), p1)
        s2 = jnp.where(row >= 2, pltpu.roll(cin, 2, 0), jnp.where(row == 0, p2, p1))
        carry_ref[0:2, :] = cin[tm - 2:tm, :]
        ctail_ref[0] = cin[tm - SUBLANE:tm, :]
    y = s2 * cw_ref[0:1, :] + s1 * cw_ref[1:2, :] + cin * cw_ref[2:3, :] + cbias_ref[...]
    oc = (cb * y).astype(BF16)

    g0_ref[...] = jax.nn.sigmoid(
        jnp.dot(h, w_ref[:, C_G:C_G + D_MODEL], preferred_element_type=F32)).astype(BF16)
    g1 = jax.nn.sigmoid(jnp.dot(h, w_ref[:, C_G + D_MODEL:C_G + 2 * D_MODEL], preferred_element_type=F32))
    mbc = g1 * jnp.dot(ob, wbb_ref[...], preferred_element_type=F32)
    g2 = jax.nn.sigmoid(jnp.dot(h, w_ref[:, C_G + 2 * D_MODEL:N_PAD], preferred_element_type=F32))
    mbc_ref[...] = mbc + g2 * jnp.dot(oc, wbc_ref[...], preferred_element_type=F32)


def _layer_spec(layer, shape):
    nd = len(shape)
    return pl.BlockSpec((None,) + shape, lambda *_: (layer,) + (0,) * nd, pipeline_mode=pl.Buffered(1))


STACK_OUTPUTS = (6, 7, 8)


def _proj(x, gpre, w, cos, sin, lng, lnb, wmix, bmix, cw, cbias, wbb, wbc, *, layer, sample, seq_len,
          st=None, stacks=None):
    m = x.shape[0]
    aliases = {}
    _const_spec = functools.partial(_layer_spec, layer)
    tm = m if sample else ROW_TILE
    tps = max(seq_len // tm, 1)
    n_seq = m // seq_len
    row = lambda n: pl.BlockSpec((tm, n), lambda i: (i, 0))
    pos_spec = pl.BlockSpec((tm, LANE), (lambda i: (0, 0)) if sample else (lambda i: (i % tps, 0)))
    in_specs = [row(D_MODEL), _const_spec((1, D_MODEL)), _const_spec((D_MODEL, N_PAD)),
                pos_spec, pos_spec, _const_spec((1, E_B)), _const_spec((1, E_B)),
                _const_spec((GMLP_GROUPS, CHUNK, CHUNK)), _const_spec((CHUNK, E_B)),
                _const_spec((CONV_W, E_C)), _const_spec((1, E_C)),
                _const_spec((E_B, D_MODEL)), _const_spec((E_C, D_MODEL))]
    args = [x, gpre, w, cos, sin, lng, lnb, wmix, bmix, cw, cbias, wbb, wbc]
    sds = jax.ShapeDtypeStruct
    if sample:
        in_specs.append(pl.BlockSpec((None, tm, 2 * E_C), lambda i: (layer, i, 0)))
        args.append(st)
        out_shape = [sds((m, E_A), BF16), sds((m, E_A), F32), sds((m, E_A), F32), sds((m, E_A), BF16),
                     sds((m, LANE), F32), sds((m, D_MODEL), BF16), sds((m, D_MODEL), F32),
                     sds((m, E_C), F32), sds((m, E_B), F32)]
        out_specs = [row(E_A), row(E_A), row(E_A), row(E_A), row(LANE), row(D_MODEL), row(D_MODEL),
                     row(E_C), row(E_B)]
        scratch = []
    else:
        depth = w.shape[0]
        colT = lambda n: pl.BlockSpec((1, n, tm), lambda i: (i // tps, 0, i % tps))
        slabT = lambda n: pl.BlockSpec((None, 1, n, tm), lambda i: (layer, i // tps, 0, i % tps))
        out_shape = [sds((n_seq, E_A, seq_len), BF16), sds((n_seq, E_A, seq_len), BF16),
                     sds((n_seq, IDX_HEADS, seq_len), F32), sds((m, E_A), BF16), sds((m, IDX_DIM), BF16),
                     sds((n_seq, E_A, seq_len), BF16), sds((depth, n_seq, E_A, seq_len), F32),
                     sds((depth, n_seq, E_A, seq_len), F32), sds((depth, n_seq, IDX_DIM, seq_len), F32),
                     sds((m, D_MODEL), BF16), sds((m, D_MODEL), F32), sds((n_seq, SUBLANE, E_C), F32)]
        out_specs = [colT(E_A), colT(E_A), colT(IDX_HEADS), row(E_A), row(IDX_DIM),
                     colT(E_A), slabT(E_A), slabT(E_A), slabT(IDX_DIM),
                     row(D_MODEL), row(D_MODEL),
                     pl.BlockSpec((1, SUBLANE, E_C), lambda i: (i // tps, 0, 0))]
        scratch = [pltpu.VMEM((SUBLANE, E_C), F32)]
        if stacks is not None:
            n_in = len(args)
            in_specs += [pl.BlockSpec(memory_space=pl.ANY)] * len(stacks)
            args += list(stacks)
            aliases = {n_in + i: STACK_OUTPUTS[i] for i in range(len(stacks))}
    return pl.pallas_call(
        functools.partial(_proj_kernel, sample=sample, tm=tm, seq_rows=seq_len, n_aliased=len(aliases)),
        grid=(m // tm,), in_specs=in_specs, out_specs=out_specs, out_shape=out_shape,
        scratch_shapes=scratch, input_output_aliases=aliases,
        compiler_params=pltpu.CompilerParams(dimension_semantics=("arbitrary",),
                                             vmem_limit_bytes=VMEM_LIMIT),
        name="proj_sample" if sample else "proj_prompt",
    )(*args)


def _key_to_f32(key):
    bits = key ^ ((key >> 31) & jnp.int32(0x7FFFFFFF))
    return lax.bitcast_convert_type(bits, F32)


def _truncate_bf16(x):
    bits = lax.bitcast_convert_type(x, jnp.int32) & jnp.int32(-65536)
    return lax.bitcast_convert_type(bits, F32).astype(BF16)


def _topk_threshold(sc_ref, tb_ref, n_chunks, n_chunks_packed, k_row):
    lanes = sc_ref.shape[1]
    k_f = k_row.astype(F32)
    n_chunks_f32 = n_chunks

    def count(ref, pred_fn, dtype):
        n_chunks = n_chunks_packed if dtype == BF16 else n_chunks_f32
        count_rows = COUNT_ELEMS // lanes * (4 // jnp.dtype(dtype).itemsize)
        acc_rows = count_rows // 8
        one, zero = jnp.ones((), dtype), jnp.zeros((), dtype)

        def body(c, acc):
            off = pl.multiple_of(c * count_rows, count_rows)
            hit = jnp.where(pred_fn(ref[pl.ds(off, count_rows), :]), one, zero)
            p = [hit[i * acc_rows:(i + 1) * acc_rows] for i in range(8)]
            return acc + (((p[0] + p[1]) + (p[2] + p[3])) + ((p[4] + p[5]) + (p[6] + p[7])))
        acc = lax.fori_loop(0, n_chunks * CK // count_rows, body, jnp.zeros((acc_rows, lanes), dtype))
        return jnp.sum(acc.astype(F32), axis=0, keepdims=True)

    def bisect_high(_, carry):
        lo, hi, n_hi = carry
        mid = (lo + hi) >> 1
        rep = (mid << 16) | jnp.where(mid < 0, jnp.int32(0xFFFF), jnp.int32(0))
        midb = _key_to_f32(rep).astype(BF16)
        n_mid = count(tb_ref, lambda t: t >= midb, BF16)
        ge = n_mid >= k_f
        return jnp.where(ge, mid, lo), jnp.where(ge, hi, mid), jnp.where(ge, n_hi, n_mid)

    def bisect_low(_, carry):
        lo, hi, n_hi = carry
        mid = (lo >> 1) + (hi >> 1) + (lo & hi & 1)
        midf = _key_to_f32(mid)
        n_mid = count(sc_ref, lambda s: s >= midf, F32)
        ge = n_mid >= k_f
        return jnp.where(ge, mid, lo), jnp.where(ge, hi, mid), jnp.where(ge, n_hi, n_mid)

    lo0 = jnp.full((1, lanes), KEY_NEG_INF >> 16, jnp.int32)
    hi0 = jnp.full((1, lanes), (KEY_POS_INF >> 16) + 1, jnp.int32)
    hi_half, _, n_hi = lax.fori_loop(0, 16, bisect_high, (lo0, hi0, jnp.zeros((1, lanes), F32)))
    lo, _, n_above = lax.fori_loop(0, 16, bisect_low, (hi_half << 16, (hi_half << 16) + 65536, n_hi))
    return _key_to_f32(lo), k_f - n_above


def _select_cols(s, thr, n_eq_take, eq_seen, tril):
    eq = s == thr
    prefix = jnp.dot(tril, jnp.where(eq, 1.0, 0.0).astype(BF16), preferred_element_type=F32) + eq_seen
    sel = (s > thr) | (eq & (prefix <= n_eq_take))
    return sel, prefix[s.shape[0] - 1:s.shape[0], :]


def _select_rows(s, thr, n_eq_take, eq_seen, triu):
    blk = triu.shape[0]
    eq = s == thr
    eqf = jnp.where(eq, 1.0, 0.0)
    prefixes = []
    for i in range(s.shape[1] // blk):
        e = eqf[:, i * blk:(i + 1) * blk]
        prefixes.append(jnp.dot(e.astype(BF16), triu, preferred_element_type=F32) + eq_seen)
        eq_seen = eq_seen + jnp.sum(e, axis=1, keepdims=True)
    prefix = prefixes[0] if len(prefixes) == 1 else jnp.concatenate(prefixes, axis=1)
    return (s > thr) | (eq & (prefix <= n_eq_take)), eq_seen


def _tri(n, lower):
    r = lax.broadcasted_iota(jnp.int32, (n, n), 0)
    c = lax.broadcasted_iota(jnp.int32, (n, n), 1)
    return jnp.where((c <= r) if lower else (r <= c), 1.0, 0.0).astype(BF16)


def _attn_prompt_kernel(qT_ref, qiT_ref, wT_ref, ki_ref, k_ref, vT_ref, o_ref, sc_ref, tb_ref, acc_ref,
                        lga_ref, lgb_ref, wq_ref, tri_ref, *, topk):
    j = pl.program_id(1)
    nck = (j * Q_BLOCK) // CK + 1
    qiT = qiT_ref[0]
    w_idx = jnp.concatenate([qiT[h * IDX_DIM:(h + 1) * IDX_DIM, :] for h in range(IDX_HEADS)], axis=1)
    wT = wT_ref[0]
    q_pos = j * Q_BLOCK + lax.broadcasted_iota(jnp.int32, (1, Q_BLOCK), 1)

    def score_chunk(c, carry):
        off = pl.multiple_of(c * CK, CK)
        d = jnp.dot(ki_ref[0, pl.ds(off, CK), :], w_idx, preferred_element_type=F32)
        acc = jnp.maximum(d[:, 0:Q_BLOCK], 0.0) * wT[0:1, :]
        for h in range(1, IDX_HEADS):
            acc = acc + jnp.maximum(d[:, h * Q_BLOCK:(h + 1) * Q_BLOCK], 0.0) * wT[h:h + 1, :]
        key_pos = off + lax.broadcasted_iota(jnp.int32, (CK, 1), 0)
        acc = jnp.where(key_pos <= q_pos, acc, -jnp.inf)
        sc_ref[pl.ds(off, CK), :] = acc
        tb_ref[pl.ds(off, CK), :] = _truncate_bf16(acc)
        return carry

    lax.fori_loop(0, nck, score_chunk, 0)

    @pl.when(nck * CK < sc_ref.shape[0])
    def _():
        off = pl.multiple_of(nck * CK, CK)
        sc_ref[pl.ds(off, CK), :] = jnp.full((CK, Q_BLOCK), -jnp.inf, F32)
        tb_ref[pl.ds(off, CK), :] = jnp.full((CK, Q_BLOCK), -jnp.inf, BF16)

    thr, n_eq_take = _topk_threshold(sc_ref, tb_ref, nck, 2 * ((nck + 1) // 2),
                                     jnp.minimum(q_pos + 1, topk))

    qT = qT_ref[0].astype(F32)
    upper = lax.broadcasted_iota(jnp.int32, (LANE, Q_BLOCK), 0) < HEAD_DIM
    for h in range(N_HEADS):
        slab = qT[(h // 2) * LANE:(h // 2 + 1) * LANE, :]
        wq_ref[h] = jnp.where(upper if h % 2 == 0 else ~upper, slab, 0.0).astype(BF16)
    tri_ref[...] = _tri(CK, lower=True)
    acc_ref[...] = jnp.zeros_like(acc_ref)

    last_chunk = k_ref.shape[1] // CK - 1

    def chunk_offset(c):
        return pl.multiple_of(jnp.minimum(c, last_chunk) * CK, CK)

    def masked_logits(c, eq_seen, lg_ref):
        off = chunk_offset(c)
        sel, eq_seen = _select_cols(sc_ref[pl.ds(off, CK), :], thr, n_eq_take, eq_seen, tri_ref[...])
        sc_ref[pl.ds(off, CK), :] = jnp.where(sel, 0.0, -jnp.inf)
        col_max = []
        for h in range(N_HEADS):
            kh = k_ref[0, pl.ds(off, CK), (h // 2) * LANE:(h // 2 + 1) * LANE]
            lg = jnp.dot(kh, wq_ref[h], preferred_element_type=F32) + sc_ref[pl.ds(off, CK), :]
            lg_ref[h] = lg
            col_max.append(jnp.max(lg, axis=0, keepdims=True))
        return eq_seen, jnp.concatenate(col_max, axis=0)

    def softmax_update(c, lg_ref, lg_next_ref, carry):
        eq_seen, m_all, chunk_max = carry
        m_new = jnp.maximum(m_all, chunk_max)
        alpha = jnp.exp2(m_all - m_new)
        if lg_next_ref is not None:
            eq_seen, chunk_max = masked_logits(c + 1, eq_seen, lg_next_ref)
        off = chunk_offset(c)
        ones_rows = jnp.ones((ACC_ROWS - HEAD_DIM, CK), BF16)
        for h in range(N_HEADS):
            p = jnp.exp2(lg_ref[h] - m_new[h:h + 1, :])
            lhs = jnp.concatenate([vT_ref[0, h * HEAD_DIM:(h + 1) * HEAD_DIM, pl.ds(off, CK)], ones_rows],
                                  axis=0)
            acc_ref[h] = alpha[h:h + 1, :] * acc_ref[h] + jnp.dot(lhs, p.astype(BF16),
                                                                  preferred_element_type=F32)
        return eq_seen, m_new, chunk_max

    def attend_pair(i, carry):
        carry = softmax_update(2 * i, lga_ref, lgb_ref, carry)
        return softmax_update(2 * i + 1, lgb_ref, lga_ref, carry)

    eq_seen0, chunk_max0 = masked_logits(0, jnp.zeros((1, Q_BLOCK), F32), lga_ref)
    carry = lax.fori_loop(0, nck // 2, attend_pair,
                          (eq_seen0, jnp.full((N_HEADS, Q_BLOCK), NEG_BIG, F32), chunk_max0))

    @pl.when(nck % 2 == 1)
    def _():
        softmax_update(nck - 1, lga_ref, None, carry)

    o_ref[0] = jnp.concatenate(
        [acc_ref[h][0:HEAD_DIM, :] / acc_ref[h][HEAD_DIM:HEAD_DIM + 1, :] for h in range(N_HEADS)],
        axis=0).T.astype(BF16)


def _attn_prompt(qT, qiT, wT, kib, kb, vTb, *, topk):
    b, _, s = qT.shape
    colT = lambda n: pl.BlockSpec((1, n, Q_BLOCK), lambda i, j: (i, 0, j))
    return pl.pallas_call(
        functools.partial(_attn_prompt_kernel, topk=topk),
        grid=(b, s // Q_BLOCK),
        in_specs=[colT(E_A), colT(E_A), colT(IDX_HEADS),
                  pl.BlockSpec((1, s, IDX_DIM), lambda i, j: (i, 0, 0)),
                  pl.BlockSpec((1, s, E_A), lambda i, j: (i, 0, 0)),
                  pl.BlockSpec((1, E_A, s), lambda i, j: (i, 0, 0))],
        out_specs=pl.BlockSpec((1, Q_BLOCK, E_A), lambda i, j: (i, j, 0)),
        out_shape=jax.ShapeDtypeStruct((b, s, E_A), BF16),
        scratch_shapes=[pltpu.VMEM((s, Q_BLOCK), F32), pltpu.VMEM((s, Q_BLOCK), BF16),
                        pltpu.VMEM((N_HEADS, ACC_ROWS, Q_BLOCK), F32),
                        pltpu.VMEM((N_HEADS, CK, Q_BLOCK), F32), pltpu.VMEM((N_HEADS, CK, Q_BLOCK), F32),
                        pltpu.VMEM((N_HEADS, LANE, Q_BLOCK), BF16), pltpu.VMEM((CK, CK), BF16)],
        compiler_params=pltpu.CompilerParams(dimension_semantics=("arbitrary", "arbitrary"),
                                             vmem_limit_bytes=VMEM_LIMIT),
        name="attn_prompt",
    )(qT, qiT, wT, kib, kb, vTb)


def _sample_scores_kernel(pt_ref, qi_ref, w_ref, kinew_ref, *refs, n_pages, t_new):
    pages = refs[:n_pages]
    out_ref = refs[n_pages]
    qi = qi_ref[0]
    w = w_ref[0][:, 0:1]

    def scores(keysT):
        t = jnp.maximum(jnp.dot(qi, keysT.astype(BF16), preferred_element_type=F32), 0.0) * w
        acc = t[0:SROWS]
        for h in range(1, IDX_HEADS):
            acc = acc + t[h * SROWS:(h + 1) * SROWS]
        return acc

    past = scores(jnp.concatenate([p[0, 0] for p in pages], axis=1))
    new = scores(kinew_ref[0])
    qrow = lax.broadcasted_iota(jnp.int32, new.shape, 0)
    kcol = lax.broadcasted_iota(jnp.int32, new.shape, 1)
    new = jnp.where(kcol <= jnp.minimum(qrow, t_new - 1), new, -jnp.inf)
    pad = jnp.full((SROWS, out_ref.shape[2] - past.shape[1] - new.shape[1]), -jnp.inf, F32)
    out_ref[0] = jnp.concatenate([past, new, pad], axis=1)


def _sample_scores(page_table, qi_h, w_h, kiT_new, cache_iT, layer, *, t_new):
    db, n_pages = page_table.shape
    page = cache_iT.shape[3]
    width = (n_pages // PAGES_PER_STEP + 1) * PAGES_PER_STEP * page

    def page_spec(i):
        return pl.BlockSpec((1, 1, IDX_DIM, page), lambda b, pt: (layer, pt[b, i], 0, 0))

    per_b = lambda *shape: pl.BlockSpec((1,) + shape, lambda b, pt: (b,) + (0,) * len(shape))
    grid_spec = pltpu.PrefetchScalarGridSpec(
        num_scalar_prefetch=1, grid=(db,),
        in_specs=[per_b(IDX_HEADS * SROWS, IDX_DIM), per_b(IDX_HEADS * SROWS, LANE), per_b(IDX_DIM, page)]
                 + [page_spec(i) for i in range(n_pages)],
        out_specs=per_b(SROWS, width))
    return pl.pallas_call(
        functools.partial(_sample_scores_kernel, n_pages=n_pages, t_new=t_new),
        grid_spec=grid_spec,
        out_shape=jax.ShapeDtypeStruct((db, SROWS, width), F32),
        compiler_params=pltpu.CompilerParams(dimension_semantics=("arbitrary",),
                                             vmem_limit_bytes=VMEM_LIMIT),
        name="sample_scores",
    )(page_table, qi_h, w_h, kiT_new, *([cache_iT] * n_pages))


def _sample_threshold_kernel(sc_ref, thr_ref, take_ref, tb_ref, *, topk):
    keys, lanes = sc_ref.shape

    def truncate_chunk(c, carry):
        off = pl.multiple_of(c * CK, CK)
        tb_ref[pl.ds(off, CK), :] = _truncate_bf16(sc_ref[pl.ds(off, CK), :])
        return carry

    lax.fori_loop(0, keys // CK, truncate_chunk, 0)
    thr, n_eq_take = _topk_threshold(sc_ref, tb_ref, keys // CK, keys // CK,
                                     jnp.full((1, lanes), topk, jnp.int32))
    thr_ref[...] = jnp.broadcast_to(thr, thr_ref.shape)
    take_ref[...] = jnp.broadcast_to(n_eq_take, take_ref.shape)


def _sample_threshold(scoresT, *, topk):
    keys, nq = scoresT.shape
    return pl.pallas_call(
        functools.partial(_sample_threshold_kernel, topk=topk),
        grid=(nq // LANE,),
        in_specs=[pl.BlockSpec((keys, LANE), lambda i: (0, i))],
        out_specs=[pl.BlockSpec((SUBLANE, LANE), lambda i: (0, i))] * 2,
        out_shape=[jax.ShapeDtypeStruct((SUBLANE, nq), F32)] * 2,
        scratch_shapes=[pltpu.VMEM((keys, LANE), BF16)],
        compiler_params=pltpu.CompilerParams(dimension_semantics=("arbitrary",),
                                             vmem_limit_bytes=VMEM_LIMIT),
        name="sample_threshold",
    )(scoresT)


def _sample_attn_kernel(pt_ref, q_ref, sc_ref, thr_ref, take_ref, kTnew_ref, vTnew_ref, *refs, nch):
    kpages = refs[:PAGES_PER_STEP]
    vpages = refs[PAGES_PER_STEP:2 * PAGES_PER_STEP]
    o_ref, m_ref, l_ref, acc_ref, seen_ref = refs[2 * PAGES_PER_STEP:]
    c = pl.program_id(1)

    @pl.when(c == 0)
    def _():
        m_ref[...] = jnp.full_like(m_ref, NEG_BIG)
        l_ref[...] = jnp.zeros_like(l_ref)
        acc_ref[...] = jnp.zeros_like(acc_ref)
        seen_ref[...] = jnp.zeros_like(seen_ref)

    thr = thr_ref[0][:, 0:1]
    take = take_ref[0][:, 0:1]

    def attend(kT, vT):
        width = kT.shape[1]
        sel, seen = _select_rows(sc_ref[0][:, 0:width], thr, take, seen_ref[:, 0:1],
                                 _tri(min(CK, width), lower=False))
        seen_ref[...] = jnp.broadcast_to(seen, seen_ref.shape)
        sel = jnp.tile(sel, (N_HEADS, 1))
        lg = jnp.where(sel, jnp.dot(q_ref[0], kT.astype(BF16), preferred_element_type=F32), -jnp.inf)
        m_old = m_ref[:, 0:1]
        m_new = jnp.maximum(m_old, jnp.max(lg, axis=1, keepdims=True))
        p = jnp.exp(lg - m_new)
        alpha = jnp.exp(m_old - m_new)
        l_new = alpha * l_ref[:, 0:1] + jnp.sum(p, axis=1, keepdims=True)
        pv = lax.dot_general(p.astype(BF16), vT.astype(BF16), (((1,), (1,)), ((), ())),
                             preferred_element_type=F32)
        acc_ref[...] = alpha * acc_ref[...] + pv
        m_ref[...] = jnp.broadcast_to(m_new, m_ref.shape)
        l_ref[...] = jnp.broadcast_to(l_new, l_ref.shape)

    def stack(pages):
        return jnp.concatenate([p[0, 0].reshape(E_A, p.shape[4]) for p in pages], axis=1)

    @pl.when(c < nch)
    def _():
        attend(stack(kpages), stack(vpages))

    @pl.when(c == nch)
    def _():
        attend(kTnew_ref[0], vTnew_ref[0])
        o = acc_ref[...] / l_ref[:, 0:1]
        o_ref[0] = jnp.concatenate(
            [o[h * SROWS:(h + 1) * SROWS, h * HEAD_DIM:(h + 1) * HEAD_DIM] for h in range(N_HEADS)], axis=1)


def _sample_attn(page_table, q_bd, scores, thr, take, kT_new, vT_new, cache_kT, cache_vT, layer):
    db, n_pages = page_table.shape
    nch = n_pages // PAGES_PER_STEP
    page = cache_kT.shape[4]
    sck = PAGES_PER_STEP * page
    hq = N_HEADS * SROWS

    def page_spec(i):
        return pl.BlockSpec(
            (1, 1, N_HEADS, HEAD_DIM, page),
            lambda b, c, pt: (layer, pt[b, jnp.minimum(c, nch - 1) * PAGES_PER_STEP + i], 0, 0, 0))

    per_b = lambda *shape: pl.BlockSpec((1,) + shape, lambda b, c, pt: (b,) + (0,) * len(shape))
    grid_spec = pltpu.PrefetchScalarGridSpec(
        num_scalar_prefetch=1, grid=(db, nch + 1),
        in_specs=[per_b(hq, E_A),
                  pl.BlockSpec((1, SROWS, sck), lambda b, c, pt: (b, 0, c)),
                  per_b(SROWS, LANE), per_b(SROWS, LANE),
                  per_b(E_A, page), per_b(E_A, page)]
                 + [page_spec(i) for i in range(PAGES_PER_STEP)] * 2,
        out_specs=per_b(SROWS, E_A),
        scratch_shapes=[pltpu.VMEM((hq, LANE), F32), pltpu.VMEM((hq, LANE), F32),
                        pltpu.VMEM((hq, E_A), F32), pltpu.VMEM((SROWS, LANE), F32)])
    return pl.pallas_call(
        functools.partial(_sample_attn_kernel, nch=nch),
        grid_spec=grid_spec,
        out_shape=jax.ShapeDtypeStruct((db, SROWS, E_A), F32),
        compiler_params=pltpu.CompilerParams(dimension_semantics=("arbitrary", "arbitrary"),
                                             vmem_limit_bytes=VMEM_LIMIT),
        name="sample_attn",
    )(page_table, q_bd, scores, thr, take, kT_new, vT_new,
      *([cache_kT] * PAGES_PER_STEP), *([cache_vT] * PAGES_PER_STEP))


def _post_kernel(x_ref, oa_ref, g0_ref, mbc_ref, wba_ref, wo_ref, wup_ref, wdn_ref,
                 gpost_ref, gfpre_ref, gfpost_ref, y_ref):
    m = g0_ref[...].astype(F32) * jnp.dot(oa_ref[...], wba_ref[...], preferred_element_type=F32) \
        + mbc_ref[...]
    y = jnp.dot(m.astype(BF16), wo_ref[...], preferred_element_type=F32)
    x1 = x_ref[...] + _rms(y, gpost_ref[...])
    a = jnp.maximum(jnp.dot(_rms(x1, gfpre_ref[...]).astype(BF16), wup_ref[...],
                            preferred_element_type=F32), 0.0)
    f = jnp.dot((a * a).astype(BF16), wdn_ref[...], preferred_element_type=F32)
    y_ref[...] = x1 + _rms(f, gfpost_ref[...])


def _post(x, oa, g0, mbc, wba, wo, wup, wdn, gpost, gfpre, gfpost, *, layer):
    m = x.shape[0]
    _const_spec = functools.partial(_layer_spec, layer)
    tm = min(POST_ROW_TILE, m)
    row = lambda n: pl.BlockSpec((tm, n), lambda i: (i, 0))
    return pl.pallas_call(
        _post_kernel, grid=(m // tm,),
        in_specs=[row(D_MODEL), row(E_A), row(D_MODEL), row(D_MODEL),
                  _const_spec((E_A, D_MODEL)), _const_spec((D_MODEL, D_MODEL)),
                  _const_spec((D_MODEL, D_FF)), _const_spec((D_FF, D_MODEL)),
                  _const_spec((1, D_MODEL)), _const_spec((1, D_MODEL)), _const_spec((1, D_MODEL))],
        out_specs=row(D_MODEL),
        out_shape=jax.ShapeDtypeStruct((m, D_MODEL), F32),
        compiler_params=pltpu.CompilerParams(dimension_semantics=("arbitrary",),
                                             vmem_limit_bytes=VMEM_LIMIT),
        name="post",
    )(x, oa, g0, mbc, wba, wo, wup, wdn, gpost, gfpre, gfpost)


def _rope_tables(pos):
    half = HEAD_DIM // 2
    inv = ROPE_THETA ** (-jnp.arange(half, dtype=F32) * (2.0 / HEAD_DIM))
    ang = pos[:, None] * inv[None, :]
    c, s = jnp.cos(ang), jnp.sin(ang)
    return jnp.tile(c, (1, LANE // half)), jnp.tile(jnp.concatenate([-s, s], axis=1), (1, LANE // HEAD_DIM))


def _heads_first(a, t):
    db = a.shape[0] // t
    a = a.reshape(db, t, N_HEADS, -1).transpose(0, 2, 1, 3)
    a = jnp.pad(a, ((0, 0), (0, 0), (0, SROWS - t), (0, 0)))
    return a.reshape(db, N_HEADS * SROWS, -1)


def _new_keys_T(a, db, t, page):
    a = jnp.pad(a.reshape(db, t, -1), ((0, 0), (0, page - t), (0, 0)))
    return a.transpose(0, 2, 1)


def kernel(x_prompt, x_sample, cache_k, cache_v, cache_idx_k, state_conv, page_table, norm_mix_pre, norm_mix_post, norm_ffn_pre, norm_ffn_post, w_in, gmlp_ln_g, gmlp_ln_b, gmlp_ws, gmlp_bs, conv_w, conv_b, w_br_attn, w_br_gmlp, w_br_conv, w_out, w_ff_up, w_ff_down):
    depth = w_in.shape[0]
    b, s, _ = x_prompt.shape
    db, t, _ = x_sample.shape
    page = cache_k.shape[2]
    n_pages = page_table.shape[1]
    past = n_pages * page
    ms = db * t
    assert s % POST_ROW_TILE == 0 and s % (2 * CK) == 0 and ms == CHUNK and CONV_W - 1 <= t <= SROWS
    assert page == LANE and n_pages % PAGES_PER_STEP == 0

    cache_kT = cache_k.transpose(0, 1, 3, 4, 2)
    cache_vT = cache_v.transpose(0, 1, 3, 4, 2)
    cache_iT = cache_idx_k.transpose(0, 1, 3, 2)

    w_in_p = jnp.concatenate(
        [w_in[:, :, :OFF_B], jnp.zeros((depth, D_MODEL, C_B - OFF_B), w_in.dtype), w_in[:, :, OFF_B:]],
        axis=2).astype(BF16)
    causal = jnp.tril(jnp.ones((CHUNK, CHUNK), bool))
    wm = jnp.where(causal[None, None], gmlp_ws, 0)
    wmix_p = wm.astype(BF16)
    bmix_p = jnp.repeat(jnp.swapaxes(gmlp_bs, 1, 2), E_B // GMLP_GROUPS, axis=2)
    eye = jnp.eye(db, dtype=wm.dtype)
    wmix_s = jnp.einsum('ab,lgts->lgatbs', eye, wm[:, :, :t, :t]).reshape(depth, GMLP_GROUPS, ms, ms)
    wmix_s = wmix_s.astype(BF16)
    bmix_s = jnp.tile(bmix_p[:, :t], (1, db, 1))
    wbb, wbc, wba = (w.astype(BF16) for w in (w_br_gmlp, w_br_conv, w_br_attn))
    wo, wup, wdn = (w.astype(BF16) for w in (w_out, w_ff_up, w_ff_down))
    row_stack = lambda a: a[:, None, :]
    common = (row_stack(norm_mix_pre), w_in_p)
    tail = (row_stack(gmlp_ln_g), row_stack(gmlp_ln_b))
    conv = (conv_w, row_stack(conv_b), wbb, wbc)
    post_w = (wba, wo, wup, wdn, row_stack(norm_mix_post), row_stack(norm_ffn_pre),
              row_stack(norm_ffn_post))
    zs = jnp.zeros((depth, db, 1, E_C), state_conv.dtype)
    st1 = jnp.concatenate([state_conv[:, :, 1:2]] + [zs] * (t - 1), axis=2)
    st2 = jnp.concatenate([state_conv[:, :, 0:1], state_conv[:, :, 1:2]] + [zs] * (t - 2), axis=2)
    st12 = jnp.concatenate([st1, st2], axis=3).reshape(depth, ms, 2 * E_C)

    cos_p, sin_p = _rope_tables(jnp.arange(s, dtype=F32))
    cos_s, sin_s = _rope_tables(jnp.tile(jnp.arange(t, dtype=F32) + past, db))
    topk_p = min(TOPK_MAX, s // 4)
    topk_s = min(TOPK_MAX, (past + t) // 4)
    head_eye = jnp.eye(N_HEADS, dtype=BF16)

    xp = x_prompt.reshape(b * s, D_MODEL)
    xs = x_sample.reshape(ms, D_MODEL)
    outs = [[] for _ in range(9)]
    stacks = tuple(jnp.zeros((depth, b, n, s), F32) for n in (E_A, E_A, IDX_DIM))
    for l in range(depth):
        (qT, qiT, wT, kb, kib, vTb, kT_all, vT_all, kiT_all, g0, mbc, ctail) = _proj(
            xp, *common, cos_p, sin_p, *tail, wmix_p, bmix_p, *conv, layer=l, sample=False, seq_len=s,
            stacks=stacks)
        stacks = (kT_all, vT_all, kiT_all)
        oa = _attn_prompt(qT, qiT, wT, kib.reshape(b, s, IDX_DIM), kb.reshape(b, s, E_A), vTb, topk=topk_p)
        xp = _post(xp, oa.reshape(b * s, E_A), g0, mbc, *post_w, layer=l)
        outs[3].append(ctail[:, SUBLANE - (CONV_W - 1):, :])
        (q_s, k_s, v_s, qi_s, kw_s, g0_s, mbc_s, cin_s, vn_s) = _proj(
            xs, *common, cos_s, sin_s, *tail, wmix_s, bmix_s, *conv, layer=l, sample=True, seq_len=t,
            st=st12)
        qi_h = _heads_first(qi_s, t)
        w_h = _heads_first(kw_s[:, IDX_DIM:IDX_DIM + IDX_HEADS][:, :, None], t)
        w_h = jnp.broadcast_to(w_h, w_h.shape[:2] + (LANE,))
        scores = _sample_scores(page_table, qi_h, w_h, _new_keys_T(kw_s[:, :IDX_DIM], db, t, page),
                                cache_iT, l, t_new=t)
        thr, take = _sample_threshold(scores[:, :t].reshape(ms, -1).T, topk=topk_s)
        per_q = lambda a, fill: jnp.broadcast_to(
            jnp.pad(a[0].reshape(db, t, 1), ((0, 0), (0, SROWS - t), (0, 0)), constant_values=fill),
            (db, SROWS, LANE))
        q4 = jnp.pad(q_s.reshape(db, t, N_HEADS, HEAD_DIM), ((0, 0), (0, SROWS - t), (0, 0), (0, 0)))
        q_bd = jnp.einsum('bqhd,hg->bhqgd', q4, head_eye).reshape(db, N_HEADS * SROWS, E_A)
        oa_s = _sample_attn(page_table, q_bd, scores, per_q(thr, NEG_BIG), per_q(take, 0.0),
                            _new_keys_T(k_s, db, t, page), _new_keys_T(v_s, db, t, page),
                            cache_kT, cache_vT, l)
        oa_s = oa_s[:, :t].reshape(ms, E_A).astype(BF16)
        xs = _post(xs, oa_s, g0_s, mbc_s, *post_w, layer=l)
        outs[4].append(k_s.reshape(db, t, N_HEADS, HEAD_DIM))
        outs[5].append(v_s.reshape(db, t, N_HEADS, HEAD_DIM))
        outs[6].append(kw_s[:, :IDX_DIM].reshape(db, t, IDX_DIM))
        outs[7].append(cin_s.reshape(db, t, E_C)[:, t - (CONV_W - 1):])
        outs[8].append(vn_s.reshape(db, t, E_B))
    kT_all, vT_all, kiT_all = stacks
    stacked = [None] * 3 + [jnp.stack(o) for o in outs[3:]]
    per_head = (depth, b, N_HEADS, HEAD_DIM, s)
    stacked[0] = kT_all.reshape(per_head).transpose(0, 1, 4, 2, 3)
    stacked[1] = vT_all.reshape(per_head).transpose(0, 1, 4, 2, 3)
    stacked[2] = kiT_all.transpose(0, 1, 3, 2)
    return (xp.reshape(b, s, D_MODEL), xs.reshape(db, t, D_MODEL)) + tuple(stacked)
```
